```python
import math
import jax, jax.numpy as jnp
from jax import lax
import numpy as np

D_MODEL = 1024
BATCH = 2
SEQ = 8192
DEPTH = 4
DEC_BATCH = 32
DEC_SEQ = 1
PAST_LEN = 8192
PAGE_SIZE = 128

N_MIXERS = 3
N_LAYERS_A = (DEPTH + 2) // 3
N_LAYERS_B = (DEPTH + 1) // 3
N_LAYERS_C = DEPTH // 3

D_RNN = 1280
LRU_BLOCKS = 10
LRU_BLOCK = D_RNN // LRU_BLOCKS
LRU_CONV = 4
LRU_C = 8.0

ATTN_GROUPS = ((128, 1), (512, 4), (2048, 16))
N_ATTN_GROUPS = len(ATTN_GROUPS)
HEADS_PER_GROUP = 8
HEAD_DIM = D_MODEL // HEADS_PER_GROUP
ROT_DIM = HEAD_DIM // 4
ROPE_THETA = 500000.0
QKV_WIDTH = 3 * N_ATTN_GROUPS * HEADS_PER_GROUP * HEAD_DIM

D_CONF = D_MODEL
CONF_WIDTH = 31

N_EGROUPS = 4
EXPERTS_PER_GROUP = 4
N_EXPERTS = N_EGROUPS * EXPERTS_PER_GROUP
D_EXPERT = 512
TOP_K = 2

NORM_EPS = 1e-6

kernel_name = "hybrid_rglru_dilattn_conformer_hmoe_step"


def rms_norm(x, g):
    xf = x.astype(jnp.float32)
    y = xf * lax.rsqrt(jnp.mean(xf * xf, axis=-1, keepdims=True) + NORM_EPS)
    return (y * g.astype(jnp.float32)).astype(x.dtype)


def layer_norm(x, g, b):
    xf = x.astype(jnp.float32)
    mu = jnp.mean(xf, axis=-1, keepdims=True)
    xc = xf - mu
    var = jnp.mean(xc * xc, axis=-1, keepdims=True)
    return (xc * lax.rsqrt(var + NORM_EPS) * g.astype(jnp.float32) + b.astype(jnp.float32)).astype(x.dtype)


def ada_mod(c, w, b):
    m = jax.nn.silu(c) @ w + b
    return jnp.split(m[:, None, :], 6, axis=-1)


def modulate(h, shift, scale):
    return h * (1.0 + scale) + shift


def apply_rope(x, pos):
    half = ROT_DIM // 2
    inv_freq = jnp.exp(jnp.arange(half, dtype=jnp.float32) * (-2.0 * math.log(ROPE_THETA) / ROT_DIM))
    ang = pos.astype(jnp.float32)[:, None] * inv_freq[None, :]
    cos = jnp.cos(ang)[None, :, None, :]
    sin = jnp.sin(ang)[None, :, None, :]
    xr = x[..., :ROT_DIM].astype(jnp.float32)
    x1, x2 = xr[..., :half], xr[..., half:]
    rot = jnp.concatenate([x1 * cos - x2 * sin, x2 * cos + x1 * sin], axis=-1)
    return jnp.concatenate([rot.astype(x.dtype), x[..., ROT_DIM:]], axis=-1)


def causal_dwconv(hist, x, w, b):
    width = w.shape[0]
    xx = jnp.concatenate([hist.astype(x.dtype), x], axis=1)
    y = lax.conv_general_dilated(xx, w[:, None, :].astype(x.dtype), window_strides=(1,), padding="VALID",
                                 dimension_numbers=("NWC", "WIO", "NWC"), feature_group_count=x.shape[-1])
    return y + b, xx[:, xx.shape[1] - (width - 1):]


def recurrent_block(u, conv_hist, h_prev, w_in_y, w_in_x, conv_w, conv_b, gate_r_w, gate_r_b,
                    gate_i_w, gate_i_b, lru_lambda, w_out):
    bsz, t, _ = u.shape
    y_br = jax.nn.gelu(u @ w_in_y)
    xc, new_hist = causal_dwconv(conv_hist, u @ w_in_x, conv_w, conv_b)
    xblk = xc.reshape(bsz, t, LRU_BLOCKS, LRU_BLOCK)
    r = jax.nn.sigmoid(jnp.einsum("btnk,nkj->btnj", xblk, gate_r_w).reshape(bsz, t, D_RNN) + gate_r_b)
    i = jax.nn.sigmoid(jnp.einsum("btnk,nkj->btnj", xblk, gate_i_w).reshape(bsz, t, D_RNN) + gate_i_b)
    log_a = -LRU_C * r.astype(jnp.float32) * jax.nn.softplus(-lru_lambda.astype(jnp.float32))
    a = jnp.exp(log_a)
    bterm = jnp.sqrt(-jnp.expm1(2.0 * log_a)) * (i * xc).astype(jnp.float32)
    bterm = bterm.at[:, 0].add(a[:, 0] * h_prev.astype(jnp.float32))

    def combine(left, right):
        return (left[0] * right[0], right[0] * left[1] + right[1])

    _, hs = lax.associative_scan(combine, (a, bterm), axis=1)
    out = (hs.astype(u.dtype) * y_br) @ w_out
    return out, new_hist, hs[:, -1].astype(h_prev.dtype)


def softmax_with_lse(s, valid):
    s = jnp.where(valid, s.astype(jnp.float32), -jnp.inf)
    m = jnp.max(s, axis=-1, keepdims=True)
    e = jnp.exp(s - m)
    l = jnp.sum(e, axis=-1, keepdims=True)
    return e / l, (m + jnp.log(l))[..., 0]


def dilated_prompt(q, k, v, win, dil):
    bsz, t, nh, hd = q.shape
    n = win // dil
    span = n * dil
    tp = -(-t // span) * span
    lr = tp // dil
    nb = lr // n

    def to_blocks(a):
        a = jnp.pad(a, ((0, 0), (0, tp - t), (0, 0), (0, 0)))
        a = a.reshape(bsz, lr, dil, nh, hd).transpose(0, 2, 1, 3, 4)
        return a.reshape(bsz, dil, nb, n, nh, hd)

    def with_prev(a):
        prev = jnp.pad(a, ((0, 0), (0, 0), (1, 0), (0, 0), (0, 0), (0, 0)))[:, :, :nb]
        return jnp.concatenate([prev, a], axis=3)

    qb = to_blocks(q)
    kb = with_prev(to_blocks(k))
    vb = with_prev(to_blocks(v))
    s = jnp.einsum("bdcqhe,bdckhe->bdchqk", qb, kb)
    qi = jnp.arange(n)[:, None]
    kj = jnp.arange(2 * n)[None, :]
    dist = qi + n - kj
    blk = jnp.arange(nb)[:, None, None]
    valid = (dist >= 0) & (dist <= n) & (blk * n + kj - n >= 0)
    p, lse = softmax_with_lse(s, valid[None, None, :, None])
    o = jnp.einsum("bdchqk,bdckhe->bdcqhe", p.astype(v.dtype), vb)
    o = o.reshape(bsz, dil, lr, nh, hd).transpose(0, 2, 1, 3, 4).reshape(bsz, tp, nh, hd)[:, :t]
    lse = lse.transpose(0, 1, 2, 4, 3).reshape(bsz, dil, lr, nh).transpose(0, 2, 1, 3).reshape(bsz, tp, nh)[:, :t]
    return o, lse


def dilated_step(q, k_all, v_all, hist_len, win, dil):
    n = win // dil
    s_len = q.shape[1]
    idx = hist_len + jnp.arange(s_len)[:, None] - dil * jnp.arange(n + 1)[None, :]
    valid = idx >= 0
    idx = jnp.maximum(idx, 0)
    kg = k_all[:, idx]
    vg = v_all[:, idx]
    s = jnp.einsum("bqhe,bqkhe->bqhk", q, kg)
    p, lse = softmax_with_lse(s, valid[None, :, None, :])
    o = jnp.einsum("bqhk,bqkhe->bqhe", p.astype(v_all.dtype), vg)
    return o, lse


def dilated_attention(u, pos, kv_cache, w_qkv, w_o):
    bsz, t, _ = u.shape
    qkv = (u @ w_qkv).reshape(bsz, t, 3, N_ATTN_GROUPS, HEADS_PER_GROUP, HEAD_DIM)
    outs, lses, new_kv = [], [], []
    for g, (win, dil) in enumerate(ATTN_GROUPS):
        q = apply_rope(qkv[:, :, 0, g], pos) * (HEAD_DIM ** -0.5)
        k = apply_rope(qkv[:, :, 1, g], pos)
        v = qkv[:, :, 2, g]
        if kv_cache is None:
            o, lse = dilated_prompt(q, k, v, win, dil)
            keep = min(win, t)
            new_kv.append((k[:, t - keep:], v[:, t - keep:]))
        else:
            kc, vc = kv_cache[g]
            o, lse = dilated_step(q, jnp.concatenate([kc.astype(k.dtype), k], axis=1),
                                  jnp.concatenate([vc.astype(v.dtype), v], axis=1), kc.shape[1], win, dil)
            new_kv.append((k, v))
        outs.append(o)
        lses.append(lse)
    wgt = jax.nn.softmax(jnp.stack(lses), axis=0)
    o = jnp.sum(wgt[..., None] * jnp.stack(outs).astype(jnp.float32), axis=0).astype(u.dtype)
    return o.reshape(bsz, t, HEADS_PER_GROUP * HEAD_DIM) @ w_o, new_kv


def conformer_conv(u, hist, w_pw1, b_pw1, w_dw, b_dw, ln_g, ln_b, w_pw2, b_pw2):
    z = u @ w_pw1 + b_pw1
    glu = z[..., :D_CONF] * jax.nn.sigmoid(z[..., D_CONF:])
    y, new_hist = causal_dwconv(hist, glu, w_dw, b_dw)
    y = jax.nn.silu(layer_norm(y, ln_g, ln_b))
    return y @ w_pw2 + b_pw2, new_hist


def hier_moe(u, w_grouter, b_grouter, w_erouter, b_erouter, w_gate, w_up, w_down):
    bsz, t, _ = u.shape
    glog = (u @ w_grouter).astype(jnp.float32) + b_grouter.astype(jnp.float32)
    gsel = jnp.argmax(glog, axis=-1)
    gp = jnp.max(jax.nn.softmax(glog, axis=-1), axis=-1, keepdims=True)
    elog = ((u @ w_erouter).astype(jnp.float32) + b_erouter.astype(jnp.float32)).reshape(
        bsz, t, N_EGROUPS, EXPERTS_PER_GROUP)
    elog_sel = jnp.einsum("btge,btg->bte", elog, jax.nn.one_hot(gsel, N_EGROUPS, dtype=jnp.float32))
    top_v, top_i = lax.top_k(elog_sel, TOP_K)
    top_w = jax.nn.softmax(top_v, axis=-1) * gp
    expert_id = gsel[..., None] * EXPERTS_PER_GROUP + top_i
    combine = jnp.einsum("btk,btke->bte", top_w, jax.nn.one_hot(expert_id, N_EXPERTS, dtype=jnp.float32))
    hg = jnp.einsum("btd,edf->btef", u, w_gate)
    hu = jnp.einsum("btd,edf->btef", u, w_up)
    act = jax.nn.silu(hg) * hu * combine[..., None].astype(u.dtype)
    return jnp.einsum("btef,efd->btd", act, w_down)


def setup_inputs(seed: int = 0) -> dict:
    key = jax.random.key(seed)
    ks = list(jax.random.split(key, 64))

    def nrm(shape, scale):
        return jax.random.normal(ks.pop(), shape, jnp.float32) * scale

    win_len = [min(w, PAST_LEN) for w, _ in ATTN_GROUPS]
    d = D_MODEL
    inp = {}
    inp["x_prompt"] = nrm((BATCH, SEQ, d), 1.0)
    inp["x_sample"] = nrm((DEC_BATCH, DEC_SEQ, d), 1.0)
    inp["c_prompt"] = nrm((BATCH, d), 1.0)
    inp["c_sample"] = nrm((DEC_BATCH, d), 1.0)
    inp["state_a_conv"] = nrm((N_LAYERS_A, DEC_BATCH, LRU_CONV - 1, D_RNN), 1.0)
    inp["state_a_h"] = nrm((N_LAYERS_A, DEC_BATCH, D_RNN), 0.5)
    for g in range(N_ATTN_GROUPS):
        inp["cache_b_k%d" % g] = nrm((N_LAYERS_B, DEC_BATCH, win_len[g], HEADS_PER_GROUP, HEAD_DIM), 1.0)
        inp["cache_b_v%d" % g] = nrm((N_LAYERS_B, DEC_BATCH, win_len[g], HEADS_PER_GROUP, HEAD_DIM), 1.0)
    inp["state_c_conv"] = nrm((N_LAYERS_C, DEC_BATCH, CONF_WIDTH - 1, D_CONF), 0.5)
    inp["norm_mix_g"] = 1.0 + nrm((DEPTH, d), 0.02)
    inp["norm_ffn_g"] = 1.0 + nrm((DEPTH, d), 0.02)
    inp["ada_w"] = nrm((DEPTH, d, 6 * d), 0.5 * d ** -0.5)
    inp["ada_b"] = nrm((DEPTH, 6 * d), 0.02)
    inp["final_norm_g"] = 1.0 + nrm((d,), 0.02)
    inp["a_w_in_y"] = nrm((N_LAYERS_A, d, D_RNN), d ** -0.5)
    inp["a_w_in_x"] = nrm((N_LAYERS_A, d, D_RNN), d ** -0.5)
    inp["a_conv_w"] = nrm((N_LAYERS_A, LRU_CONV, D_RNN), LRU_CONV ** -0.5)
    inp["a_conv_b"] = nrm((N_LAYERS_A, D_RNN), 0.02)
    inp["a_gate_r_w"] = nrm((N_LAYERS_A, LRU_BLOCKS, LRU_BLOCK, LRU_BLOCK), LRU_BLOCK ** -0.5)
    inp["a_gate_r_b"] = nrm((N_LAYERS_A, D_RNN), 0.02)
    inp["a_gate_i_w"] = nrm((N_LAYERS_A, LRU_BLOCKS, LRU_BLOCK, LRU_BLOCK), LRU_BLOCK ** -0.5)
    inp["a_gate_i_b"] = nrm((N_LAYERS_A, D_RNN), 0.02)
    a_c = jax.random.uniform(ks.pop(), (N_LAYERS_A, D_RNN), jnp.float32, 0.9, 0.999) ** (1.0 / LRU_C)
    inp["a_lambda"] = jnp.log(a_c) - jnp.log1p(-a_c)
    inp["a_w_out"] = nrm((N_LAYERS_A, D_RNN, d), D_RNN ** -0.5)
    inp["b_w_qkv"] = nrm((N_LAYERS_B, d, QKV_WIDTH), d ** -0.5)
    inp["b_w_o"] = nrm((N_LAYERS_B, HEADS_PER_GROUP * HEAD_DIM, d), (HEADS_PER_GROUP * HEAD_DIM) ** -0.5)
    inp["conf_w_pw1"] = nrm((N_LAYERS_C, d, 2 * D_CONF), d ** -0.5)
    inp["conf_b_pw1"] = nrm((N_LAYERS_C, 2 * D_CONF), 0.02)
    inp["conf_w_dw"] = nrm((N_LAYERS_C, CONF_WIDTH, D_CONF), CONF_WIDTH ** -0.5)
    inp["conf_b_dw"] = nrm((N_LAYERS_C, D_CONF), 0.02)
    inp["conf_ln_g"] = 1.0 + nrm((N_LAYERS_C, D_CONF), 0.02)
    inp["conf_ln_b"] = nrm((N_LAYERS_C, D_CONF), 0.02)
    inp["conf_w_pw2"] = nrm((N_LAYERS_C, D_CONF, d), D_CONF ** -0.5)
    inp["conf_b_pw2"] = nrm((N_LAYERS_C, d), 0.02)
    inp["moe_w_grouter"] = nrm((DEPTH, d, N_EGROUPS), d ** -0.5)
    inp["moe_b_grouter"] = nrm((DEPTH, N_EGROUPS), 0.01)
    inp["moe_w_erouter"] = nrm((DEPTH, d, N_EXPERTS), d ** -0.5)
    inp["moe_b_erouter"] = nrm((DEPTH, N_EXPERTS), 0.01)
    inp["moe_w_gate"] = nrm((DEPTH, N_EXPERTS, d, D_EXPERT), d ** -0.5)
    inp["moe_w_up"] = nrm((DEPTH, N_EXPERTS, d, D_EXPERT), d ** -0.5)
    inp["moe_w_down"] = nrm((DEPTH, N_EXPERTS, D_EXPERT, d), D_EXPERT ** -0.5)
    return inp


def reference(x_prompt, x_sample, c_prompt, c_sample, state_a_conv, state_a_h,
              cache_b_k0, cache_b_v0, cache_b_k1, cache_b_v1, cache_b_k2, cache_b_v2, state_c_conv,
              norm_mix_g, norm_ffn_g, ada_w, ada_b, final_norm_g,
              a_w_in_y, a_w_in_x, a_conv_w, a_conv_b, a_gate_r_w, a_gate_r_b, a_gate_i_w, a_gate_i_b,
              a_lambda, a_w_out,
              b_w_qkv, b_w_o,
              conf_w_pw1, conf_b_pw1, conf_w_dw, conf_b_dw, conf_ln_g, conf_ln_b, conf_w_pw2, conf_b_pw2,
              moe_w_grouter, moe_b_grouter, moe_w_erouter, moe_b_erouter, moe_w_gate, moe_w_up, moe_w_down):
    bsz, seq = x_prompt.shape[:2]
    dseq = x_sample.shape[1]
    pos_p = jnp.arange(seq, dtype=jnp.int32)
    pos_s = PAST_LEN + jnp.arange(dseq, dtype=jnp.int32)
    cache_b_k = (cache_b_k0, cache_b_k1, cache_b_k2)
    cache_b_v = (cache_b_v0, cache_b_v1, cache_b_v2)

    xp, xs = x_prompt, x_sample
    a_conv_p, a_conv_s, a_h_p, a_h_s = [], [], [], []
    kp = [[] for _ in ATTN_GROUPS]
    vp = [[] for _ in ATTN_GROUPS]
    ksm = [[] for _ in ATTN_GROUPS]
    vsm = [[] for _ in ATTN_GROUPS]
    cf_p, cf_s = [], []

    for i in range(DEPTH):
        kind, j = i % N_MIXERS, i // N_MIXERS
        mp = ada_mod(c_prompt, ada_w[i], ada_b[i])
        ms = ada_mod(c_sample, ada_w[i], ada_b[i])
        hp = modulate(rms_norm(xp, norm_mix_g[i]), mp[0], mp[1])
        hs = modulate(rms_norm(xs, norm_mix_g[i]), ms[0], ms[1])
        if kind == 0:
            wts = (a_w_in_y[j], a_w_in_x[j], a_conv_w[j], a_conv_b[j], a_gate_r_w[j], a_gate_r_b[j],
                   a_gate_i_w[j], a_gate_i_b[j], a_lambda[j], a_w_out[j])
            op, cp_new, hp_new = recurrent_block(hp, jnp.zeros((bsz, LRU_CONV - 1, D_RNN), hp.dtype),
                                                 jnp.zeros((bsz, D_RNN), state_a_h.dtype), *wts)
            os_, cs_new, hs_new = recurrent_block(hs, state_a_conv[j], state_a_h[j], *wts)
            a_conv_p.append(cp_new)
            a_conv_s.append(cs_new)
            a_h_p.append(hp_new)
            a_h_s.append(hs_new)
        elif kind == 1:
            op, kv_p = dilated_attention(hp, pos_p, None, b_w_qkv[j], b_w_o[j])
            past = tuple((cache_b_k[g][j], cache_b_v[g][j]) for g in range(N_ATTN_GROUPS))
            os_, kv_s = dilated_attention(hs, pos_s, past, b_w_qkv[j], b_w_o[j])
            for g in range(N_ATTN_GROUPS):
                kp[g].append(kv_p[g][0])
                vp[g].append(kv_p[g][1])
                ksm[g].append(kv_s[g][0])
                vsm[g].append(kv_s[g][1])
        else:
            wts = (conf_w_pw1[j], conf_b_pw1[j], conf_w_dw[j], conf_b_dw[j], conf_ln_g[j], conf_ln_b[j],
                   conf_w_pw2[j], conf_b_pw2[j])
            op, hist_p = conformer_conv(hp, jnp.zeros((bsz, CONF_WIDTH - 1, D_CONF), hp.dtype), *wts)
            os_, hist_s = conformer_conv(hs, state_c_conv[j], *wts)
            cf_p.append(hist_p)
            cf_s.append(hist_s)
        xp = xp + mp[2] * op
        xs = xs + ms[2] * os_
        moe_w = (moe_w_grouter[i], moe_b_grouter[i], moe_w_erouter[i], moe_b_erouter[i],
                 moe_w_gate[i], moe_w_up[i], moe_w_down[i])
        xp = xp + mp[5] * hier_moe(modulate(rms_norm(xp, norm_ffn_g[i]), mp[3], mp[4]), *moe_w)
        xs = xs + ms[5] * hier_moe(modulate(rms_norm(xs, norm_ffn_g[i]), ms[3], ms[4]), *moe_w)

    y_prompt = rms_norm(xp, final_norm_g)
    y_sample = rms_norm(xs, final_norm_g)
    return (y_prompt, y_sample,
            jnp.stack(a_conv_p), jnp.stack(a_conv_s), jnp.stack(a_h_p), jnp.stack(a_h_s),
            jnp.stack(kp[0]), jnp.stack(ksm[0]), jnp.stack(vp[0]), jnp.stack(vsm[0]),
            jnp.stack(kp[1]), jnp.stack(ksm[1]), jnp.stack(vp[1]), jnp.stack(vsm[1]),
            jnp.stack(kp[2]), jnp.stack(ksm[2]), jnp.stack(vp[2]), jnp.stack(vsm[2]),
            jnp.stack(cf_p), jnp.stack(cf_s))
```

```python
import functools
import math

import jax
import jax.numpy as jnp
from jax import lax
from jax.experimental import pallas as pl
from jax.experimental.pallas import tpu as pltpu

F32 = jnp.float32
BF16 = jnp.bfloat16
HIGHEST = lax.Precision.HIGHEST

D = 1024
D_RNN = 1280
LRU_BLOCKS = 10
LRU_BLOCK = 128
LRU_CONV = 4
LRU_C = 8.0
ATTN_GROUPS = ((128, 1), (512, 4), (2048, 16))
N_GROUPS = 3
N_HEADS = 8
HEAD_DIM = 128
ROT_DIM = 32
ROPE_THETA = 500000.0
QKV_WIDTH = 9 * D
CONF_WIDTH = 31
N_EGROUPS = 4
EXPERTS_PER_GROUP = 4
N_EXPERTS = 16
D_EXPERT = 512
N_PAIRS = 6
N_CLASSES = N_EGROUPS * N_PAIRS
NORM_EPS = 1e-6
PAST_LEN = 8192

LANES = 128
SUBLANES = 8
ATT_BLOCK = 128
MOE_TILE = 256
VMEM_LIMIT = 56 * 1024 * 1024


def _cparams(sem):
    return pltpu.CompilerParams(dimension_semantics=sem, vmem_limit_bytes=VMEM_LIMIT)


def _sigmoid(x):
    return 1.0 / (1.0 + jnp.exp(-x))


def _silu(x):
    return x * _sigmoid(x)


def _gelu_tanh(x):
    return 0.5 * x * (1.0 + jnp.tanh(math.sqrt(2.0 / math.pi) * (x + 0.044715 * (x * x * x))))


def _softplus(x):
    return jnp.maximum(x, 0.0) + jnp.log1p(jnp.exp(-jnp.abs(x)))


def _bdot(a, b):
    return jnp.dot(a.astype(BF16), b.astype(BF16), preferred_element_type=F32)


def _norm_mod(x, g, shift, scale):
    ms = jnp.mean(x * x, axis=-1, keepdims=True)
    y = x * lax.rsqrt(ms + NORM_EPS) * g
    return y * (1.0 + scale) + shift


def _ada_kernel(c_ref, w_ref, b_ref, o_ref):
    c = c_ref[...]
    o_ref[...] = jnp.dot(_silu(c), w_ref[...], preferred_element_type=F32, precision=HIGHEST) + b_ref[...]


def _ada_mod(c_all, ada_w, ada_b):
    depth, _, n6 = ada_w.shape
    rows = c_all.shape[0]
    tn = 1536
    return pl.pallas_call(
        _ada_kernel,
        out_shape=jax.ShapeDtypeStruct((depth, rows, n6), F32),
        grid=(depth, n6 // tn),
        in_specs=[pl.BlockSpec((rows, D), lambda l, j: (0, 0)),
                  pl.BlockSpec((None, D, tn), lambda l, j: (l, 0, j)),
                  pl.BlockSpec((None, 1, tn), lambda l, j: (l, 0, j))],
        out_specs=pl.BlockSpec((None, rows, tn), lambda l, j: (l, 0, j)),
        compiler_params=_cparams(("arbitrary", "arbitrary")),
        name="ada_mod",
    )(c_all, ada_w, ada_b.reshape(depth, 1, n6))


def _mod_spec(tm, per_row, tiles_per_seq, ngrid):
    if per_row:
        if ngrid == 1:
            return pl.BlockSpec((tm, D), lambda i: (i, 0))
        return pl.BlockSpec((tm, D), lambda i, j: (i, 0))
    if ngrid == 1:
        return pl.BlockSpec((None, 1, D), lambda i: (i // tiles_per_seq, 0, 0))
    return pl.BlockSpec((None, 1, D), lambda i, j: (i // tiles_per_seq, 0, 0))


def _row_tile(rows):
    return 1024 if rows % 1024 == 0 else rows


def _proj_a_kernel(x_ref, g_ref, sh_ref, sc_ref, wy_ref, wx_ref, y_ref, xi_ref, u_ref):
    @pl.when(pl.program_id(1) == 0)
    def _():
        u_ref[...] = _norm_mod(x_ref[...], g_ref[...], sh_ref[...], sc_ref[...]).astype(BF16)

    u = u_ref[...]
    y_ref[...] = _gelu_tanh(jnp.dot(u, wy_ref[...].astype(BF16), preferred_element_type=F32))
    xi_ref[...] = jnp.dot(u, wx_ref[...].astype(BF16), preferred_element_type=F32)


def _proj_a(x, g, shift, scale, w_y, w_x, per_row, seq_len):
    rows = x.shape[0]
    tm = _row_tile(rows)
    tn = 256
    tps = max(seq_len // tm, 1)
    mod = _mod_spec(tm, per_row, tps, 2)
    return pl.pallas_call(
        _proj_a_kernel,
        out_shape=(jax.ShapeDtypeStruct((rows, D_RNN), F32), jax.ShapeDtypeStruct((rows, D_RNN), F32)),
        grid=(rows // tm, D_RNN // tn),
        in_specs=[pl.BlockSpec((tm, D), lambda i, j: (i, 0)),
                  pl.BlockSpec((1, D), lambda i, j: (0, 0)),
                  mod, mod,
                  pl.BlockSpec((D, tn), lambda i, j: (0, j)),
                  pl.BlockSpec((D, tn), lambda i, j: (0, j))],
        out_specs=(pl.BlockSpec((tm, tn), lambda i, j: (i, j)),
                   pl.BlockSpec((tm, tn), lambda i, j: (i, j))),
        scratch_shapes=[pltpu.VMEM((tm, D), BF16)],
        compiler_params=_cparams(("arbitrary", "arbitrary")),
        name="proj_a",
    )(x, g, shift, scale, w_y, w_x)


def _scan_rows(a, b):
    n = a.shape[0]
    row = lax.broadcasted_iota(jnp.int32, a.shape, 0)
    d = 1
    while d < n:
        keep = row >= d
        a_sh = jnp.where(keep, pltpu.roll(a, d, 0), 1.0)
        b_sh = jnp.where(keep, pltpu.roll(b, d, 0), 0.0)
        b = a * b_sh + b
        a = a * a_sh
        d *= 2
    return a, b


def _lru_gates(xc, grw, grb, giw, gib, lam):
    r = _sigmoid(_bdot(xc, grw) + grb)
    i = _sigmoid(_bdot(xc, giw) + gib)
    log_a = -LRU_C * r * _softplus(-lam)
    a = jnp.exp(log_a)
    b = jnp.sqrt(jnp.tanh(-log_a) * (a * a + 1.0)) * (i * xc)
    return a, b


def _rglru_kernel(xin_ref, ybr_ref, x_ref, gate_ref, cw_ref, cb_ref, grw_ref, grb_ref, giw_ref, gib_ref,
                  lam_ref, wo_ref, xo_ref, hl_ref, xx_ref, hc_ref, hy_ref, wobf_ref, *, tt):
    b = pl.program_id(0)
    t = pl.program_id(1)

    @pl.when((b == 0) & (t == 0))
    def _():
        wobf_ref[...] = wo_ref[...].astype(BF16)

    @pl.when(t == 0)
    def _():
        xx_ref[0:SUBLANES, :] = jnp.zeros((SUBLANES, D_RNN), F32)
        hc_ref[...] = jnp.zeros((1, D_RNN), F32)

    @pl.when(t > 0)
    def _():
        xx_ref[0:SUBLANES, :] = xx_ref[tt:tt + SUBLANES, :]

    xx_ref[SUBLANES:tt + SUBLANES, :] = xin_ref[...]
    for n in range(LRU_BLOCKS):
        sl = slice(n * LRU_BLOCK, (n + 1) * LRU_BLOCK)
        xc = cb_ref[:, sl]
        for k in range(LRU_CONV):
            off = SUBLANES - (LRU_CONV - 1) + k
            xc = xc + cw_ref[k:k + 1, sl] * xx_ref[off:off + tt, sl]
        a, bt = _lru_gates(xc, grw_ref[n], grb_ref[:, sl], giw_ref[n], gib_ref[:, sl], lam_ref[:, sl])
        a_cum, h = _scan_rows(a, bt)
        h = h + a_cum * hc_ref[:, sl]
        hc_ref[:, sl] = h[tt - 1:tt, :]
        hy_ref[:, sl] = (h * ybr_ref[:, sl]).astype(BF16)
    out = jnp.dot(hy_ref[...], wobf_ref[...], preferred_element_type=F32)
    xo_ref[...] = x_ref[...] + gate_ref[...] * out
    hl_ref[...] = hc_ref[...]


def _rglru_prompt(xin, ybr, x, gate, cw, cb, grw, grb, giw, gib, lam, wo):
    bsz, seq, _ = xin.shape
    tt = 512
    full2 = lambda b, t: (0, 0)
    full3 = lambda b, t: (0, 0, 0)
    return pl.pallas_call(
        functools.partial(_rglru_kernel, tt=tt),
        out_shape=(jax.ShapeDtypeStruct((bsz, seq, D), F32), jax.ShapeDtypeStruct((bsz, 1, D_RNN), F32)),
        grid=(bsz, seq // tt),
        in_specs=[pl.BlockSpec((None, tt, D_RNN), lambda b, t: (b, t, 0)),
                  pl.BlockSpec((None, tt, D_RNN), lambda b, t: (b, t, 0)),
                  pl.BlockSpec((None, tt, D), lambda b, t: (b, t, 0)),
                  pl.BlockSpec((None, 1, D), lambda b, t: (b, 0, 0)),
                  pl.BlockSpec((LRU_CONV, D_RNN), full2),
                  pl.BlockSpec((1, D_RNN), full2),
                  pl.BlockSpec((LRU_BLOCKS, LRU_BLOCK, LRU_BLOCK), full3),
                  pl.BlockSpec((1, D_RNN), full2),
                  pl.BlockSpec((LRU_BLOCKS, LRU_BLOCK, LRU_BLOCK), full3),
                  pl.BlockSpec((1, D_RNN), full2),
                  pl.BlockSpec((1, D_RNN), full2),
                  pl.BlockSpec((D_RNN, D), full2)],
        out_specs=(pl.BlockSpec((None, tt, D), lambda b, t: (b, t, 0)),
                   pl.BlockSpec((None, 1, D_RNN), lambda b, t: (b, 0, 0))),
        scratch_shapes=[pltpu.VMEM((tt + SUBLANES, D_RNN), F32),
                        pltpu.VMEM((1, D_RNN), F32),
                        pltpu.VMEM((tt, D_RNN), BF16),
                        pltpu.VMEM((D_RNN, D), BF16)],
        compiler_params=_cparams(("arbitrary", "arbitrary")),
        name="rglru_prompt",
    )(xin, ybr, x, gate, cw, cb, grw, grb, giw, gib, lam, wo)


def _rglru_step_kernel(xin_ref, ybr_ref, x_ref, gate_ref, hist_ref, hprev_ref, cw_ref, cb_ref, grw_ref, grb_ref,
                       giw_ref, gib_ref, lam_ref, wo_ref, xo_ref, hn_ref, hy_ref):
    for n in range(LRU_BLOCKS):
        sl = slice(n * LRU_BLOCK, (n + 1) * LRU_BLOCK)
        xc = cb_ref[:, sl] + cw_ref[LRU_CONV - 1:LRU_CONV, sl] * xin_ref[:, sl]
        for k in range(LRU_CONV - 1):
            xc = xc + cw_ref[k:k + 1, sl] * hist_ref[:, k * D_RNN + n * LRU_BLOCK:k * D_RNN + (n + 1) * LRU_BLOCK]
        a, bt = _lru_gates(xc, grw_ref[n], grb_ref[:, sl], giw_ref[n], gib_ref[:, sl], lam_ref[:, sl])
        h = a * hprev_ref[:, sl] + bt
        hn_ref[:, sl] = h
        hy_ref[:, sl] = (h * ybr_ref[:, sl]).astype(BF16)
    out = jnp.dot(hy_ref[...], wo_ref[...].astype(BF16), preferred_element_type=F32)
    xo_ref[...] = x_ref[...] + gate_ref[...] * out


def _rglru_step(xin, ybr, x, gate, hist, hprev, cw, cb, grw, grb, giw, gib, lam, wo):
    rows = x.shape[0]
    return pl.pallas_call(
        _rglru_step_kernel,
        out_shape=(jax.ShapeDtypeStruct((rows, D), F32), jax.ShapeDtypeStruct((rows, D_RNN), F32)),
        scratch_shapes=[pltpu.VMEM((rows, D_RNN), BF16)],
        compiler_params=pltpu.CompilerParams(vmem_limit_bytes=VMEM_LIMIT),
        name="rglru_step",
    )(xin, ybr, x, gate, hist, hprev, cw, cb, grw, grb, giw, gib, lam, wo)


def _rope_table_kernel(cos_ref, sin_ref):
    half = ROT_DIM // 2
    shape = cos_ref.shape
    pos = lax.broadcasted_iota(jnp.int32, shape, 0).astype(F32)
    lane = lax.broadcasted_iota(jnp.int32, shape, 1)
    fidx = jnp.where(lane < half, lane, lane - half).astype(F32)
    inv_freq = jnp.exp(fidx * (-2.0 * math.log(ROPE_THETA) / ROT_DIM))
    ang = pos * inv_freq
    rot = lane < ROT_DIM
    cos_ref[...] = jnp.where(rot, jnp.cos(ang), 1.0)
    sin_ref[...] = jnp.where(rot, jnp.where(lane < half, -jnp.sin(ang), jnp.sin(ang)), 0.0)


def _rope_table(n_pos):
    return pl.pallas_call(
        _rope_table_kernel,
        out_shape=(jax.ShapeDtypeStruct((n_pos, HEAD_DIM), F32), jax.ShapeDtypeStruct((n_pos, HEAD_DIM), F32)),
        compiler_params=pltpu.CompilerParams(vmem_limit_bytes=VMEM_LIMIT),
        name="rope_table",
    )()


def _proj_qkv_kernel(x_ref, g_ref, sh_ref, sc_ref, cos_ref, sin_ref, w_ref, o_ref, u_ref, *, tn):
    j = pl.program_id(1)

    @pl.when(j == 0)
    def _():
        u_ref[...] = _norm_mod(x_ref[...], g_ref[...], sh_ref[...], sc_ref[...]).astype(BF16)

    acc = jnp.dot(u_ref[...], w_ref[...].astype(BF16), preferred_element_type=F32)
    n_qk_tiles = 2 * N_GROUPS * D // tn

    @pl.when(j < n_qk_tiles)
    def _():
        half = ROT_DIM // 2
        cos = cos_ref[...]
        sin = sin_ref[...]
        lane = lax.broadcasted_iota(jnp.int32, cos.shape, 1)
        qscale = jnp.where(j < n_qk_tiles // 2, HEAD_DIM ** -0.5, 1.0).astype(F32)
        for h in range(tn // HEAD_DIM):
            xh = acc[:, h * HEAD_DIM:(h + 1) * HEAD_DIM]
            partner = jnp.where(lane < half, pltpu.roll(xh, HEAD_DIM - half, 1), pltpu.roll(xh, half, 1))
            o_ref[:, h * HEAD_DIM:(h + 1) * HEAD_DIM] = (xh * cos + partner * sin) * qscale

    @pl.when(j >= n_qk_tiles)
    def _():
        o_ref[...] = acc


def _proj_qkv(x, g, shift, scale, cos_t, sin_t, w, per_row, seq_len):
    rows = x.shape[0]
    tm = _row_tile(rows)
    tn = 512
    tps = max(seq_len // tm, 1)
    mod = _mod_spec(tm, per_row, tps, 2)
    if per_row:
        rope = pl.BlockSpec((tm, HEAD_DIM), lambda i, j: (i, 0))
    else:
        rope = pl.BlockSpec((tm, HEAD_DIM), lambda i, j: (i % tps, 0))
    return pl.pallas_call(
        functools.partial(_proj_qkv_kernel, tn=tn),
        out_shape=jax.ShapeDtypeStruct((rows, QKV_WIDTH), F32),
        grid=(rows // tm, QKV_WIDTH // tn),
        in_specs=[pl.BlockSpec((tm, D), lambda i, j: (i, 0)),
                  pl.BlockSpec((1, D), lambda i, j: (0, 0)),
                  mod, mod, rope, rope,
                  pl.BlockSpec((D, tn), lambda i, j: (0, j))],
        out_specs=pl.BlockSpec((tm, tn), lambda i, j: (i, j)),
        scratch_shapes=[pltpu.VMEM((tm, D), BF16)],
        compiler_params=_cparams(("arbitrary", "arbitrary")),
        name="proj_qkv",
    )(x, g, shift, scale, cos_t, sin_t, w)


def _attn_prompt_kernel(q_ref, kp_ref, kc_ref, vp_ref, vc_ref, o_ref, lse_ref):
    c = pl.program_id(2)
    n = ATT_BLOCK
    qi = lax.broadcasted_iota(jnp.int32, (n, 2 * n), 0)
    kj = lax.broadcasted_iota(jnp.int32, (n, 2 * n), 1)
    dist = qi + n - kj
    valid = (dist >= 0) & (dist <= n) & ((kj >= n) | (c > 0))
    lane = lax.broadcasted_iota(jnp.int32, (n, LANES), 1)
    lse_all = jnp.zeros((n, LANES), F32)
    nt = (((1,), (1,)), ((), ()))
    for h in range(N_HEADS):
        sl = slice(h * HEAD_DIM, (h + 1) * HEAD_DIM)
        q = q_ref[:, sl].astype(BF16)
        k = jnp.concatenate([kp_ref[:, sl], kc_ref[:, sl]], axis=0).astype(BF16)
        v = jnp.concatenate([vp_ref[:, sl], vc_ref[:, sl]], axis=0).astype(BF16)
        s = lax.dot_general(q, k, nt, preferred_element_type=F32)
        s = jnp.where(valid, s, -jnp.inf)
        m = jnp.max(s, axis=-1, keepdims=True)
        e = jnp.exp(s - m)
        l = jnp.sum(e, axis=-1, keepdims=True)
        p = e / l
        o_ref[:, sl] = jnp.dot(p.astype(BF16), v, preferred_element_type=F32)
        lse_all = jnp.where(lane == h, m + jnp.log(l), lse_all)
    lse_ref[...] = lse_all


def _attn_prompt(qkv, g, dil):
    bsz, seq, _ = qkv.shape
    lr = seq // dil
    nb = lr // ATT_BLOCK
    qkv_r = qkv.reshape(bsz, lr, dil * 9 * D)
    blk = (None, ATT_BLOCK, D)

    def col(which):
        return which * N_GROUPS + g

    def cur(which):
        return pl.BlockSpec(blk, lambda b, r, c: (b, c, r * 9 + col(which)))

    def prev(which):
        return pl.BlockSpec(blk, lambda b, r, c: (b, jnp.maximum(c - 1, 0), r * 9 + col(which)))

    o, lse = pl.pallas_call(
        _attn_prompt_kernel,
        out_shape=(jax.ShapeDtypeStruct((bsz, lr, dil * D), F32),
                   jax.ShapeDtypeStruct((bsz, lr, dil * LANES), F32)),
        grid=(bsz, dil, nb),
        in_specs=[cur(0), prev(1), cur(1), prev(2), cur(2)],
        out_specs=(pl.BlockSpec(blk, lambda b, r, c: (b, c, r)),
                   pl.BlockSpec((None, ATT_BLOCK, LANES), lambda b, r, c: (b, c, r))),
        compiler_params=_cparams(("arbitrary", "arbitrary", "arbitrary")),
        name="attn_prompt_g%d" % g,
    )(qkv_r, qkv_r, qkv_r, qkv_r, qkv_r)
    return o.reshape(bsz * seq, D), lse.reshape(bsz * seq, LANES)


def _attn_merge_kernel(o0_ref, o1_ref, o2_ref, l0_ref, l1_ref, l2_ref, ex_ref, x_ref, gate_ref, wo_ref,
                       xo_ref, wobf_ref):
    @pl.when(pl.program_id(0) == 0)
    def _():
        wobf_ref[...] = wo_ref[...].astype(BF16)

    l0, l1, l2 = l0_ref[...], l1_ref[...], l2_ref[...]
    m = jnp.maximum(jnp.maximum(l0, l1), l2)
    e0, e1, e2 = jnp.exp(l0 - m), jnp.exp(l1 - m), jnp.exp(l2 - m)
    inv = 1.0 / (e0 + e1 + e2)
    ex = ex_ref[...]

    def expand(w):
        hi = w.astype(BF16)
        lo = (w - hi.astype(F32)).astype(BF16)
        return (jnp.dot(hi, ex, preferred_element_type=F32) + jnp.dot(lo, ex, preferred_element_type=F32))

    o = expand(e0 * inv) * o0_ref[...] + expand(e1 * inv) * o1_ref[...] + expand(e2 * inv) * o2_ref[...]
    out = jnp.dot(o.astype(BF16), wobf_ref[...], preferred_element_type=F32)
    xo_ref[...] = x_ref[...] + gate_ref[...] * out


def _head_expand():
    h = lax.broadcasted_iota(jnp.int32, (LANES, D), 0)
    col = lax.broadcasted_iota(jnp.int32, (LANES, D), 1)
    return (col // HEAD_DIM == h).astype(BF16)


def _attn_merge(os_, lses, x, gate, wo, seq_len):
    rows = x.shape[0]
    tm = 512
    tps = seq_len // tm
    row = lambda i: (i, 0)
    return pl.pallas_call(
        _attn_merge_kernel,
        out_shape=jax.ShapeDtypeStruct((rows, D), F32),
        grid=(rows // tm,),
        in_specs=[pl.BlockSpec((tm, D), row)] * 3 + [pl.BlockSpec((tm, LANES), row)] * 3 + [
            pl.BlockSpec((LANES, D), lambda i: (0, 0)),
            pl.BlockSpec((tm, D), row),
            _mod_spec(tm, False, tps, 1),
            pl.BlockSpec((D, D), lambda i: (0, 0))],
        out_specs=pl.BlockSpec((tm, D), row),
        scratch_shapes=[pltpu.VMEM((D, D), BF16)],
        compiler_params=_cparams(("arbitrary",)),
        name="attn_merge",
    )(*os_, *lses, _head_expand(), x, gate, wo)


def _attn_step_kernel(*refs):
    q_refs, kn_refs, vn_refs = refs[0:3], refs[3:6], refs[6:9]
    kc_refs, vc_refs = refs[9:12], refs[12:15]
    o_ref = refs[15]
    outs, lses = [], []
    for g in range(N_GROUPS):
        q = q_refs[g][...]
        kn = kn_refs[g][...]
        vn = vn_refs[g][...]
        s_c = jnp.sum(kc_refs[g][...] * q[None], axis=-1, keepdims=True)
        s_n = jnp.sum(kn * q, axis=-1, keepdims=True)
        m = jnp.maximum(jnp.max(s_c, axis=0), s_n)
        e_c = jnp.exp(s_c - m[None])
        e_n = jnp.exp(s_n - m)
        l = jnp.sum(e_c, axis=0) + e_n
        outs.append((jnp.sum(e_c * vc_refs[g][...], axis=0) + e_n * vn) / l)
        lses.append(m + jnp.log(l))
    m = jnp.maximum(jnp.maximum(lses[0], lses[1]), lses[2])
    es = [jnp.exp(x - m) for x in lses]
    inv = 1.0 / (es[0] + es[1] + es[2])
    o_ref[...] = (es[0] * outs[0] + es[1] * outs[1] + es[2] * outs[2]) * inv


def _attn_step(qkv_s, caches_k, caches_v):
    bsz = qkv_s.shape[0]
    q4 = qkv_s.reshape(bsz, 9, N_HEADS, HEAD_DIM)

    def new_spec(which, g):
        return pl.BlockSpec((None, None, N_HEADS, HEAD_DIM), lambda b: (b, which * N_GROUPS + g, 0, 0))

    specs = [new_spec(w, g) for w in range(3) for g in range(N_GROUPS)]
    cache_spec = pl.BlockSpec((None, ATT_BLOCK, N_HEADS, HEAD_DIM), lambda b: (b, 0, 0, 0))
    ks, vs = [], []
    for g, (win, dil) in enumerate(ATTN_GROUPS):
        ks.append(caches_k[g].reshape(bsz, win // dil, dil * N_HEADS, HEAD_DIM))
        vs.append(caches_v[g].reshape(bsz, win // dil, dil * N_HEADS, HEAD_DIM))
    o = pl.pallas_call(
        _attn_step_kernel,
        out_shape=jax.ShapeDtypeStruct((bsz, N_HEADS, HEAD_DIM), F32),
        grid=(bsz,),
        in_specs=specs + [cache_spec] * 6,
        out_specs=pl.BlockSpec((None, N_HEADS, HEAD_DIM), lambda b: (b, 0, 0)),
        compiler_params=_cparams(("arbitrary",)),
        name="attn_step",
    )(*([q4] * 9), *ks, *vs)
    return o.reshape(bsz, D)


def _mm_res_kernel(a_ref, w_ref, b_ref, x_ref, gate_ref, xo_ref):
    out = _bdot(a_ref[...], w_ref[...]) + b_ref[...]
    xo_ref[...] = x_ref[...] + gate_ref[...] * out


def _mm_res(a, w, bias, x, gate):
    return pl.pallas_call(
        _mm_res_kernel,
        out_shape=jax.ShapeDtypeStruct(x.shape, F32),
        compiler_params=pltpu.CompilerParams(vmem_limit_bytes=VMEM_LIMIT),
        name="mm_res",
    )(a, w, bias, x, gate)


def _proj_glu_kernel(x_ref, g_ref, sh_ref, sc_ref, wa_ref, wb_ref, ba_ref, bb_ref, o_ref, u_ref):
    @pl.when(pl.program_id(1) == 0)
    def _():
        u_ref[...] = _norm_mod(x_ref[...], g_ref[...], sh_ref[...], sc_ref[...]).astype(BF16)

    u = u_ref[...]
    za = jnp.dot(u, wa_ref[...].astype(BF16), preferred_element_type=F32) + ba_ref[...]
    zb = jnp.dot(u, wb_ref[...].astype(BF16), preferred_element_type=F32) + bb_ref[...]
    o_ref[...] = za * _sigmoid(zb)


def _proj_glu(x, g, shift, scale, w, bias, per_row, seq_len):
    rows = x.shape[0]
    tm = _row_tile(rows)
    tn = 256
    nt = D // tn
    tps = max(seq_len // tm, 1)
    mod = _mod_spec(tm, per_row, tps, 2)
    return pl.pallas_call(
        _proj_glu_kernel,
        out_shape=jax.ShapeDtypeStruct((rows, D), F32),
        grid=(rows // tm, nt),
        in_specs=[pl.BlockSpec((tm, D), lambda i, j: (i, 0)),
                  pl.BlockSpec((1, D), lambda i, j: (0, 0)),
                  mod, mod,
                  pl.BlockSpec((D, tn), lambda i, j: (0, j)),
                  pl.BlockSpec((D, tn), lambda i, j: (0, j + nt)),
                  pl.BlockSpec((1, tn), lambda i, j: (0, j)),
                  pl.BlockSpec((1, tn), lambda i, j: (0, j + nt))],
        out_specs=pl.BlockSpec((tm, tn), lambda i, j: (i, j)),
        scratch_shapes=[pltpu.VMEM((tm, D), BF16)],
        compiler_params=_cparams(("arbitrary", "arbitrary")),
        name="proj_glu",
    )(x, g, shift, scale, w, w, bias, bias)


def _layer_norm(y, g, b):
    mu = jnp.mean(y, axis=-1, keepdims=True)
    yc = y - mu
    var = jnp.mean(yc * yc, axis=-1, keepdims=True)
    return yc * lax.rsqrt(var + NORM_EPS) * g + b


CONF_HALO = 32
CONF_CHUNK = 64


def _conf_kernel(glu_ref, x_ref, gate_ref, wdw_ref, bdw_ref, lg_ref, lb_ref, w2_ref, b2_ref, xo_ref,
                 xx_ref, z_ref, w2bf_ref, *, tt):
    b = pl.program_id(0)
    t = pl.program_id(1)

    @pl.when((b == 0) & (t == 0))
    def _():
        w2bf_ref[...] = w2_ref[...].astype(BF16)

    @pl.when(t == 0)
    def _():
        xx_ref[0:CONF_HALO, :] = jnp.zeros((CONF_HALO, D), F32)

    @pl.when(t > 0)
    def _():
        xx_ref[0:CONF_HALO, :] = xx_ref[tt:tt + CONF_HALO, :]

    xx_ref[CONF_HALO:tt + CONF_HALO, :] = glu_ref[...]
    first = CONF_HALO - (CONF_WIDTH - 1)
    for r0 in range(0, tt, CONF_CHUNK):
        for c in range(D // LANES):
            sl = slice(c * LANES, (c + 1) * LANES)
            acc = jnp.broadcast_to(bdw_ref[:, sl], (CONF_CHUNK, LANES))
            for k in range(CONF_WIDTH):
                acc = acc + wdw_ref[k:k + 1, sl] * xx_ref[r0 + first + k:r0 + first + k + CONF_CHUNK, sl]
            z_ref[r0:r0 + CONF_CHUNK, sl] = acc
    y = _silu(_layer_norm(z_ref[...], lg_ref[...], lb_ref[...]))
    out = jnp.dot(y.astype(BF16), w2bf_ref[...], preferred_element_type=F32) + b2_ref[...]
    xo_ref[...] = x_ref[...] + gate_ref[...] * out


def _conf_prompt(glu, x, gate, wdw, bdw, lg, lb, w2, b2):
    bsz, seq, _ = glu.shape
    tt = 512
    full2 = lambda b, t: (0, 0)
    vec = pl.BlockSpec((1, D), full2)
    return pl.pallas_call(
        functools.partial(_conf_kernel, tt=tt),
        out_shape=jax.ShapeDtypeStruct((bsz, seq, D), F32),
        grid=(bsz, seq // tt),
        in_specs=[pl.BlockSpec((None, tt, D), lambda b, t: (b, t, 0)),
                  pl.BlockSpec((None, tt, D), lambda b, t: (b, t, 0)),
                  pl.BlockSpec((None, 1, D), lambda b, t: (b, 0, 0)),
                  pl.BlockSpec((CONF_WIDTH, D), full2),
                  vec, vec, vec,
                  pl.BlockSpec((D, D), full2),
                  vec],
        out_specs=pl.BlockSpec((None, tt, D), lambda b, t: (b, t, 0)),
        scratch_shapes=[pltpu.VMEM((tt + CONF_HALO, D), F32),
                        pltpu.VMEM((tt, D), F32),
                        pltpu.VMEM((D, D), BF16)],
        compiler_params=_cparams(("arbitrary", "arbitrary")),
        name="conf_prompt",
    )(glu, x, gate, wdw, bdw, lg, lb, w2, b2)


def _conf_step_kernel(glu_ref, hist_ref, x_ref, gate_ref, wdw_ref, bdw_ref, lg_ref, lb_ref, w2_ref, b2_ref, xo_ref):
    acc = bdw_ref[...] + wdw_ref[CONF_WIDTH - 1:CONF_WIDTH, :] * glu_ref[...]
    for k in range(CONF_WIDTH - 1):
        acc = acc + wdw_ref[k:k + 1, :] * hist_ref[:, k * D:(k + 1) * D]
    y = _silu(_layer_norm(acc, lg_ref[...], lb_ref[...]))
    out = _bdot(y, w2_ref[...]) + b2_ref[...]
    xo_ref[...] = x_ref[...] + gate_ref[...] * out


def _conf_step(glu, hist, x, gate, wdw, bdw, lg, lb, w2, b2):
    return pl.pallas_call(
        _conf_step_kernel,
        out_shape=jax.ShapeDtypeStruct(x.shape, F32),
        compiler_params=pltpu.CompilerParams(vmem_limit_bytes=VMEM_LIMIT),
        name="conf_step",
    )(glu, hist, x, gate, wdw, bdw, lg, lb, w2, b2)


def _moe_route_kernel(x_ref, g_ref, sh_ref, sc_ref, wr_ref, br_ref, *rest):
    u_ref, info_ref = rest[-2], rest[-1]
    u = _norm_mod(x_ref[...], g_ref[...], sh_ref[...], sc_ref[...])
    u_ref[...] = u.astype(BF16)
    logits = jnp.dot(u, wr_ref[...], preferred_element_type=F32, precision=HIGHEST) + br_ref[...]
    lane = lax.broadcasted_iota(jnp.int32, logits.shape, 1)
    neg = -jnp.inf
    big = jnp.int32(LANES)
    gl = jnp.where(lane < N_EGROUPS, logits, neg)
    gmax = jnp.max(gl, axis=-1, keepdims=True)
    gsel = jnp.min(jnp.where(gl == gmax, lane, big), axis=-1, keepdims=True)
    gp = 1.0 / jnp.sum(jnp.exp(gl - gmax), axis=-1, keepdims=True)
    base = N_EGROUPS + EXPERTS_PER_GROUP * gsel
    el = jnp.where((lane >= base) & (lane < base + EXPERTS_PER_GROUP), logits, neg)
    v1 = jnp.max(el, axis=-1, keepdims=True)
    i1 = jnp.min(jnp.where(el == v1, lane, big), axis=-1, keepdims=True)
    el2 = jnp.where(lane == i1, neg, el)
    v2 = jnp.max(el2, axis=-1, keepdims=True)
    i2 = jnp.min(jnp.where(el2 == v2, lane, big), axis=-1, keepdims=True)
    t = jnp.exp(v2 - v1)
    w1 = gp / (1.0 + t)
    w2 = gp * t / (1.0 + t)
    e1 = i1 - base
    e2 = i2 - base
    lo = jnp.minimum(e1, e2)
    hi = jnp.maximum(e1, e2)
    w_lo = jnp.where(e1 < e2, w1, w2)
    w_hi = jnp.where(e1 < e2, w2, w1)
    pair = (lo * (7 - lo)) // 2 + hi - lo - 1
    cls = (gsel * N_PAIRS + pair).astype(F32)
    info_ref[...] = jnp.where(lane == 0, cls, jnp.where(lane == 1, w_lo, jnp.where(lane == 2, w_hi, 0.0)))


def _moe_route(x, g, shift, scale, wr, br, per_row, seq_len, rows_all, row_block0, prev):
    rows = x.shape[0]
    tm = _row_tile(rows)
    tps = max(seq_len // tm, 1)
    mod = _mod_spec(tm, per_row, tps, 1)
    in_specs = [pl.BlockSpec((tm, D), lambda i: (i, 0)),
                pl.BlockSpec((1, D), lambda i: (0, 0)),
                mod, mod,
                pl.BlockSpec((D, LANES), lambda i: (0, 0)),
                pl.BlockSpec((1, LANES), lambda i: (0, 0))]
    args = [x, g, shift, scale, wr, br]
    aliases = {}
    if prev is not None:
        in_specs += [pl.BlockSpec(memory_space=pl.ANY)] * 2
        args += list(prev)
        aliases = {6: 0, 7: 1}
    return pl.pallas_call(
        _moe_route_kernel,
        out_shape=(jax.ShapeDtypeStruct((rows_all, D), BF16), jax.ShapeDtypeStruct((rows_all, LANES), F32)),
        grid=(rows // tm,),
        in_specs=in_specs,
        out_specs=(pl.BlockSpec((tm, D), lambda i: (i + row_block0, 0)),
                   pl.BlockSpec((tm, LANES), lambda i: (i + row_block0, 0))),
        input_output_aliases=aliases,
        compiler_params=_cparams(("arbitrary",)),
        name="moe_route",
    )(*args)


def _moe_expert_kernel(ea_ref, eb_ref, valid_ref, x_ref, info_ref, wga_ref, wua_ref, wda_ref,
                       wgb_ref, wub_ref, wdb_ref, o_ref):
    t = pl.program_id(0)

    @pl.when(valid_ref[t] == 1)
    def _():
        x = x_ref[...]
        info = info_ref[...]

        def expert(wg_ref, wu_ref, wd_ref, w):
            hg = jnp.dot(x, wg_ref[...], preferred_element_type=F32)
            hu = jnp.dot(x, wu_ref[...], preferred_element_type=F32)
            act = _silu(hg) * hu * w
            return jnp.dot(act.astype(BF16), wd_ref[...], preferred_element_type=F32)

        o_ref[...] = (expert(wga_ref, wua_ref, wda_ref, info[:, 1:2])
                      + expert(wgb_ref, wub_ref, wdb_ref, info[:, 2:3]))

    @pl.when(valid_ref[t] == 0)
    def _():
        o_ref[...] = jnp.zeros(o_ref.shape, F32)


def _moe_experts(layer, tile_ea, tile_eb, tile_valid, u_sorted, info_sorted, w_gate, w_up, w_down):
    n_tiles = tile_ea.shape[0]

    def wspec(shape, which):
        if which == 0:
            return pl.BlockSpec((None, None) + shape, lambda t, ea, eb, va: (layer, ea[t], 0, 0))
        return pl.BlockSpec((None, None) + shape, lambda t, ea, eb, va: (layer, eb[t], 0, 0))

    up = (D, D_EXPERT)
    down = (D_EXPERT, D)
    grid_spec = pltpu.PrefetchScalarGridSpec(
        num_scalar_prefetch=3,
        grid=(n_tiles,),
        in_specs=[pl.BlockSpec((MOE_TILE, D), lambda t, ea, eb, va: (t, 0)),
                  pl.BlockSpec((MOE_TILE, LANES), lambda t, ea, eb, va: (t, 0)),
                  wspec(up, 0), wspec(up, 0), wspec(down, 0),
                  wspec(up, 1), wspec(up, 1), wspec(down, 1)],
        out_specs=pl.BlockSpec((MOE_TILE, D), lambda t, ea, eb, va: (t, 0)),
    )
    return pl.pallas_call(
        _moe_expert_kernel,
        out_shape=jax.ShapeDtypeStruct((n_tiles * MOE_TILE, D), F32),
        grid_spec=grid_spec,
        compiler_params=_cparams(("arbitrary",)),
        name="moe_experts",
    )(tile_ea, tile_eb, tile_valid, u_sorted, info_sorted, w_gate, w_up, w_down, w_gate, w_up, w_down)


_PAIR_LO = (0, 0, 0, 1, 1, 2)
_PAIR_HI = (1, 2, 3, 2, 3, 3)


def _moe_plan(info, n_tiles):
    rows = info.shape[0]
    cls = info[:, 0].astype(jnp.int32)
    onehot = (cls[:, None] == jnp.arange(N_CLASSES, dtype=jnp.int32)[None, :]).astype(jnp.int32)
    csum = jnp.cumsum(onehot, axis=0)
    counts = csum[-1]
    rank = jnp.sum(onehot * csum, axis=1) - 1
    padded = ((counts + MOE_TILE - 1) // MOE_TILE) * MOE_TILE
    ends = jnp.cumsum(padded)
    offs = ends - padded
    dest = offs[cls] + rank
    order = jnp.argsort(cls, stable=True).astype(jnp.int32)
    uoffs = jnp.cumsum(counts) - counts
    pos = jnp.arange(n_tiles * MOE_TILE, dtype=jnp.int32)
    pcls = jnp.minimum(jnp.searchsorted(ends, pos, side="right").astype(jnp.int32), N_CLASSES - 1)
    within = pos - offs[pcls]
    src = order[jnp.clip(uoffs[pcls] + jnp.minimum(within, counts[pcls] - 1), 0, rows - 1)]
    tile_start = jnp.arange(n_tiles, dtype=jnp.int32) * MOE_TILE
    tile_valid = (tile_start < ends[-1]).astype(jnp.int32)
    last_cls = jnp.max(jnp.where(counts > 0, jnp.arange(N_CLASSES, dtype=jnp.int32), 0))
    tcls = jnp.where(tile_valid == 1, pcls[::MOE_TILE], last_cls)
    grp = tcls // N_PAIRS
    pair = tcls % N_PAIRS
    tile_ea = grp * EXPERTS_PER_GROUP + jnp.asarray(_PAIR_LO, jnp.int32)[pair]
    tile_eb = grp * EXPERTS_PER_GROUP + jnp.asarray(_PAIR_HI, jnp.int32)[pair]
    return dest, src, tile_ea, tile_eb, tile_valid


def _residual_kernel(x_ref, y_ref, gate_ref, fg_ref, xo_ref, *, final_norm):
    xn = x_ref[...] + gate_ref[...] * y_ref[...]
    if final_norm:
        ms = jnp.mean(xn * xn, axis=-1, keepdims=True)
        xn = xn * lax.rsqrt(ms + NORM_EPS) * fg_ref[...]
    xo_ref[...] = xn


def _residual(x, y_all, gate, fg, per_row, seq_len, row_block0, final_norm):
    rows = x.shape[0]
    tm = _row_tile(rows)
    tps = max(seq_len // tm, 1)
    return pl.pallas_call(
        functools.partial(_residual_kernel, final_norm=final_norm),
        out_shape=jax.ShapeDtypeStruct((rows, D), F32),
        grid=(rows // tm,),
        in_specs=[pl.BlockSpec((tm, D), lambda i: (i, 0)),
                  pl.BlockSpec((tm, D), lambda i: (i + row_block0, 0)),
                  _mod_spec(tm, per_row, tps, 1),
                  pl.BlockSpec((1, D), lambda i: (0, 0))],
        out_specs=pl.BlockSpec((tm, D), lambda i: (i, 0)),
        compiler_params=_cparams(("arbitrary",)),
        name="residual",
    )(x, y_all, gate, fg)


def kernel(x_prompt, x_sample, c_prompt, c_sample, state_a_conv, state_a_h, cache_b_k0, cache_b_v0, cache_b_k1, cache_b_v1, cache_b_k2, cache_b_v2, state_c_conv, norm_mix_g, norm_ffn_g, ada_w, ada_b, final_norm_g, a_w_in_y, a_w_in_x, a_conv_w, a_conv_b, a_gate_r_w, a_gate_r_b, a_gate_i_w, a_gate_i_b, a_lambda, a_w_out, b_w_qkv, b_w_o, conf_w_pw1, conf_b_pw1, conf_w_dw, conf_b_dw, conf_ln_g, conf_ln_b, conf_w_pw2, conf_b_pw2, moe_w_grouter, moe_b_grouter, moe_w_erouter, moe_b_erouter, moe_w_gate, moe_w_up, moe_w_down):
    bsz, seq, _ = x_prompt.shape
    dbsz = x_sample.shape[0]
    depth = ada_w.shape[0]
    rows_p = bsz * seq
    rows_all = rows_p + dbsz
    n_tiles = rows_all // MOE_TILE + N_CLASSES
    caches_k = (cache_b_k0, cache_b_k1, cache_b_k2)
    caches_v = (cache_b_v0, cache_b_v1, cache_b_v2)

    c_rows = -(-(bsz + dbsz) // SUBLANES) * SUBLANES
    c_all = jnp.concatenate([c_prompt, c_sample, jnp.zeros((c_rows - bsz - dbsz, D), F32)], axis=0)
    mods = _ada_mod(c_all, ada_w, ada_b)

    cos_all, sin_all = _rope_table(seq + SUBLANES)
    cos_p, sin_p = cos_all[:seq], sin_all[:seq]
    cos_s = jnp.broadcast_to(cos_all[PAST_LEN:PAST_LEN + 1], (dbsz, HEAD_DIM))
    sin_s = jnp.broadcast_to(sin_all[PAST_LEN:PAST_LEN + 1], (dbsz, HEAD_DIM))

    wg_bf = moe_w_gate.astype(BF16)
    wu_bf = moe_w_up.astype(BF16)
    wd_bf = moe_w_down.astype(BF16)

    xp = x_prompt.reshape(rows_p, D)
    xs = x_sample.reshape(dbsz, D)
    row1 = lambda v: v.reshape(1, -1)

    a_conv_p, a_conv_s, a_h_p, a_h_s = [], [], [], []
    kp, vp, ksm, vsm = ([[] for _ in range(N_GROUPS)] for _ in range(4))
    cf_p, cf_s = [], []

    for i in range(depth):
        kind, j = i % 3, i // 3
        mp = [mods[i, :bsz, k * D:(k + 1) * D].reshape(bsz, 1, D) for k in range(6)]
        ms = [mods[i, bsz:bsz + dbsz, k * D:(k + 1) * D] for k in range(6)]
        g_mix = row1(norm_mix_g[i])
        if kind == 0:
            wts = (a_conv_w[j], row1(a_conv_b[j]), a_gate_r_w[j], row1(a_gate_r_b[j]),
                   a_gate_i_w[j], row1(a_gate_i_b[j]), row1(a_lambda[j]), a_w_out[j])
            ybr, xin = _proj_a(xp, g_mix, mp[0], mp[1], a_w_in_y[j], a_w_in_x[j], False, seq)
            xp3, h_last = _rglru_prompt(xin.reshape(bsz, seq, D_RNN), ybr.reshape(bsz, seq, D_RNN),
                                        xp.reshape(bsz, seq, D), mp[2], *wts)
            xp = xp3.reshape(rows_p, D)
            a_conv_p.append(xin.reshape(bsz, seq, D_RNN)[:, seq - (LRU_CONV - 1):])
            a_h_p.append(h_last.reshape(bsz, D_RNN))
            ybr_s, xin_s = _proj_a(xs, g_mix, ms[0], ms[1], a_w_in_y[j], a_w_in_x[j], True, 1)
            hist = state_a_conv[j]
            xs, h_new = _rglru_step(xin_s, ybr_s, xs, ms[2], hist.reshape(dbsz, (LRU_CONV - 1) * D_RNN),
                                    state_a_h[j], *wts)
            a_conv_s.append(jnp.concatenate([hist[:, 1:], xin_s[:, None, :]], axis=1))
            a_h_s.append(h_new)
        elif kind == 1:
            qkv = _proj_qkv(xp, g_mix, mp[0], mp[1], cos_p, sin_p, b_w_qkv[j], False, seq)
            qkv3 = qkv.reshape(bsz, seq, QKV_WIDTH)
            os_, lses = [], []
            for g, (win, dil) in enumerate(ATTN_GROUPS):
                o, lse = _attn_prompt(qkv3, g, dil)
                os_.append(o)
                lses.append(lse)
                keep = min(win, seq)
                kcol = (N_GROUPS + g) * D
                vcol = (2 * N_GROUPS + g) * D
                kp[g].append(qkv3[:, seq - keep:, kcol:kcol + D].reshape(bsz, keep, N_HEADS, HEAD_DIM))
                vp[g].append(qkv3[:, seq - keep:, vcol:vcol + D].reshape(bsz, keep, N_HEADS, HEAD_DIM))
            xp = _attn_merge(os_, lses, xp, mp[2], b_w_o[j], seq)
            qkv_s = _proj_qkv(xs, g_mix, ms[0], ms[1], cos_s, sin_s, b_w_qkv[j], True, 1)
            o_s = _attn_step(qkv_s, [c[j] for c in caches_k], [c[j] for c in caches_v])
            xs = _mm_res(o_s, b_w_o[j], jnp.zeros((1, D), F32), xs, ms[2])
            for g in range(N_GROUPS):
                kcol = (N_GROUPS + g) * D
                vcol = (2 * N_GROUPS + g) * D
                ksm[g].append(qkv_s[:, kcol:kcol + D].reshape(dbsz, 1, N_HEADS, HEAD_DIM))
                vsm[g].append(qkv_s[:, vcol:vcol + D].reshape(dbsz, 1, N_HEADS, HEAD_DIM))
        else:
            wts = (conf_w_dw[j], row1(conf_b_dw[j]), row1(conf_ln_g[j]), row1(conf_ln_b[j]),
                   conf_w_pw2[j], row1(conf_b_pw2[j]))
            b1 = row1(conf_b_pw1[j])
            glu = _proj_glu(xp, g_mix, mp[0], mp[1], conf_w_pw1[j], b1, False, seq)
            glu3 = glu.reshape(bsz, seq, D)
            xp = _conf_prompt(glu3, xp.reshape(bsz, seq, D), mp[2], *wts).reshape(rows_p, D)
            cf_p.append(glu3[:, seq - (CONF_WIDTH - 1):])
            glu_s = _proj_glu(xs, g_mix, ms[0], ms[1], conf_w_pw1[j], b1, True, 1)
            hist = state_c_conv[j]
            xs = _conf_step(glu_s, hist.reshape(dbsz, (CONF_WIDTH - 1) * D), xs, ms[2], *wts)
            cf_s.append(jnp.concatenate([hist[:, 1:], glu_s[:, None, :]], axis=1))

        g_ffn = row1(norm_ffn_g[i])
        wr = jnp.concatenate([moe_w_grouter[i], moe_w_erouter[i],
                              jnp.zeros((D, LANES - N_EGROUPS - N_EXPERTS), F32)], axis=1)
        br = jnp.concatenate([moe_b_grouter[i], moe_b_erouter[i],
                              jnp.zeros((LANES - N_EGROUPS - N_EXPERTS,), F32)]).reshape(1, LANES)
        joint = _moe_route(xp, g_ffn, mp[3], mp[4], wr, br, False, seq, rows_all, 0, None)
        u_all, info = _moe_route(xs, g_ffn, ms[3], ms[4], wr, br, True, 1, rows_all, rows_p // dbsz, joint)
        dest, src, tile_ea, tile_eb, tile_valid = _moe_plan(info, n_tiles)
        u_sorted = jnp.take(u_all, src, axis=0)
        info_sorted = jnp.take(info, src, axis=0)
        y_sorted = _moe_experts(i, tile_ea, tile_eb, tile_valid, u_sorted, info_sorted, wg_bf, wu_bf, wd_bf)
        y_all = jnp.take(y_sorted, dest, axis=0)
        last = i == depth - 1
        fg = row1(final_norm_g)
        xp = _residual(xp, y_all, mp[5], fg, False, seq, 0, last)
        xs = _residual(xs, y_all, ms[5], fg, True, 1, rows_p // dbsz, last)

    y_prompt = xp.reshape(bsz, seq, D)
    y_sample = xs.reshape(dbsz, 1, D)
    return (y_prompt, y_sample,
            jnp.stack(a_conv_p), jnp.stack(a_conv_s), jnp.stack(a_h_p), jnp.stack(a_h_s),
            jnp.stack(kp[0]), jnp.stack(ksm[0]), jnp.stack(vp[0]), jnp.stack(vsm[0]),
            jnp.stack(kp[1]), jnp.stack(ksm[1]), jnp.stack(vp[1]), jnp.stack(vsm[1]),
            jnp.stack(kp[2]), jnp.stack(ksm[2]), jnp.stack(vp[2]), jnp.stack(vsm[2]),
            jnp.stack(cf_p), jnp.stack(cf_s))
```

```python
import functools
import math

import jax
import jax.numpy as jnp
from jax import lax
from jax.experimental import pallas as pl
from jax.experimental.pallas import tpu as pltpu

F32 = jnp.float32
BF16 = jnp.bfloat16
HIGHEST = lax.Precision.HIGHEST

D = 1024
D_RNN = 1280
LRU_BLOCKS = 10
LRU_BLOCK = 128
LRU_CONV = 4
LRU_C = 8.0
ATTN_GROUPS = ((128, 1), (512, 4), (2048, 16))
N_GROUPS = 3
N_HEADS = 8
HEAD_DIM = 128
ROT_DIM = 32
ROPE_THETA = 500000.0
QKV_WIDTH = 9 * D
CONF_WIDTH = 31
N_EGROUPS = 4
EXPERTS_PER_GROUP = 4
N_EXPERTS = 16
D_EXPERT = 512
N_PAIRS = 6
N_CLASSES = N_EGROUPS * N_PAIRS
NORM_EPS = 1e-6
PAST_LEN = 8192

LANES = 128
SUBLANES = 8
ATT_BLOCK = 128
ATT_TOKENS = 2048
MOE_TILE = 256
VMEM_LIMIT = 56 * 1024 * 1024


def _cparams(sem):
    return pltpu.CompilerParams(dimension_semantics=sem, vmem_limit_bytes=VMEM_LIMIT)


def _sigmoid(x):
    return 1.0 / (1.0 + jnp.exp(-x))


def _silu(x):
    return x * _sigmoid(x)


def _gelu_tanh(x):
    return 0.5 * x * (1.0 + jnp.tanh(math.sqrt(2.0 / math.pi) * (x + 0.044715 * (x * x * x))))


def _softplus(x):
    return jnp.maximum(x, 0.0) + jnp.log1p(jnp.exp(-jnp.abs(x)))


def _bdot(a, b):
    return jnp.dot(a.astype(BF16), b.astype(BF16), preferred_element_type=F32)


def _norm_mod(x, g, shift, scale):
    ms = jnp.mean(x * x, axis=-1, keepdims=True)
    y = x * lax.rsqrt(ms + NORM_EPS) * g
    return y * (1.0 + scale) + shift


def _ada_kernel(c_ref, w_ref, b_ref, o_ref):
    c = c_ref[...]
    o_ref[...] = jnp.dot(_silu(c), w_ref[...], preferred_element_type=F32, precision=HIGHEST) + b_ref[...]


def _ada_mod(c_all, ada_w, ada_b):
    depth, _, n6 = ada_w.shape
    rows = c_all.shape[0]
    tn = 1536
    return pl.pallas_call(
        _ada_kernel,
        out_shape=jax.ShapeDtypeStruct((depth, rows, n6), F32),
        grid=(depth, n6 // tn),
        in_specs=[pl.BlockSpec((rows, D), lambda l, j: (0, 0)),
                  pl.BlockSpec((None, D, tn), lambda l, j: (l, 0, j)),
                  pl.BlockSpec((None, 1, tn), lambda l, j: (l, 0, j))],
        out_specs=pl.BlockSpec((None, rows, tn), lambda l, j: (l, 0, j)),
        compiler_params=_cparams(("arbitrary", "arbitrary")),
        name="ada_mod",
    )(c_all, ada_w, ada_b.reshape(depth, 1, n6))


def _mod_spec(tm, per_row, tiles_per_seq, ngrid):
    if per_row:
        if ngrid == 1:
            return pl.BlockSpec((tm, D), lambda i: (i, 0))
        return pl.BlockSpec((tm, D), lambda i, j: (i, 0))
    if ngrid == 1:
        return pl.BlockSpec((None, 1, D), lambda i: (i // tiles_per_seq, 0, 0))
    return pl.BlockSpec((None, 1, D), lambda i, j: (i // tiles_per_seq, 0, 0))


def _row_tile(rows):
    return 1024 if rows % 1024 == 0 else rows


def _proj_a_kernel(x_ref, g_ref, sh_ref, sc_ref, wy_ref, wx_ref, y_ref, xi_ref):
    u = _norm_mod(x_ref[...], g_ref[...], sh_ref[...], sc_ref[...]).astype(BF16)
    y_ref[...] = _gelu_tanh(jnp.dot(u, wy_ref[...], preferred_element_type=F32))
    xi_ref[...] = jnp.dot(u, wx_ref[...], preferred_element_type=F32)


def _proj_a(x, g, shift, scale, w_y, w_x, per_row, seq_len):
    rows = x.shape[0]
    tm = 512 if rows % 512 == 0 else rows
    tps = max(seq_len // tm, 1)
    mod = _mod_spec(tm, per_row, tps, 1)
    wspec = pl.BlockSpec((D, D_RNN), lambda i: (0, 0))
    ospec = pl.BlockSpec((tm, D_RNN), lambda i: (i, 0))
    return pl.pallas_call(
        _proj_a_kernel,
        out_shape=(jax.ShapeDtypeStruct((rows, D_RNN), F32), jax.ShapeDtypeStruct((rows, D_RNN), F32)),
        grid=(rows // tm,),
        in_specs=[pl.BlockSpec((tm, D), lambda i: (i, 0)),
                  pl.BlockSpec((1, D), lambda i: (0, 0)),
                  mod, mod, wspec, wspec],
        out_specs=(ospec, ospec),
        compiler_params=_cparams(("arbitrary",)),
        name="proj_a",
    )(x, g, shift, scale, w_y, w_x)


def _scan_rows(a, b):
    n = a.shape[0]
    row = lax.broadcasted_iota(jnp.int32, a.shape, 0)
    d = 1
    while d < n:
        keep = row >= d
        a_sh = jnp.where(keep, pltpu.roll(a, d, 0), 1.0)
        b_sh = jnp.where(keep, pltpu.roll(b, d, 0), 0.0)
        b = a * b_sh + b
        a = a * a_sh
        d *= 2
    return a, b


def _lru_gates(xc, grw, grb, giw, gib, lam):
    xcb = xc.astype(BF16)
    r = _sigmoid(jnp.dot(xcb, grw, preferred_element_type=F32) + grb)
    i = _sigmoid(jnp.dot(xcb, giw, preferred_element_type=F32) + gib)
    log_a = -LRU_C * r * _softplus(-lam)
    a = jnp.exp(log_a)
    b = jnp.sqrt(jnp.tanh(-log_a) * (a * a + 1.0)) * (i * xc)
    return a, b


def _rglru_kernel(xin_ref, ybr_ref, x_ref, gate_ref, cw_ref, cb_ref, grw_ref, grb_ref, giw_ref, gib_ref,
                  lam_ref, wo_ref, xo_ref, hl_ref, xx_ref, hc_ref, hy_ref, *, tt):
    t = pl.program_id(1)

    @pl.when(t == 0)
    def _():
        xx_ref[0:SUBLANES, :] = jnp.zeros((SUBLANES, D_RNN), F32)
        hc_ref[...] = jnp.zeros((1, D_RNN), F32)

    @pl.when(t > 0)
    def _():
        xx_ref[0:SUBLANES, :] = xx_ref[tt:tt + SUBLANES, :]

    xx_ref[SUBLANES:tt + SUBLANES, :] = xin_ref[...]
    for n in range(LRU_BLOCKS):
        sl = slice(n * LRU_BLOCK, (n + 1) * LRU_BLOCK)
        xc = cb_ref[:, sl]
        for k in range(LRU_CONV):
            off = SUBLANES - (LRU_CONV - 1) + k
            xc = xc + cw_ref[k:k + 1, sl] * xx_ref[off:off + tt, sl]
        a, bt = _lru_gates(xc, grw_ref[n], grb_ref[:, sl], giw_ref[n], gib_ref[:, sl], lam_ref[:, sl])
        a_cum, h = _scan_rows(a, bt)
        h = h + a_cum * hc_ref[:, sl]
        hc_ref[:, sl] = h[tt - 1:tt, :]
        hy_ref[:, sl] = (h * ybr_ref[:, sl]).astype(BF16)
    out = jnp.dot(hy_ref[...], wo_ref[...], preferred_element_type=F32)
    xo_ref[...] = x_ref[...] + gate_ref[...] * out
    hl_ref[...] = hc_ref[...]


def _rglru_prompt(xin, ybr, x, gate, cw, cb, grw, grb, giw, gib, lam, wo):
    bsz, seq, _ = xin.shape
    tt = 512
    full2 = lambda b, t: (0, 0)
    full3 = lambda b, t: (0, 0, 0)
    return pl.pallas_call(
        functools.partial(_rglru_kernel, tt=tt),
        out_shape=(jax.ShapeDtypeStruct((bsz, seq, D), F32), jax.ShapeDtypeStruct((bsz, 1, D_RNN), F32)),
        grid=(bsz, seq // tt),
        in_specs=[pl.BlockSpec((None, tt, D_RNN), lambda b, t: (b, t, 0)),
                  pl.BlockSpec((None, tt, D_RNN), lambda b, t: (b, t, 0)),
                  pl.BlockSpec((None, tt, D), lambda b, t: (b, t, 0)),
                  pl.BlockSpec((None, 1, D), lambda b, t: (b, 0, 0)),
                  pl.BlockSpec((LRU_CONV, D_RNN), full2),
                  pl.BlockSpec((1, D_RNN), full2),
                  pl.BlockSpec((LRU_BLOCKS, LRU_BLOCK, LRU_BLOCK), full3),
                  pl.BlockSpec((1, D_RNN), full2),
                  pl.BlockSpec((LRU_BLOCKS, LRU_BLOCK, LRU_BLOCK), full3),
                  pl.BlockSpec((1, D_RNN), full2),
                  pl.BlockSpec((1, D_RNN), full2),
                  pl.BlockSpec((D_RNN, D), full2)],
        out_specs=(pl.BlockSpec((None, tt, D), lambda b, t: (b, t, 0)),
                   pl.BlockSpec((None, 1, D_RNN), lambda b, t: (b, 0, 0))),
        scratch_shapes=[pltpu.VMEM((tt + SUBLANES, D_RNN), F32),
                        pltpu.VMEM((1, D_RNN), F32),
                        pltpu.VMEM((tt, D_RNN), BF16)],
        compiler_params=_cparams(("arbitrary", "arbitrary")),
        name="rglru_prompt",
    )(xin, ybr, x, gate, cw, cb, grw, grb, giw, gib, lam, wo)


def _rglru_step_kernel(xin_ref, ybr_ref, x_ref, gate_ref, hist_ref, hprev_ref, cw_ref, cb_ref, grw_ref, grb_ref,
                       giw_ref, gib_ref, lam_ref, wo_ref, xo_ref, hn_ref, hy_ref):
    for n in range(LRU_BLOCKS):
        sl = slice(n * LRU_BLOCK, (n + 1) * LRU_BLOCK)
        xc = cb_ref[:, sl] + cw_ref[LRU_CONV - 1:LRU_CONV, sl] * xin_ref[:, sl]
        for k in range(LRU_CONV - 1):
            xc = xc + cw_ref[k:k + 1, sl] * hist_ref[:, k * D_RNN + n * LRU_BLOCK:k * D_RNN + (n + 1) * LRU_BLOCK]
        a, bt = _lru_gates(xc, grw_ref[n], grb_ref[:, sl], giw_ref[n], gib_ref[:, sl], lam_ref[:, sl])
        h = a * hprev_ref[:, sl] + bt
        hn_ref[:, sl] = h
        hy_ref[:, sl] = (h * ybr_ref[:, sl]).astype(BF16)
    out = jnp.dot(hy_ref[...], wo_ref[...], preferred_element_type=F32)
    xo_ref[...] = x_ref[...] + gate_ref[...] * out


def _rglru_step(xin, ybr, x, gate, hist, hprev, cw, cb, grw, grb, giw, gib, lam, wo):
    rows = x.shape[0]
    return pl.pallas_call(
        _rglru_step_kernel,
        out_shape=(jax.ShapeDtypeStruct((rows, D), F32), jax.ShapeDtypeStruct((rows, D_RNN), F32)),
        scratch_shapes=[pltpu.VMEM((rows, D_RNN), BF16)],
        compiler_params=pltpu.CompilerParams(vmem_limit_bytes=VMEM_LIMIT),
        name="rglru_step",
    )(xin, ybr, x, gate, hist, hprev, cw, cb, grw, grb, giw, gib, lam, wo)


def _rope_table_kernel(cos_ref, sin_ref):
    half = ROT_DIM // 2
    shape = cos_ref.shape
    pos = lax.broadcasted_iota(jnp.int32, shape, 0).astype(F32)
    lane = lax.broadcasted_iota(jnp.int32, shape, 1)
    fidx = jnp.where(lane < half, lane, lane - half).astype(F32)
    inv_freq = jnp.exp(fidx * (-2.0 * math.log(ROPE_THETA) / ROT_DIM))
    ang = pos * inv_freq
    rot = lane < ROT_DIM
    cos_ref[...] = jnp.where(rot, jnp.cos(ang), 1.0)
    sin_ref[...] = jnp.where(rot, jnp.where(lane < half, -jnp.sin(ang), jnp.sin(ang)), 0.0)


def _rope_table(n_pos):
    return pl.pallas_call(
        _rope_table_kernel,
        out_shape=(jax.ShapeDtypeStruct((n_pos, HEAD_DIM), F32), jax.ShapeDtypeStruct((n_pos, HEAD_DIM), F32)),
        compiler_params=pltpu.CompilerParams(vmem_limit_bytes=VMEM_LIMIT),
        name="rope_table",
    )()


def _proj_qkv_kernel(x_ref, g_ref, sh_ref, sc_ref, cos_ref, sin_ref, w_ref, o_ref, u_ref, *, tn):
    j = pl.program_id(1)

    @pl.when(j == 0)
    def _():
        u_ref[...] = _norm_mod(x_ref[...], g_ref[...], sh_ref[...], sc_ref[...]).astype(BF16)

    acc = jnp.dot(u_ref[...], w_ref[...], preferred_element_type=F32)
    n_qk_tiles = 2 * N_GROUPS * D // tn

    @pl.when(j < n_qk_tiles)
    def _():
        half = ROT_DIM // 2
        cos = cos_ref[...]
        sin = sin_ref[...]
        lane = lax.broadcasted_iota(jnp.int32, cos.shape, 1)
        qscale = jnp.where(j < n_qk_tiles // 2, HEAD_DIM ** -0.5, 1.0).astype(F32)
        for h in range(tn // HEAD_DIM):
            xh = acc[:, h * HEAD_DIM:(h + 1) * HEAD_DIM]
            partner = jnp.where(lane < half, pltpu.roll(xh, HEAD_DIM - half, 1), pltpu.roll(xh, half, 1))
            o_ref[:, h * HEAD_DIM:(h + 1) * HEAD_DIM] = (xh * cos + partner * sin) * qscale

    @pl.when(j >= n_qk_tiles)
    def _():
        o_ref[...] = acc


def _proj_qkv(x, g, shift, scale, cos_t, sin_t, w, per_row, seq_len):
    rows = x.shape[0]
    tm = _row_tile(rows)
    tn = 1024
    tps = max(seq_len // tm, 1)
    mod = _mod_spec(tm, per_row, tps, 2)
    if per_row:
        rope = pl.BlockSpec((tm, HEAD_DIM), lambda i, j: (i, 0))
    else:
        rope = pl.BlockSpec((tm, HEAD_DIM), lambda i, j: (i % tps, 0))
    return pl.pallas_call(
        functools.partial(_proj_qkv_kernel, tn=tn),
        out_shape=jax.ShapeDtypeStruct((rows, QKV_WIDTH), F32),
        grid=(rows // tm, QKV_WIDTH // tn),
        in_specs=[pl.BlockSpec((tm, D), lambda i, j: (i, 0)),
                  pl.BlockSpec((1, D), lambda i, j: (0, 0)),
                  mod, mod, rope, rope,
                  pl.BlockSpec((D, tn), lambda i, j: (0, j))],
        out_specs=pl.BlockSpec((tm, tn), lambda i, j: (i, j)),
        scratch_shapes=[pltpu.VMEM((tm, D), BF16)],
        compiler_params=_cparams(("arbitrary", "arbitrary")),
        name="proj_qkv",
    )(x, g, shift, scale, cos_t, sin_t, w)


def _rows(start, dil):
    if dil == 1:
        return pl.ds(start, ATT_BLOCK)
    return pl.ds(start, ATT_BLOCK, stride=dil)


def _attn_prompt_kernel(*refs):
    in_refs, o_ref, og_ref, lg_ref = refs[:15], refs[15], refs[16], refs[17]
    n = ATT_BLOCK
    ri = lax.broadcasted_iota(jnp.int32, (n, n), 0)
    cj = lax.broadcasted_iota(jnp.int32, (n, n), 1)
    cur_ok = cj <= ri
    prev_ok = cj >= ri
    neg = -jnp.inf
    first_bias = jnp.where(pl.program_id(1) > 0, 0.0, neg).astype(F32)
    nt = (((1,), (1,)), ((), ()))
    for g, (win, dil) in enumerate(ATTN_GROUPS):
        q_ref, kc_ref, vc_ref, kp_ref, vp_ref = in_refs[5 * g:5 * g + 5]
        for s in range(ATT_TOKENS // win):
            for r in range(dil):
                rows = _rows(s * win + r, dil)
                q = q_ref[rows, :].astype(BF16)
                kc = kc_ref[rows, :].astype(BF16)
                vc = vc_ref[rows, :].astype(BF16)
                if s == 0:
                    prows = _rows(r, dil)
                    kp = kp_ref[prows, :].astype(BF16)
                    vp = vp_ref[prows, :].astype(BF16)
                else:
                    prows = _rows((s - 1) * win + r, dil)
                    kp = kc_ref[prows, :].astype(BF16)
                    vp = vc_ref[prows, :].astype(BF16)
                sp = lax.dot_general(q, kp, nt, preferred_element_type=F32)
                sc = lax.dot_general(q, kc, nt, preferred_element_type=F32)
                if s == 0:
                    sp = sp + first_bias
                sp = jnp.where(prev_ok, sp, neg)
                sc = jnp.where(cur_ok, sc, neg)
                m = jnp.maximum(jnp.max(sp, axis=-1, keepdims=True), jnp.max(sc, axis=-1, keepdims=True))
                ep = jnp.exp(sp - m)
                ec = jnp.exp(sc - m)
                l = jnp.sum(ep, axis=-1, keepdims=True) + jnp.sum(ec, axis=-1, keepdims=True)
                o = (jnp.dot(ep.astype(BF16), vp, preferred_element_type=F32)
                     + jnp.dot(ec.astype(BF16), vc, preferred_element_type=F32))
                og_ref[g, rows, :] = o / l
                lg_ref[g, rows, :] = jnp.broadcast_to(m + jnp.log(l), (n, HEAD_DIM))
    l0, l1, l2 = lg_ref[0], lg_ref[1], lg_ref[2]
    m = jnp.maximum(jnp.maximum(l0, l1), l2)
    e0, e1, e2 = jnp.exp(l0 - m), jnp.exp(l1 - m), jnp.exp(l2 - m)
    o = (e0 * og_ref[0] + e1 * og_ref[1] + e2 * og_ref[2]) / (e0 + e1 + e2)
    o_ref[...] = o.astype(o_ref.dtype)


def _attn_prompt(qkv):
    bsz, seq, _ = qkv.shape
    tb = ATT_TOKENS
    specs = []
    for g, (win, dil) in enumerate(ATTN_GROUPS):
        per = tb // win

        def cur(which, g=g):
            return pl.BlockSpec((None, tb, HEAD_DIM),
                                lambda b, i, h: (b, i, (which * N_GROUPS + g) * N_HEADS + h))

        def prev(which, g=g, win=win, per=per):
            return pl.BlockSpec((None, win, HEAD_DIM),
                                lambda b, i, h: (b, jnp.maximum(i * per - 1, 0), (which * N_GROUPS + g) * N_HEADS + h))

        specs += [cur(0), cur(1), cur(2), prev(1), prev(2)]
    return pl.pallas_call(
        _attn_prompt_kernel,
        out_shape=jax.ShapeDtypeStruct((bsz, seq, D), BF16),
        grid=(bsz, seq // tb, N_HEADS),
        in_specs=specs,
        out_specs=pl.BlockSpec((None, tb, HEAD_DIM), lambda b, i, h: (b, i, h)),
        scratch_shapes=[pltpu.VMEM((N_GROUPS, tb, HEAD_DIM), F32),
                        pltpu.VMEM((N_GROUPS, tb, HEAD_DIM), F32)],
        compiler_params=_cparams(("arbitrary", "arbitrary", "arbitrary")),
        name="attn_prompt",
    )(*([qkv] * 15))


def _attn_step_kernel(*refs):
    q_refs, kn_refs, vn_refs = refs[0:3], refs[3:6], refs[6:9]
    kc_refs, vc_refs = refs[9:12], refs[12:15]
    o_ref = refs[15]
    outs, lses = [], []
    for g in range(N_GROUPS):
        q = q_refs[g][...]
        kn = kn_refs[g][...]
        vn = vn_refs[g][...]
        s_c = jnp.sum(kc_refs[g][...] * q[None], axis=-1, keepdims=True)
        s_n = jnp.sum(kn * q, axis=-1, keepdims=True)
        m = jnp.maximum(jnp.max(s_c, axis=0), s_n)
        e_c = jnp.exp(s_c - m[None])
        e_n = jnp.exp(s_n - m)
        l = jnp.sum(e_c, axis=0) + e_n
        outs.append((jnp.sum(e_c * vc_refs[g][...], axis=0) + e_n * vn) / l)
        lses.append(m + jnp.log(l))
    m = jnp.maximum(jnp.maximum(lses[0], lses[1]), lses[2])
    es = [jnp.exp(x - m) for x in lses]
    inv = 1.0 / (es[0] + es[1] + es[2])
    o_ref[...] = (es[0] * outs[0] + es[1] * outs[1] + es[2] * outs[2]) * inv


def _attn_step(qkv_s, caches_k, caches_v):
    bsz = qkv_s.shape[0]
    q4 = qkv_s.reshape(bsz, 9, N_HEADS, HEAD_DIM)

    def new_spec(which, g):
        return pl.BlockSpec((None, None, N_HEADS, HEAD_DIM), lambda b: (b, which * N_GROUPS + g, 0, 0))

    specs = [new_spec(w, g) for w in range(3) for g in range(N_GROUPS)]
    cache_spec = pl.BlockSpec((None, ATT_BLOCK, N_HEADS, HEAD_DIM), lambda b: (b, 0, 0, 0))
    ks, vs = [], []
    for g, (win, dil) in enumerate(ATTN_GROUPS):
        ks.append(caches_k[g].reshape(bsz, win // dil, dil * N_HEADS, HEAD_DIM))
        vs.append(caches_v[g].reshape(bsz, win // dil, dil * N_HEADS, HEAD_DIM))
    o = pl.pallas_call(
        _attn_step_kernel,
        out_shape=jax.ShapeDtypeStruct((bsz, N_HEADS, HEAD_DIM), F32),
        grid=(bsz,),
        in_specs=specs + [cache_spec] * 6,
        out_specs=pl.BlockSpec((None, N_HEADS, HEAD_DIM), lambda b: (b, 0, 0)),
        compiler_params=_cparams(("arbitrary",)),
        name="attn_step",
    )(*([q4] * 9), *ks, *vs)
    return o.reshape(bsz, D)


def _out_proj_kernel(a_ref, w_ref, b_ref, x_ref, gate_ref, xo_ref):
    out = jnp.dot(a_ref[...].astype(BF16), w_ref[...], preferred_element_type=F32) + b_ref[...]
    xo_ref[...] = x_ref[...] + gate_ref[...] * out


def _out_proj(a, w, bias, x, gate, per_row, seq_len):
    rows, k = a.shape
    tm = _row_tile(rows)
    tps = max(seq_len // tm, 1)
    return pl.pallas_call(
        _out_proj_kernel,
        out_shape=jax.ShapeDtypeStruct((rows, D), F32),
        grid=(rows // tm,),
        in_specs=[pl.BlockSpec((tm, k), lambda i: (i, 0)),
                  pl.BlockSpec((k, D), lambda i: (0, 0)),
                  pl.BlockSpec((1, D), lambda i: (0, 0)),
                  pl.BlockSpec((tm, D), lambda i: (i, 0)),
                  _mod_spec(tm, per_row, tps, 1)],
        out_specs=pl.BlockSpec((tm, D), lambda i: (i, 0)),
        compiler_params=_cparams(("arbitrary",)),
        name="out_proj",
    )(a, w, bias, x, gate)


def _proj_glu_kernel(x_ref, g_ref, sh_ref, sc_ref, wa_ref, wb_ref, ba_ref, bb_ref, o_ref):
    u = _norm_mod(x_ref[...], g_ref[...], sh_ref[...], sc_ref[...]).astype(BF16)
    za = jnp.dot(u, wa_ref[...], preferred_element_type=F32) + ba_ref[...]
    zb = jnp.dot(u, wb_ref[...], preferred_element_type=F32) + bb_ref[...]
    o_ref[...] = za * _sigmoid(zb)


def _proj_glu(x, g, shift, scale, w, bias, per_row, seq_len):
    rows = x.shape[0]
    tm = _row_tile(rows)
    tps = max(seq_len // tm, 1)
    mod = _mod_spec(tm, per_row, tps, 1)
    return pl.pallas_call(
        _proj_glu_kernel,
        out_shape=jax.ShapeDtypeStruct((rows, D), F32),
        grid=(rows // tm,),
        in_specs=[pl.BlockSpec((tm, D), lambda i: (i, 0)),
                  pl.BlockSpec((1, D), lambda i: (0, 0)),
                  mod, mod,
                  pl.BlockSpec((D, D), lambda i: (0, 0)),
                  pl.BlockSpec((D, D), lambda i: (0, 1)),
                  pl.BlockSpec((1, D), lambda i: (0, 0)),
                  pl.BlockSpec((1, D), lambda i: (0, 1))],
        out_specs=pl.BlockSpec((tm, D), lambda i: (i, 0)),
        compiler_params=_cparams(("arbitrary",)),
        name="proj_glu",
    )(x, g, shift, scale, w, w, bias, bias)


def _layer_norm(y, g, b):
    mu = jnp.mean(y, axis=-1, keepdims=True)
    yc = y - mu
    var = jnp.mean(yc * yc, axis=-1, keepdims=True)
    return yc * lax.rsqrt(var + NORM_EPS) * g + b


CONF_HALO = 32
CONF_CHUNK = 64


def _conf_kernel(glu_ref, x_ref, gate_ref, wdw_ref, bdw_ref, lg_ref, lb_ref, w2_ref, b2_ref, xo_ref,
                 xx_ref, z_ref, *, tt):
    t = pl.program_id(1)

    @pl.when(t == 0)
    def _():
        xx_ref[0:CONF_HALO, :] = jnp.zeros((CONF_HALO, D), F32)

    @pl.when(t > 0)
    def _():
        xx_ref[0:CONF_HALO, :] = xx_ref[tt:tt + CONF_HALO, :]

    xx_ref[CONF_HALO:tt + CONF_HALO, :] = glu_ref[...]
    first = CONF_HALO - (CONF_WIDTH - 1)
    for r0 in range(0, tt, CONF_CHUNK):
        for c in range(D // LANES):
            sl = slice(c * LANES, (c + 1) * LANES)
            acc = jnp.broadcast_to(bdw_ref[:, sl], (CONF_CHUNK, LANES))
            for k in range(CONF_WIDTH):
                acc = acc + wdw_ref[k:k + 1, sl] * xx_ref[r0 + first + k:r0 + first + k + CONF_CHUNK, sl]
            z_ref[r0:r0 + CONF_CHUNK, sl] = acc
    y = _silu(_layer_norm(z_ref[...], lg_ref[...], lb_ref[...]))
    out = jnp.dot(y.astype(BF16), w2_ref[...], preferred_element_type=F32) + b2_ref[...]
    xo_ref[...] = x_ref[...] + gate_ref[...] * out


def _conf_prompt(glu, x, gate, wdw, bdw, lg, lb, w2, b2):
    bsz, seq, _ = glu.shape
    tt = 512
    full2 = lambda b, t: (0, 0)
    vec = pl.BlockSpec((1, D), full2)
    return pl.pallas_call(
        functools.partial(_conf_kernel, tt=tt),
        out_shape=jax.ShapeDtypeStruct((bsz, seq, D), F32),
        grid=(bsz, seq // tt),
        in_specs=[pl.BlockSpec((None, tt, D), lambda b, t: (b, t, 0)),
                  pl.BlockSpec((None, tt, D), lambda b, t: (b, t, 0)),
                  pl.BlockSpec((None, 1, D), lambda b, t: (b, 0, 0)),
                  pl.BlockSpec((CONF_WIDTH, D), full2),
                  vec, vec, vec,
                  pl.BlockSpec((D, D), full2),
                  vec],
        out_specs=pl.BlockSpec((None, tt, D), lambda b, t: (b, t, 0)),
        scratch_shapes=[pltpu.VMEM((tt + CONF_HALO, D), F32),
                        pltpu.VMEM((tt, D), F32)],
        compiler_params=_cparams(("arbitrary", "arbitrary")),
        name="conf_prompt",
    )(glu, x, gate, wdw, bdw, lg, lb, w2, b2)


def _conf_step_kernel(glu_ref, hist_ref, x_ref, gate_ref, wdw_ref, bdw_ref, lg_ref, lb_ref, w2_ref, b2_ref, xo_ref):
    acc = bdw_ref[...] + wdw_ref[CONF_WIDTH - 1:CONF_WIDTH, :] * glu_ref[...]
    for k in range(CONF_WIDTH - 1):
        acc = acc + wdw_ref[k:k + 1, :] * hist_ref[:, k * D:(k + 1) * D]
    y = _silu(_layer_norm(acc, lg_ref[...], lb_ref[...]))
    out = jnp.dot(y.astype(BF16), w2_ref[...], preferred_element_type=F32) + b2_ref[...]
    xo_ref[...] = x_ref[...] + gate_ref[...] * out


def _conf_step(glu, hist, x, gate, wdw, bdw, lg, lb, w2, b2):
    return pl.pallas_call(
        _conf_step_kernel,
        out_shape=jax.ShapeDtypeStruct(x.shape, F32),
        compiler_params=pltpu.CompilerParams(vmem_limit_bytes=VMEM_LIMIT),
        name="conf_step",
    )(glu, hist, x, gate, wdw, bdw, lg, lb, w2, b2)


def _moe_route_kernel(x_ref, g_ref, sh_ref, sc_ref, wr_ref, br_ref, tri_ref, cnt0_ref, *rest):
    u_ref, info_ref, cnt_ref, carry_ref = rest[-4], rest[-3], rest[-2], rest[-1]

    @pl.when(pl.program_id(0) == 0)
    def _():
        carry_ref[...] = cnt0_ref[...]

    u = _norm_mod(x_ref[...], g_ref[...], sh_ref[...], sc_ref[...])
    u_ref[...] = u
    logits = jnp.dot(u, wr_ref[...], preferred_element_type=F32, precision=HIGHEST) + br_ref[...]
    lane = lax.broadcasted_iota(jnp.int32, logits.shape, 1)
    neg = -jnp.inf
    big = jnp.int32(LANES)
    gl = jnp.where(lane < N_EGROUPS, logits, neg)
    gmax = jnp.max(gl, axis=-1, keepdims=True)
    gsel = jnp.min(jnp.where(gl == gmax, lane, big), axis=-1, keepdims=True)
    gp = 1.0 / jnp.sum(jnp.exp(gl - gmax), axis=-1, keepdims=True)
    base = N_EGROUPS + EXPERTS_PER_GROUP * gsel
    el = jnp.where((lane >= base) & (lane < base + EXPERTS_PER_GROUP), logits, neg)
    v1 = jnp.max(el, axis=-1, keepdims=True)
    i1 = jnp.min(jnp.where(el == v1, lane, big), axis=-1, keepdims=True)
    el2 = jnp.where(lane == i1, neg, el)
    v2 = jnp.max(el2, axis=-1, keepdims=True)
    i2 = jnp.min(jnp.where(el2 == v2, lane, big), axis=-1, keepdims=True)
    t = jnp.exp(v2 - v1)
    w1 = gp / (1.0 + t)
    w2 = gp * t / (1.0 + t)
    e1 = i1 - base
    e2 = i2 - base
    lo = jnp.minimum(e1, e2)
    hi = jnp.maximum(e1, e2)
    w_lo = jnp.where(e1 < e2, w1, w2)
    w_hi = jnp.where(e1 < e2, w2, w1)
    pair = (lo * (7 - lo)) // 2 + hi - lo - 1
    cls = gsel * N_PAIRS + pair
    onehot = jnp.where(lane == cls, 1.0, 0.0)
    before = jnp.dot(tri_ref[...], onehot.astype(BF16), preferred_element_type=F32) + carry_ref[...]
    rank = jnp.sum(onehot * before, axis=-1, keepdims=True)
    carry_ref[...] = carry_ref[...] + jnp.sum(onehot, axis=0, keepdims=True)
    cnt_ref[...] = carry_ref[...]
    info_ref[...] = jnp.where(lane == 0, cls.astype(F32),
                              jnp.where(lane == 1, w_lo, jnp.where(lane == 2, w_hi, jnp.where(lane == 3, rank, 0.0))))


def _moe_route(x, g, shift, scale, wr, br, per_row, seq_len, rows_all, row_block0, prev):
    rows = x.shape[0]
    tm = _row_tile(rows)
    tps = max(seq_len // tm, 1)
    mod = _mod_spec(tm, per_row, tps, 1)
    r = lax.broadcasted_iota(jnp.int32, (tm, tm), 0)
    c = lax.broadcasted_iota(jnp.int32, (tm, tm), 1)
    tri = (c < r).astype(BF16)
    cnt0 = jnp.zeros((1, LANES), F32) if prev is None else prev[2]
    in_specs = [pl.BlockSpec((tm, D), lambda i: (i, 0)),
                pl.BlockSpec((1, D), lambda i: (0, 0)),
                mod, mod,
                pl.BlockSpec((D, LANES), lambda i: (0, 0)),
                pl.BlockSpec((1, LANES), lambda i: (0, 0)),
                pl.BlockSpec((tm, tm), lambda i: (0, 0)),
                pl.BlockSpec((1, LANES), lambda i: (0, 0))]
    args = [x, g, shift, scale, wr, br, tri, cnt0]
    aliases = {}
    if prev is not None:
        in_specs += [pl.BlockSpec(memory_space=pl.ANY)] * 2
        args += [prev[0], prev[1]]
        aliases = {8: 0, 9: 1}
    return pl.pallas_call(
        _moe_route_kernel,
        out_shape=(jax.ShapeDtypeStruct((rows_all, D), F32), jax.ShapeDtypeStruct((rows_all, LANES), F32),
                   jax.ShapeDtypeStruct((1, LANES), F32)),
        grid=(rows // tm,),
        in_specs=in_specs,
        out_specs=(pl.BlockSpec((tm, D), lambda i: (i + row_block0, 0)),
                   pl.BlockSpec((tm, LANES), lambda i: (i + row_block0, 0)),
                   pl.BlockSpec((1, LANES), lambda i: (0, 0))),
        scratch_shapes=[pltpu.VMEM((1, LANES), F32)],
        input_output_aliases=aliases,
        compiler_params=_cparams(("arbitrary",)),
        name="moe_route",
    )(*args)


def _moe_expert_kernel(ea_ref, eb_ref, valid_ref, x_ref, info_ref, wga_ref, wua_ref, wda_ref,
                       wgb_ref, wub_ref, wdb_ref, o_ref):
    t = pl.program_id(0)

    @pl.when(valid_ref[t] == 1)
    def _():
        x = x_ref[...].astype(BF16)
        info = info_ref[...]

        def expert(wg_ref, wu_ref, wd_ref, w):
            hg = jnp.dot(x, wg_ref[...], preferred_element_type=F32)
            hu = jnp.dot(x, wu_ref[...], preferred_element_type=F32)
            act = _silu(hg) * hu * w
            return jnp.dot(act.astype(BF16), wd_ref[...], preferred_element_type=F32)

        o_ref[...] = (expert(wga_ref, wua_ref, wda_ref, info[:, 1:2])
                      + expert(wgb_ref, wub_ref, wdb_ref, info[:, 2:3]))

    @pl.when(valid_ref[t] == 0)
    def _():
        o_ref[...] = jnp.zeros(o_ref.shape, F32)


def _moe_experts(layer, tile_ea, tile_eb, tile_valid, u_sorted, info_sorted, w_gate, w_up, w_down):
    n_tiles = tile_ea.shape[0]

    def wspec(shape, which):
        if which == 0:
            return pl.BlockSpec((None, None) + shape, lambda t, ea, eb, va: (layer, ea[t], 0, 0))
        return pl.BlockSpec((None, None) + shape, lambda t, ea, eb, va: (layer, eb[t], 0, 0))

    up = (D, D_EXPERT)
    down = (D_EXPERT, D)
    grid_spec = pltpu.PrefetchScalarGridSpec(
        num_scalar_prefetch=3,
        grid=(n_tiles,),
        in_specs=[pl.BlockSpec((MOE_TILE, D), lambda t, ea, eb, va: (t, 0)),
                  pl.BlockSpec((MOE_TILE, LANES), lambda t, ea, eb, va: (t, 0)),
                  wspec(up, 0), wspec(up, 0), wspec(down, 0),
                  wspec(up, 1), wspec(up, 1), wspec(down, 1)],
        out_specs=pl.BlockSpec((MOE_TILE, D), lambda t, ea, eb, va: (t, 0)),
    )
    return pl.pallas_call(
        _moe_expert_kernel,
        out_shape=jax.ShapeDtypeStruct((n_tiles * MOE_TILE, D), F32),
        grid_spec=grid_spec,
        compiler_params=_cparams(("arbitrary",)),
        name="moe_experts",
    )(tile_ea, tile_eb, tile_valid, u_sorted, info_sorted, w_gate, w_up, w_down, w_gate, w_up, w_down)


_PAIR_LO = (0, 0, 0, 1, 1, 2)
_PAIR_HI = (1, 2, 3, 2, 3, 3)


def _lookup(table, idx):
    n = table.shape[0]
    return jnp.sum(jnp.where(idx[:, None] == jnp.arange(n, dtype=jnp.int32)[None, :], table[None, :], 0), axis=1)


def _moe_plan(info, counts_vec, n_tiles):
    rows = info.shape[0]
    cls = info[:, 0].astype(jnp.int32)
    rank = info[:, 3].astype(jnp.int32)
    counts = counts_vec[0, :N_CLASSES].astype(jnp.int32)
    padded = ((counts + MOE_TILE - 1) // MOE_TILE) * MOE_TILE
    ends = jnp.cumsum(padded)
    offs = ends - padded
    uoffs = jnp.cumsum(counts) - counts
    dest = _lookup(offs, cls) + rank
    order = jnp.argsort(cls, stable=True).astype(jnp.int32)
    pos = jnp.arange(n_tiles * MOE_TILE, dtype=jnp.int32)
    pcls = jnp.minimum(jnp.sum((pos[:, None] >= ends[None, :]).astype(jnp.int32), axis=1), N_CLASSES - 1)
    within = jnp.minimum(pos - _lookup(offs, pcls), _lookup(counts, pcls) - 1)
    src = jnp.take(order, jnp.clip(_lookup(uoffs, pcls) + within, 0, rows - 1), mode="clip")
    tile_start = jnp.arange(n_tiles, dtype=jnp.int32) * MOE_TILE
    tile_valid = (tile_start < ends[-1]).astype(jnp.int32)
    last_cls = jnp.max(jnp.where(counts > 0, jnp.arange(N_CLASSES, dtype=jnp.int32), 0))
    tcls = jnp.where(tile_valid == 1, pcls[::MOE_TILE], last_cls)
    grp = tcls // N_PAIRS
    pair = tcls % N_PAIRS
    tile_ea = grp * EXPERTS_PER_GROUP + _lookup(jnp.asarray(_PAIR_LO, jnp.int32), pair)
    tile_eb = grp * EXPERTS_PER_GROUP + _lookup(jnp.asarray(_PAIR_HI, jnp.int32), pair)
    return dest, src, tile_ea, tile_eb, tile_valid


def _residual_kernel(x_ref, y_ref, gate_ref, fg_ref, xo_ref, *, final_norm):
    xn = x_ref[...] + gate_ref[...] * y_ref[...]
    if final_norm:
        ms = jnp.mean(xn * xn, axis=-1, keepdims=True)
        xn = xn * lax.rsqrt(ms + NORM_EPS) * fg_ref[...]
    xo_ref[...] = xn


def _residual(x, y_all, gate, fg, per_row, seq_len, row_block0, final_norm):
    rows = x.shape[0]
    tm = _row_tile(rows)
    tps = max(seq_len // tm, 1)
    return pl.pallas_call(
        functools.partial(_residual_kernel, final_norm=final_norm),
        out_shape=jax.ShapeDtypeStruct((rows, D), F32),
        grid=(rows // tm,),
        in_specs=[pl.BlockSpec((tm, D), lambda i: (i, 0)),
                  pl.BlockSpec((tm, D), lambda i: (i + row_block0, 0)),
                  _mod_spec(tm, per_row, tps, 1),
                  pl.BlockSpec((1, D), lambda i: (0, 0))],
        out_specs=pl.BlockSpec((tm, D), lambda i: (i, 0)),
        compiler_params=_cparams(("arbitrary",)),
        name="residual",
    )(x, y_all, gate, fg)


def kernel(x_prompt, x_sample, c_prompt, c_sample, state_a_conv, state_a_h, cache_b_k0, cache_b_v0, cache_b_k1, cache_b_v1, cache_b_k2, cache_b_v2, state_c_conv, norm_mix_g, norm_ffn_g, ada_w, ada_b, final_norm_g, a_w_in_y, a_w_in_x, a_conv_w, a_conv_b, a_gate_r_w, a_gate_r_b, a_gate_i_w, a_gate_i_b, a_lambda, a_w_out, b_w_qkv, b_w_o, conf_w_pw1, conf_b_pw1, conf_w_dw, conf_b_dw, conf_ln_g, conf_ln_b, conf_w_pw2, conf_b_pw2, moe_w_grouter, moe_b_grouter, moe_w_erouter, moe_b_erouter, moe_w_gate, moe_w_up, moe_w_down):
    bsz, seq, _ = x_prompt.shape
    dbsz = x_sample.shape[0]
    depth = ada_w.shape[0]
    rows_p = bsz * seq
    rows_all = rows_p + dbsz
    n_tiles = rows_all // MOE_TILE + N_CLASSES
    caches_k = (cache_b_k0, cache_b_k1, cache_b_k2)
    caches_v = (cache_b_v0, cache_b_v1, cache_b_v2)

    c_rows = -(-(bsz + dbsz) // SUBLANES) * SUBLANES
    c_all = jnp.concatenate([c_prompt, c_sample, jnp.zeros((c_rows - bsz - dbsz, D), F32)], axis=0)
    mods = _ada_mod(c_all, ada_w, ada_b)

    cos_all, sin_all = _rope_table(seq + SUBLANES)
    cos_p, sin_p = cos_all[:seq], sin_all[:seq]
    cos_s = jnp.broadcast_to(cos_all[PAST_LEN:PAST_LEN + 1], (dbsz, HEAD_DIM))
    sin_s = jnp.broadcast_to(sin_all[PAST_LEN:PAST_LEN + 1], (dbsz, HEAD_DIM))

    bf = lambda w: w.astype(BF16)
    wg_bf, wu_bf, wd_bf = bf(moe_w_gate), bf(moe_w_up), bf(moe_w_down)
    a_w_in_y, a_w_in_x, a_w_out = bf(a_w_in_y), bf(a_w_in_x), bf(a_w_out)
    a_gate_r_w, a_gate_i_w = bf(a_gate_r_w), bf(a_gate_i_w)
    b_w_qkv, b_w_o = bf(b_w_qkv), bf(b_w_o)
    conf_w_pw1, conf_w_pw2 = bf(conf_w_pw1), bf(conf_w_pw2)

    xp = x_prompt.reshape(rows_p, D)
    xs = x_sample.reshape(dbsz, D)
    row1 = lambda v: v.reshape(1, -1)
    zero_bias = jnp.zeros((1, D), F32)

    a_conv_p, a_conv_s, a_h_p, a_h_s = [], [], [], []
    kp, vp, ksm, vsm = ([[] for _ in range(N_GROUPS)] for _ in range(4))
    cf_p, cf_s = [], []

    for i in range(depth):
        kind, j = i % 3, i // 3
        mp = [mods[i, :bsz, k * D:(k + 1) * D].reshape(bsz, 1, D) for k in range(6)]
        ms = [mods[i, bsz:bsz + dbsz, k * D:(k + 1) * D] for k in range(6)]
        g_mix = row1(norm_mix_g[i])
        if kind == 0:
            wts = (a_conv_w[j], row1(a_conv_b[j]), a_gate_r_w[j], row1(a_gate_r_b[j]),
                   a_gate_i_w[j], row1(a_gate_i_b[j]), row1(a_lambda[j]), a_w_out[j])
            ybr, xin = _proj_a(xp, g_mix, mp[0], mp[1], a_w_in_y[j], a_w_in_x[j], False, seq)
            xp3, h_last = _rglru_prompt(xin.reshape(bsz, seq, D_RNN), ybr.reshape(bsz, seq, D_RNN),
                                        xp.reshape(bsz, seq, D), mp[2], *wts)
            xp = xp3.reshape(rows_p, D)
            a_conv_p.append(xin.reshape(bsz, seq, D_RNN)[:, seq - (LRU_CONV - 1):])
            a_h_p.append(h_last.reshape(bsz, D_RNN))
            ybr_s, xin_s = _proj_a(xs, g_mix, ms[0], ms[1], a_w_in_y[j], a_w_in_x[j], True, 1)
            hist = state_a_conv[j]
            xs, h_new = _rglru_step(xin_s, ybr_s, xs, ms[2], hist.reshape(dbsz, (LRU_CONV - 1) * D_RNN),
                                    state_a_h[j], *wts)
            a_conv_s.append(jnp.concatenate([hist[:, 1:], xin_s[:, None, :]], axis=1))
            a_h_s.append(h_new)
        elif kind == 1:
            qkv = _proj_qkv(xp, g_mix, mp[0], mp[1], cos_p, sin_p, b_w_qkv[j], False, seq)
            qkv3 = qkv.reshape(bsz, seq, QKV_WIDTH)
            for g, (win, dil) in enumerate(ATTN_GROUPS):
                keep = min(win, seq)
                kcol = (N_GROUPS + g) * D
                vcol = (2 * N_GROUPS + g) * D
                kp[g].append(qkv3[:, seq - keep:, kcol:kcol + D].reshape(bsz, keep, N_HEADS, HEAD_DIM))
                vp[g].append(qkv3[:, seq - keep:, vcol:vcol + D].reshape(bsz, keep, N_HEADS, HEAD_DIM))
            o_p = _attn_prompt(qkv3).reshape(rows_p, D)
            xp = _out_proj(o_p, b_w_o[j], zero_bias, xp, mp[2], False, seq)
            qkv_s = _proj_qkv(xs, g_mix, ms[0], ms[1], cos_s, sin_s, b_w_qkv[j], True, 1)
            o_s = _attn_step(qkv_s, [c[j] for c in caches_k], [c[j] for c in caches_v])
            xs = _out_proj(o_s, b_w_o[j], zero_bias, xs, ms[2], True, 1)
            for g in range(N_GROUPS):
                kcol = (N_GROUPS + g) * D
                vcol = (2 * N_GROUPS + g) * D
                ksm[g].append(qkv_s[:, kcol:kcol + D].reshape(dbsz, 1, N_HEADS, HEAD_DIM))
                vsm[g].append(qkv_s[:, vcol:vcol + D].reshape(dbsz, 1, N_HEADS, HEAD_DIM))
        else:
            wts = (conf_w_dw[j], row1(conf_b_dw[j]), row1(conf_ln_g[j]), row1(conf_ln_b[j]),
                   conf_w_pw2[j], row1(conf_b_pw2[j]))
            b1 = row1(conf_b_pw1[j])
            glu = _proj_glu(xp, g_mix, mp[0], mp[1], conf_w_pw1[j], b1, False, seq)
            glu3 = glu.reshape(bsz, seq, D)
            xp = _conf_prompt(glu3, xp.reshape(bsz, seq, D), mp[2], *wts).reshape(rows_p, D)
            cf_p.append(glu3[:, seq - (CONF_WIDTH - 1):])
            glu_s = _proj_glu(xs, g_mix, ms[0], ms[1], conf_w_pw1[j], b1, True, 1)
            hist = state_c_conv[j]
            xs = _conf_step(glu_s, hist.reshape(dbsz, (CONF_WIDTH - 1) * D), xs, ms[2], *wts)
            cf_s.append(jnp.concatenate([hist[:, 1:], glu_s[:, None, :]], axis=1))

        g_ffn = row1(norm_ffn_g[i])
        wr = jnp.concatenate([moe_w_grouter[i], moe_w_erouter[i],
                              jnp.zeros((D, LANES - N_EGROUPS - N_EXPERTS), F32)], axis=1)
        br = jnp.concatenate([moe_b_grouter[i], moe_b_erouter[i],
                              jnp.zeros((LANES - N_EGROUPS - N_EXPERTS,), F32)]).reshape(1, LANES)
        joint = _moe_route(xp, g_ffn, mp[3], mp[4], wr, br, False, seq, rows_all, 0, None)
        u_all, info, counts = _moe_route(xs, g_ffn, ms[3], ms[4], wr, br, True, 1, rows_all, rows_p // dbsz, joint)
        dest, src, tile_ea, tile_eb, tile_valid = _moe_plan(info, counts, n_tiles)
        u_sorted = jnp.take(u_all, src, axis=0, mode="clip")
        info_sorted = jnp.take(info, src, axis=0, mode="clip")
        y_sorted = _moe_experts(i, tile_ea, tile_eb, tile_valid, u_sorted, info_sorted, wg_bf, wu_bf, wd_bf)
        y_all = jnp.take(y_sorted, dest, axis=0, mode="clip")
        last = i == depth - 1
        fg = row1(final_norm_g)
        xp = _residual(xp, y_all, mp[5], fg, False, seq, 0, last)
        xs = _residual(xs, y_all, ms[5], fg, True, 1, rows_p // dbsz, last)

    y_prompt = xp.reshape(bsz, seq, D)
    y_sample = xs.reshape(dbsz, 1, D)
    return (y_prompt, y_sample,
            jnp.stack(a_conv_p), jnp.stack(a_conv_s), jnp.stack(a_h_p), jnp.stack(a_h_s),
            jnp.stack(kp[0]), jnp.stack(ksm[0]), jnp.stack(vp[0]), jnp.stack(vsm[0]),
            jnp.stack(kp[1]), jnp.stack(ksm[1]), jnp.stack(vp[1]), jnp.stack(vsm[1]),
            jnp.stack(kp[2]), jnp.stack(ksm[2]), jnp.stack(vp[2]), jnp.stack(vsm[2]),
            jnp.stack(cf_p), jnp.stack(cf_s))
```

```python
import functools
import math

import jax
import jax.numpy as jnp
from jax import lax
from jax.experimental import pallas as pl
from jax.experimental.pallas import tpu as pltpu

F32 = jnp.float32
BF16 = jnp.bfloat16
HIGHEST = lax.Precision.HIGHEST

D = 1024
D_RNN = 1280
LRU_BLOCKS = 10
LRU_BLOCK = 128
LRU_CONV = 4
LRU_C = 8.0
ATTN_GROUPS = ((128, 1), (512, 4), (2048, 16))
N_GROUPS = 3
N_HEADS = 8
HEAD_DIM = 128
ROT_DIM = 32
ROPE_THETA = 500000.0
QKV_WIDTH = 9 * D
CONF_WIDTH = 31
N_EGROUPS = 4
EXPERTS_PER_GROUP = 4
N_EXPERTS = 16
D_EXPERT = 512
N_PAIRS = 6
N_CLASSES = N_EGROUPS * N_PAIRS
NORM_EPS = 1e-6
PAST_LEN = 8192

LANES = 128
SUBLANES = 8
ATT_BLOCK = 128
ATT_TOKENS = 2048
QKV_CHUNK = 256
MOE_TILE = 256
VMEM_LIMIT = 56 * 1024 * 1024


def _cparams(sem):
    return pltpu.CompilerParams(dimension_semantics=sem, vmem_limit_bytes=VMEM_LIMIT)


def _sigmoid(x):
    return 1.0 / (1.0 + jnp.exp(-x))


def _silu(x):
    return x * _sigmoid(x)


def _gelu_tanh(x):
    return 0.5 * x * (1.0 + jnp.tanh(math.sqrt(2.0 / math.pi) * (x + 0.044715 * (x * x * x))))


def _softplus(x):
    return jnp.maximum(x, 0.0) + jnp.log1p(jnp.exp(-jnp.abs(x)))


def _bdot(a, b):
    return jnp.dot(a.astype(BF16), b.astype(BF16), preferred_element_type=F32)


def _norm_mod(x, g, shift, scale):
    ms = jnp.mean(x * x, axis=-1, keepdims=True)
    y = x * lax.rsqrt(ms + NORM_EPS) * g
    return y * (1.0 + scale) + shift


def _ada_kernel(c_ref, w_ref, b_ref, o_ref):
    c = c_ref[...]
    o_ref[...] = jnp.dot(_silu(c), w_ref[...], preferred_element_type=F32, precision=HIGHEST) + b_ref[...]


def _ada_mod(c_all, ada_w, ada_b):
    depth, _, n6 = ada_w.shape
    rows = c_all.shape[0]
    tn = 1536
    return pl.pallas_call(
        _ada_kernel,
        out_shape=jax.ShapeDtypeStruct((depth, rows, n6), F32),
        grid=(depth, n6 // tn),
        in_specs=[pl.BlockSpec((rows, D), lambda l, j: (0, 0)),
                  pl.BlockSpec((None, D, tn), lambda l, j: (l, 0, j)),
                  pl.BlockSpec((None, 1, tn), lambda l, j: (l, 0, j))],
        out_specs=pl.BlockSpec((None, rows, tn), lambda l, j: (l, 0, j)),
        compiler_params=_cparams(("arbitrary", "arbitrary")),
        name="ada_mod",
    )(c_all, ada_w, ada_b.reshape(depth, 1, n6))


def _mod_spec(tm, per_row, tiles_per_seq, ngrid):
    if per_row:
        if ngrid == 1:
            return pl.BlockSpec((tm, D), lambda i: (i, 0))
        return pl.BlockSpec((tm, D), lambda i, j: (i, 0))
    if ngrid == 1:
        return pl.BlockSpec((None, 1, D), lambda i: (i // tiles_per_seq, 0, 0))
    return pl.BlockSpec((None, 1, D), lambda i, j: (i // tiles_per_seq, 0, 0))


def _row_tile(rows):
    return 1024 if rows % 1024 == 0 else rows


def _proj_a_kernel(x_ref, g_ref, sh_ref, sc_ref, wy_ref, wx_ref, y_ref, xi_ref):
    u = _norm_mod(x_ref[...], g_ref[...], sh_ref[...], sc_ref[...]).astype(BF16)
    y_ref[...] = _gelu_tanh(jnp.dot(u, wy_ref[...], preferred_element_type=F32))
    xi_ref[...] = jnp.dot(u, wx_ref[...], preferred_element_type=F32)


def _proj_a(x, g, shift, scale, w_y, w_x, per_row, seq_len):
    rows = x.shape[0]
    tm = 512 if rows % 512 == 0 else rows
    tps = max(seq_len // tm, 1)
    mod = _mod_spec(tm, per_row, tps, 1)
    wspec = pl.BlockSpec((D, D_RNN), lambda i: (0, 0))
    ospec = pl.BlockSpec((tm, D_RNN), lambda i: (i, 0))
    return pl.pallas_call(
        _proj_a_kernel,
        out_shape=(jax.ShapeDtypeStruct((rows, D_RNN), F32), jax.ShapeDtypeStruct((rows, D_RNN), F32)),
        grid=(rows // tm,),
        in_specs=[pl.BlockSpec((tm, D), lambda i: (i, 0)),
                  pl.BlockSpec((1, D), lambda i: (0, 0)),
                  mod, mod, wspec, wspec],
        out_specs=(ospec, ospec),
        compiler_params=_cparams(("arbitrary",)),
        name="proj_a",
    )(x, g, shift, scale, w_y, w_x)


def _scan_rows(a, b):
    n = a.shape[0]
    row = lax.broadcasted_iota(jnp.int32, a.shape, 0)
    d = 1
    while d < n:
        keep = row >= d
        a_sh = jnp.where(keep, pltpu.roll(a, d, 0), 1.0)
        b_sh = jnp.where(keep, pltpu.roll(b, d, 0), 0.0)
        b = a * b_sh + b
        a = a * a_sh
        d *= 2
    return a, b


def _lru_gates(xc, grw, grb, giw, gib, lam):
    xcb = xc.astype(BF16)
    r = _sigmoid(jnp.dot(xcb, grw, preferred_element_type=F32) + grb)
    i = _sigmoid(jnp.dot(xcb, giw, preferred_element_type=F32) + gib)
    log_a = -LRU_C * r * _softplus(-lam)
    a = jnp.exp(log_a)
    b = jnp.sqrt(jnp.tanh(-log_a) * (a * a + 1.0)) * (i * xc)
    return a, b


def _rglru_kernel(xin_ref, ybr_ref, x_ref, gate_ref, cw_ref, cb_ref, grw_ref, grb_ref, giw_ref, gib_ref,
                  lam_ref, wo_ref, xo_ref, hl_ref, xx_ref, hc_ref, hy_ref, *, tt):
    t = pl.program_id(1)

    @pl.when(t == 0)
    def _():
        xx_ref[0:SUBLANES, :] = jnp.zeros((SUBLANES, D_RNN), F32)
        hc_ref[...] = jnp.zeros((1, D_RNN), F32)

    @pl.when(t > 0)
    def _():
        xx_ref[0:SUBLANES, :] = xx_ref[tt:tt + SUBLANES, :]

    xx_ref[SUBLANES:tt + SUBLANES, :] = xin_ref[...]
    for n in range(LRU_BLOCKS):
        sl = slice(n * LRU_BLOCK, (n + 1) * LRU_BLOCK)
        xc = cb_ref[:, sl]
        for k in range(LRU_CONV):
            off = SUBLANES - (LRU_CONV - 1) + k
            xc = xc + cw_ref[k:k + 1, sl] * xx_ref[off:off + tt, sl]
        a, bt = _lru_gates(xc, grw_ref[n], grb_ref[:, sl], giw_ref[n], gib_ref[:, sl], lam_ref[:, sl])
        a_cum, h = _scan_rows(a, bt)
        h = h + a_cum * hc_ref[:, sl]
        hc_ref[:, sl] = h[tt - 1:tt, :]
        hy_ref[:, sl] = (h * ybr_ref[:, sl]).astype(BF16)
    out = jnp.dot(hy_ref[...], wo_ref[...], preferred_element_type=F32)
    xo_ref[...] = x_ref[...] + gate_ref[...] * out
    hl_ref[...] = hc_ref[...]


def _rglru_prompt(xin, ybr, x, gate, cw, cb, grw, grb, giw, gib, lam, wo):
    bsz, seq, _ = xin.shape
    tt = 512
    full2 = lambda b, t: (0, 0)
    full3 = lambda b, t: (0, 0, 0)
    return pl.pallas_call(
        functools.partial(_rglru_kernel, tt=tt),
        out_shape=(jax.ShapeDtypeStruct((bsz, seq, D), F32), jax.ShapeDtypeStruct((bsz, 1, D_RNN), F32)),
        grid=(bsz, seq // tt),
        in_specs=[pl.BlockSpec((None, tt, D_RNN), lambda b, t: (b, t, 0)),
                  pl.BlockSpec((None, tt, D_RNN), lambda b, t: (b, t, 0)),
                  pl.BlockSpec((None, tt, D), lambda b, t: (b, t, 0)),
                  pl.BlockSpec((None, 1, D), lambda b, t: (b, 0, 0)),
                  pl.BlockSpec((LRU_CONV, D_RNN), full2),
                  pl.BlockSpec((1, D_RNN), full2),
                  pl.BlockSpec((LRU_BLOCKS, LRU_BLOCK, LRU_BLOCK), full3),
                  pl.BlockSpec((1, D_RNN), full2),
                  pl.BlockSpec((LRU_BLOCKS, LRU_BLOCK, LRU_BLOCK), full3),
                  pl.BlockSpec((1, D_RNN), full2),
                  pl.BlockSpec((1, D_RNN), full2),
                  pl.BlockSpec((D_RNN, D), full2)],
        out_specs=(pl.BlockSpec((None, tt, D), lambda b, t: (b, t, 0)),
                   pl.BlockSpec((None, 1, D_RNN), lambda b, t: (b, 0, 0))),
        scratch_shapes=[pltpu.VMEM((tt + SUBLANES, D_RNN), F32),
                        pltpu.VMEM((1, D_RNN), F32),
                        pltpu.VMEM((tt, D_RNN), BF16)],
        compiler_params=_cparams(("arbitrary", "arbitrary")),
        name="rglru_prompt",
    )(xin, ybr, x, gate, cw, cb, grw, grb, giw, gib, lam, wo)


def _rglru_step_kernel(xin_ref, ybr_ref, x_ref, gate_ref, hist_ref, hprev_ref, cw_ref, cb_ref, grw_ref, grb_ref,
                       giw_ref, gib_ref, lam_ref, wo_ref, xo_ref, hn_ref, hy_ref):
    for n in range(LRU_BLOCKS):
        sl = slice(n * LRU_BLOCK, (n + 1) * LRU_BLOCK)
        xc = cb_ref[:, sl] + cw_ref[LRU_CONV - 1:LRU_CONV, sl] * xin_ref[:, sl]
        for k in range(LRU_CONV - 1):
            xc = xc + cw_ref[k:k + 1, sl] * hist_ref[:, k * D_RNN + n * LRU_BLOCK:k * D_RNN + (n + 1) * LRU_BLOCK]
        a, bt = _lru_gates(xc, grw_ref[n], grb_ref[:, sl], giw_ref[n], gib_ref[:, sl], lam_ref[:, sl])
        h = a * hprev_ref[:, sl] + bt
        hn_ref[:, sl] = h
        hy_ref[:, sl] = (h * ybr_ref[:, sl]).astype(BF16)
    out = jnp.dot(hy_ref[...], wo_ref[...], preferred_element_type=F32)
    xo_ref[...] = x_ref[...] + gate_ref[...] * out


def _rglru_step(xin, ybr, x, gate, hist, hprev, cw, cb, grw, grb, giw, gib, lam, wo):
    rows = x.shape[0]
    return pl.pallas_call(
        _rglru_step_kernel,
        out_shape=(jax.ShapeDtypeStruct((rows, D), F32), jax.ShapeDtypeStruct((rows, D_RNN), F32)),
        scratch_shapes=[pltpu.VMEM((rows, D_RNN), BF16)],
        compiler_params=pltpu.CompilerParams(vmem_limit_bytes=VMEM_LIMIT),
        name="rglru_step",
    )(xin, ybr, x, gate, hist, hprev, cw, cb, grw, grb, giw, gib, lam, wo)


def _rope_table_kernel(cos_ref, sin_ref):
    half = ROT_DIM // 2
    shape = cos_ref.shape
    pos = lax.broadcasted_iota(jnp.int32, shape, 0).astype(F32)
    lane = lax.broadcasted_iota(jnp.int32, shape, 1)
    fidx = jnp.where(lane < half, lane, lane - half).astype(F32)
    inv_freq = jnp.exp(fidx * (-2.0 * math.log(ROPE_THETA) / ROT_DIM))
    ang = pos * inv_freq
    rot = lane < ROT_DIM
    cos_ref[...] = jnp.where(rot, jnp.cos(ang), 1.0)
    sin_ref[...] = jnp.where(rot, jnp.where(lane < half, -jnp.sin(ang), jnp.sin(ang)), 0.0)


def _rope_table(n_pos):
    return pl.pallas_call(
        _rope_table_kernel,
        out_shape=(jax.ShapeDtypeStruct((n_pos, HEAD_DIM), F32), jax.ShapeDtypeStruct((n_pos, HEAD_DIM), F32)),
        compiler_params=pltpu.CompilerParams(vmem_limit_bytes=VMEM_LIMIT),
        name="rope_table",
    )()


def _proj_qkv_kernel(x_ref, g_ref, sh_ref, sc_ref, cos_ref, sin_ref, w_ref, o_ref, u_ref, *, tn):
    j = pl.program_id(1)

    @pl.when(j == 0)
    def _():
        u_ref[...] = _norm_mod(x_ref[...], g_ref[...], sh_ref[...], sc_ref[...]).astype(BF16)

    half = ROT_DIM // 2
    qscale = jnp.where(j < N_GROUPS * D // tn, HEAD_DIM ** -0.5, 1.0).astype(F32)
    cos = cos_ref[...] * qscale
    sin = sin_ref[...] * qscale
    lane = lax.broadcasted_iota(jnp.int32, cos.shape, 1)
    for c in range(tn // QKV_CHUNK):
        acc = jnp.dot(u_ref[...], w_ref[:, c * QKV_CHUNK:(c + 1) * QKV_CHUNK], preferred_element_type=F32)
        for h in range(QKV_CHUNK // HEAD_DIM):
            xh = acc[:, h * HEAD_DIM:(h + 1) * HEAD_DIM]
            partner = jnp.where(lane < half, pltpu.roll(xh, HEAD_DIM - half, 1), pltpu.roll(xh, half, 1))
            col = c * QKV_CHUNK + h * HEAD_DIM
            o_ref[:, col:col + HEAD_DIM] = xh * cos + partner * sin


def _proj_qkv(x, g, shift, scale, cos_t, sin_t, w, per_row, seq_len):
    rows = x.shape[0]
    tm = _row_tile(rows)
    tn = 1024
    tps = max(seq_len // tm, 1)
    mod = _mod_spec(tm, per_row, tps, 2)
    n_qk_tiles = 2 * N_GROUPS * D // tn
    n_rope = cos_t.shape[0] // tm - 1
    rope = pl.BlockSpec((tm, HEAD_DIM), lambda i, j: (jnp.where(j < n_qk_tiles, i % n_rope, n_rope), 0))
    return pl.pallas_call(
        functools.partial(_proj_qkv_kernel, tn=tn),
        out_shape=jax.ShapeDtypeStruct((rows, QKV_WIDTH), F32),
        grid=(rows // tm, QKV_WIDTH // tn),
        in_specs=[pl.BlockSpec((tm, D), lambda i, j: (i, 0)),
                  pl.BlockSpec((1, D), lambda i, j: (0, 0)),
                  mod, mod, rope, rope,
                  pl.BlockSpec((D, tn), lambda i, j: (0, j))],
        out_specs=pl.BlockSpec((tm, tn), lambda i, j: (i, j)),
        scratch_shapes=[pltpu.VMEM((tm, D), BF16)],
        compiler_params=_cparams(("arbitrary", "arbitrary")),
        name="proj_qkv",
    )(x, g, shift, scale, cos_t, sin_t, w)


def _rows(start, dil):
    if dil == 1:
        return pl.ds(start, ATT_BLOCK)
    return pl.ds(start, ATT_BLOCK, stride=dil)


def _attn_prompt_kernel(*refs):
    in_refs, o_ref, og_ref, lg_ref = refs[:15], refs[15], refs[16], refs[17]
    n = ATT_BLOCK
    nblk = ATT_TOKENS // n
    blk = lax.broadcasted_iota(jnp.int32, (nblk, n, n), 0)
    ri = lax.broadcasted_iota(jnp.int32, (nblk, n, n), 1)
    cj = lax.broadcasted_iota(jnp.int32, (nblk, n, n), 2)
    neg = -jnp.inf
    has_prev = pl.program_id(1) > 0
    qk_dims = (((2,), (2,)), ((0,), (0,)))
    pv_dims = (((2,), (1,)), ((0,), (0,)))
    for g, (win, dil) in enumerate(ATTN_GROUPS):
        q_ref, kc_ref, vc_ref, kp_ref, vp_ref = in_refs[5 * g:5 * g + 5]
        blocks = [(s, r) for s in range(ATT_TOKENS // win) for r in range(dil)]
        qs = [q_ref[_rows(s * win + r, dil), :].astype(BF16) for s, r in blocks]
        kcs = {sr: kc_ref[_rows(sr[0] * win + sr[1], dil), :].astype(BF16) for sr in blocks}
        vcs = {sr: vc_ref[_rows(sr[0] * win + sr[1], dil), :].astype(BF16) for sr in blocks}
        kps = [kcs[(s - 1, r)] if s > 0 else kp_ref[_rows(r, dil), :].astype(BF16) for s, r in blocks]
        vps = [vcs[(s - 1, r)] if s > 0 else vp_ref[_rows(r, dil), :].astype(BF16) for s, r in blocks]
        q = jnp.stack(qs)
        kc = jnp.stack([kcs[sr] for sr in blocks])
        vc = jnp.stack([vcs[sr] for sr in blocks])
        kp = jnp.stack(kps)
        vp = jnp.stack(vps)
        sp = lax.dot_general(q, kp, qk_dims, preferred_element_type=F32)
        sc = lax.dot_general(q, kc, qk_dims, preferred_element_type=F32)
        first_missing = jnp.where(has_prev, 0, dil)
        sp = jnp.where(cj >= ri, jnp.where(blk >= first_missing, sp, neg), neg)
        sc = jnp.where(cj <= ri, sc, neg)
        m = jnp.maximum(jnp.max(sp, axis=-1, keepdims=True), jnp.max(sc, axis=-1, keepdims=True))
        ep = jnp.exp(sp - m)
        ec = jnp.exp(sc - m)
        l = jnp.sum(ep, axis=-1, keepdims=True) + jnp.sum(ec, axis=-1, keepdims=True)
        o = (lax.dot_general(ep.astype(BF16), vp, pv_dims, preferred_element_type=F32)
             + lax.dot_general(ec.astype(BF16), vc, pv_dims, preferred_element_type=F32)) / l
        lse = jnp.broadcast_to(m + jnp.log(l), (nblk, n, HEAD_DIM))
        for idx, (s, r) in enumerate(blocks):
            rows = _rows(s * win + r, dil)
            og_ref[g, rows, :] = o[idx]
            lg_ref[g, rows, :] = lse[idx]
    l0, l1, l2 = lg_ref[0], lg_ref[1], lg_ref[2]
    m = jnp.maximum(jnp.maximum(l0, l1), l2)
    e0, e1, e2 = jnp.exp(l0 - m), jnp.exp(l1 - m), jnp.exp(l2 - m)
    o = (e0 * og_ref[0] + e1 * og_ref[1] + e2 * og_ref[2]) / (e0 + e1 + e2)
    o_ref[...] = o.astype(o_ref.dtype)


def _attn_prompt(qkv):
    bsz, seq, _ = qkv.shape
    tb = ATT_TOKENS
    specs = []
    for g, (win, dil) in enumerate(ATTN_GROUPS):
        per = tb // win

        def cur(which, g=g):
            return pl.BlockSpec((None, tb, HEAD_DIM),
                                lambda b, i, h: (b, i, (which * N_GROUPS + g) * N_HEADS + h))

        def prev(which, g=g, win=win, per=per):
            return pl.BlockSpec((None, win, HEAD_DIM),
                                lambda b, i, h: (b, jnp.maximum(i * per - 1, 0), (which * N_GROUPS + g) * N_HEADS + h))

        specs += [cur(0), cur(1), cur(2), prev(1), prev(2)]
    return pl.pallas_call(
        _attn_prompt_kernel,
        out_shape=jax.ShapeDtypeStruct((bsz, seq, D), BF16),
        grid=(bsz, seq // tb, N_HEADS),
        in_specs=specs,
        out_specs=pl.BlockSpec((None, tb, HEAD_DIM), lambda b, i, h: (b, i, h)),
        scratch_shapes=[pltpu.VMEM((N_GROUPS, tb, HEAD_DIM), F32),
                        pltpu.VMEM((N_GROUPS, tb, HEAD_DIM), F32)],
        compiler_params=_cparams(("arbitrary", "arbitrary", "arbitrary")),
        name="attn_prompt",
    )(*([qkv] * 15))


def _attn_step_kernel(*refs):
    q_refs, kn_refs, vn_refs = refs[0:3], refs[3:6], refs[6:9]
    kc_refs, vc_refs = refs[9:12], refs[12:15]
    o_ref = refs[15]
    outs, lses = [], []
    for g in range(N_GROUPS):
        q = q_refs[g][...]
        kn = kn_refs[g][...]
        vn = vn_refs[g][...]
        s_c = jnp.sum(kc_refs[g][...] * q[None], axis=-1, keepdims=True)
        s_n = jnp.sum(kn * q, axis=-1, keepdims=True)
        m = jnp.maximum(jnp.max(s_c, axis=0), s_n)
        e_c = jnp.exp(s_c - m[None])
        e_n = jnp.exp(s_n - m)
        l = jnp.sum(e_c, axis=0) + e_n
        outs.append((jnp.sum(e_c * vc_refs[g][...], axis=0) + e_n * vn) / l)
        lses.append(m + jnp.log(l))
    m = jnp.maximum(jnp.maximum(lses[0], lses[1]), lses[2])
    es = [jnp.exp(x - m) for x in lses]
    inv = 1.0 / (es[0] + es[1] + es[2])
    o_ref[...] = (es[0] * outs[0] + es[1] * outs[1] + es[2] * outs[2]) * inv


def _attn_step(qkv_s, caches_k, caches_v):
    bsz = qkv_s.shape[0]
    q4 = qkv_s.reshape(bsz, 9, N_HEADS, HEAD_DIM)

    def new_spec(which, g):
        return pl.BlockSpec((None, None, N_HEADS, HEAD_DIM), lambda b: (b, which * N_GROUPS + g, 0, 0))

    specs = [new_spec(w, g) for w in range(3) for g in range(N_GROUPS)]
    cache_spec = pl.BlockSpec((None, ATT_BLOCK, N_HEADS, HEAD_DIM), lambda b: (b, 0, 0, 0))
    ks, vs = [], []
    for g, (win, dil) in enumerate(ATTN_GROUPS):
        ks.append(caches_k[g].reshape(bsz, win // dil, dil * N_HEADS, HEAD_DIM))
        vs.append(caches_v[g].reshape(bsz, win // dil, dil * N_HEADS, HEAD_DIM))
    o = pl.pallas_call(
        _attn_step_kernel,
        out_shape=jax.ShapeDtypeStruct((bsz, N_HEADS, HEAD_DIM), F32),
        grid=(bsz,),
        in_specs=specs + [cache_spec] * 6,
        out_specs=pl.BlockSpec((None, N_HEADS, HEAD_DIM), lambda b: (b, 0, 0)),
        compiler_params=_cparams(("arbitrary",)),
        name="attn_step",
    )(*([q4] * 9), *ks, *vs)
    return o.reshape(bsz, D)


def _out_proj_kernel(a_ref, w_ref, b_ref, x_ref, gate_ref, xo_ref):
    out = jnp.dot(a_ref[...].astype(BF16), w_ref[...], preferred_element_type=F32) + b_ref[...]
    xo_ref[...] = x_ref[...] + gate_ref[...] * out


def _out_proj(a, w, bias, x, gate, per_row, seq_len):
    rows, k = a.shape
    tm = _row_tile(rows)
    tps = max(seq_len // tm, 1)
    return pl.pallas_call(
        _out_proj_kernel,
        out_shape=jax.ShapeDtypeStruct((rows, D), F32),
        grid=(rows // tm,),
        in_specs=[pl.BlockSpec((tm, k), lambda i: (i, 0)),
                  pl.BlockSpec((k, D), lambda i: (0, 0)),
                  pl.BlockSpec((1, D), lambda i: (0, 0)),
                  pl.BlockSpec((tm, D), lambda i: (i, 0)),
                  _mod_spec(tm, per_row, tps, 1)],
        out_specs=pl.BlockSpec((tm, D), lambda i: (i, 0)),
        compiler_params=_cparams(("arbitrary",)),
        name="out_proj",
    )(a, w, bias, x, gate)


def _proj_glu_kernel(x_ref, g_ref, sh_ref, sc_ref, wa_ref, wb_ref, ba_ref, bb_ref, o_ref):
    u = _norm_mod(x_ref[...], g_ref[...], sh_ref[...], sc_ref[...]).astype(BF16)
    za = jnp.dot(u, wa_ref[...], preferred_element_type=F32) + ba_ref[...]
    zb = jnp.dot(u, wb_ref[...], preferred_element_type=F32) + bb_ref[...]
    o_ref[...] = za * _sigmoid(zb)


def _proj_glu(x, g, shift, scale, w, bias, per_row, seq_len):
    rows = x.shape[0]
    tm = _row_tile(rows)
    tps = max(seq_len // tm, 1)
    mod = _mod_spec(tm, per_row, tps, 1)
    return pl.pallas_call(
        _proj_glu_kernel,
        out_shape=jax.ShapeDtypeStruct((rows, D), F32),
        grid=(rows // tm,),
        in_specs=[pl.BlockSpec((tm, D), lambda i: (i, 0)),
                  pl.BlockSpec((1, D), lambda i: (0, 0)),
                  mod, mod,
                  pl.BlockSpec((D, D), lambda i: (0, 0)),
                  pl.BlockSpec((D, D), lambda i: (0, 1)),
                  pl.BlockSpec((1, D), lambda i: (0, 0)),
                  pl.BlockSpec((1, D), lambda i: (0, 1))],
        out_specs=pl.BlockSpec((tm, D), lambda i: (i, 0)),
        compiler_params=_cparams(("arbitrary",)),
        name="proj_glu",
    )(x, g, shift, scale, w, w, bias, bias)


def _layer_norm(y, g, b):
    mu = jnp.mean(y, axis=-1, keepdims=True)
    yc = y - mu
    var = jnp.mean(yc * yc, axis=-1, keepdims=True)
    return yc * lax.rsqrt(var + NORM_EPS) * g + b


CONF_HALO = 32
CONF_CHUNK = 64


def _conf_kernel(glu_ref, x_ref, gate_ref, wdw_ref, bdw_ref, lg_ref, lb_ref, w2_ref, b2_ref, xo_ref,
                 xx_ref, z_ref, *, tt):
    t = pl.program_id(1)

    @pl.when(t == 0)
    def _():
        xx_ref[0:CONF_HALO, :] = jnp.zeros((CONF_HALO, D), F32)

    @pl.when(t > 0)
    def _():
        xx_ref[0:CONF_HALO, :] = xx_ref[tt:tt + CONF_HALO, :]

    xx_ref[CONF_HALO:tt + CONF_HALO, :] = glu_ref[...]
    first = CONF_HALO - (CONF_WIDTH - 1)
    for r0 in range(0, tt, CONF_CHUNK):
        for c in range(D // LANES):
            sl = slice(c * LANES, (c + 1) * LANES)
            acc = jnp.broadcast_to(bdw_ref[:, sl], (CONF_CHUNK, LANES))
            for k in range(CONF_WIDTH):
                acc = acc + wdw_ref[k:k + 1, sl] * xx_ref[r0 + first + k:r0 + first + k + CONF_CHUNK, sl]
            z_ref[r0:r0 + CONF_CHUNK, sl] = acc
    y = _silu(_layer_norm(z_ref[...], lg_ref[...], lb_ref[...]))
    out = jnp.dot(y.astype(BF16), w2_ref[...], preferred_element_type=F32) + b2_ref[...]
    xo_ref[...] = x_ref[...] + gate_ref[...] * out


def _conf_prompt(glu, x, gate, wdw, bdw, lg, lb, w2, b2):
    bsz, seq, _ = glu.shape
    tt = 512
    full2 = lambda b, t: (0, 0)
    vec = pl.BlockSpec((1, D), full2)
    return pl.pallas_call(
        functools.partial(_conf_kernel, tt=tt),
        out_shape=jax.ShapeDtypeStruct((bsz, seq, D), F32),
        grid=(bsz, seq // tt),
        in_specs=[pl.BlockSpec((None, tt, D), lambda b, t: (b, t, 0)),
                  pl.BlockSpec((None, tt, D), lambda b, t: (b, t, 0)),
                  pl.BlockSpec((None, 1, D), lambda b, t: (b, 0, 0)),
                  pl.BlockSpec((CONF_WIDTH, D), full2),
                  vec, vec, vec,
                  pl.BlockSpec((D, D), full2),
                  vec],
        out_specs=pl.BlockSpec((None, tt, D), lambda b, t: (b, t, 0)),
        scratch_shapes=[pltpu.VMEM((tt + CONF_HALO, D), F32),
                        pltpu.VMEM((tt, D), F32)],
        compiler_params=_cparams(("arbitrary", "arbitrary")),
        name="conf_prompt",
    )(glu, x, gate, wdw, bdw, lg, lb, w2, b2)


def _conf_step_kernel(glu_ref, hist_ref, x_ref, gate_ref, wdw_ref, bdw_ref, lg_ref, lb_ref, w2_ref, b2_ref, xo_ref):
    acc = bdw_ref[...] + wdw_ref[CONF_WIDTH - 1:CONF_WIDTH, :] * glu_ref[...]
    for k in range(CONF_WIDTH - 1):
        acc = acc + wdw_ref[k:k + 1, :] * hist_ref[:, k * D:(k + 1) * D]
    y = _silu(_layer_norm(acc, lg_ref[...], lb_ref[...]))
    out = jnp.dot(y.astype(BF16), w2_ref[...], preferred_element_type=F32) + b2_ref[...]
    xo_ref[...] = x_ref[...] + gate_ref[...] * out


def _conf_step(glu, hist, x, gate, wdw, bdw, lg, lb, w2, b2):
    return pl.pallas_call(
        _conf_step_kernel,
        out_shape=jax.ShapeDtypeStruct(x.shape, F32),
        compiler_params=pltpu.CompilerParams(vmem_limit_bytes=VMEM_LIMIT),
        name="conf_step",
    )(glu, hist, x, gate, wdw, bdw, lg, lb, w2, b2)


def _moe_route_kernel(x_ref, g_ref, sh_ref, sc_ref, wr_ref, br_ref, tri_ref, cnt0_ref, *rest):
    ui_ref, cnt_ref, carry_ref = rest[-3], rest[-2], rest[-1]

    @pl.when(pl.program_id(0) == 0)
    def _():
        carry_ref[...] = cnt0_ref[...]

    u = _norm_mod(x_ref[...], g_ref[...], sh_ref[...], sc_ref[...])
    ui_ref[:, :D] = u
    logits = jnp.dot(u, wr_ref[...], preferred_element_type=F32, precision=HIGHEST) + br_ref[...]
    lane = lax.broadcasted_iota(jnp.int32, logits.shape, 1)
    neg = -jnp.inf
    big = jnp.int32(LANES)
    gl = jnp.where(lane < N_EGROUPS, logits, neg)
    gmax = jnp.max(gl, axis=-1, keepdims=True)
    gsel = jnp.min(jnp.where(gl == gmax, lane, big), axis=-1, keepdims=True)
    gp = 1.0 / jnp.sum(jnp.exp(gl - gmax), axis=-1, keepdims=True)
    base = N_EGROUPS + EXPERTS_PER_GROUP * gsel
    el = jnp.where((lane >= base) & (lane < base + EXPERTS_PER_GROUP), logits, neg)
    v1 = jnp.max(el, axis=-1, keepdims=True)
    i1 = jnp.min(jnp.where(el == v1, lane, big), axis=-1, keepdims=True)
    el2 = jnp.where(lane == i1, neg, el)
    v2 = jnp.max(el2, axis=-1, keepdims=True)
    i2 = jnp.min(jnp.where(el2 == v2, lane, big), axis=-1, keepdims=True)
    t = jnp.exp(v2 - v1)
    w1 = gp / (1.0 + t)
    w2 = gp * t / (1.0 + t)
    e1 = i1 - base
    e2 = i2 - base
    lo = jnp.minimum(e1, e2)
    hi = jnp.maximum(e1, e2)
    w_lo = jnp.where(e1 < e2, w1, w2)
    w_hi = jnp.where(e1 < e2, w2, w1)
    pair = (lo * (7 - lo)) // 2 + hi - lo - 1
    cls = gsel * N_PAIRS + pair
    onehot = jnp.where(lane == cls, 1.0, 0.0)
    before = jnp.dot(tri_ref[...], onehot.astype(BF16), preferred_element_type=F32) + carry_ref[...]
    rank = jnp.sum(onehot * before, axis=-1, keepdims=True)
    carry_ref[...] = carry_ref[...] + jnp.sum(onehot, axis=0, keepdims=True)
    cnt_ref[...] = carry_ref[...]
    ui_ref[:, D:] = jnp.where(lane == 0, cls.astype(F32),
                              jnp.where(lane == 1, w_lo, jnp.where(lane == 2, w_hi, jnp.where(lane == 3, rank, 0.0))))


def _moe_route(x, g, shift, scale, wr, br, per_row, seq_len, rows_all, row_block0, prev):
    rows = x.shape[0]
    tm = _row_tile(rows)
    tps = max(seq_len // tm, 1)
    mod = _mod_spec(tm, per_row, tps, 1)
    r = lax.broadcasted_iota(jnp.int32, (tm, tm), 0)
    c = lax.broadcasted_iota(jnp.int32, (tm, tm), 1)
    tri = (c < r).astype(BF16)
    cnt0 = jnp.zeros((1, LANES), F32) if prev is None else prev[1]
    in_specs = [pl.BlockSpec((tm, D), lambda i: (i, 0)),
                pl.BlockSpec((1, D), lambda i: (0, 0)),
                mod, mod,
                pl.BlockSpec((D, LANES), lambda i: (0, 0)),
                pl.BlockSpec((1, LANES), lambda i: (0, 0)),
                pl.BlockSpec((tm, tm), lambda i: (0, 0)),
                pl.BlockSpec((1, LANES), lambda i: (0, 0))]
    args = [x, g, shift, scale, wr, br, tri, cnt0]
    aliases = {}
    if prev is not None:
        in_specs += [pl.BlockSpec(memory_space=pl.ANY)]
        args += [prev[0]]
        aliases = {8: 0}
    return pl.pallas_call(
        _moe_route_kernel,
        out_shape=(jax.ShapeDtypeStruct((rows_all, D + LANES), F32), jax.ShapeDtypeStruct((1, LANES), F32)),
        grid=(rows // tm,),
        in_specs=in_specs,
        out_specs=(pl.BlockSpec((tm, D + LANES), lambda i: (i + row_block0, 0)),
                   pl.BlockSpec((1, LANES), lambda i: (0, 0))),
        scratch_shapes=[pltpu.VMEM((1, LANES), F32)],
        input_output_aliases=aliases,
        compiler_params=_cparams(("arbitrary",)),
        name="moe_route",
    )(*args)


def _moe_expert_kernel(ea_ref, eb_ref, valid_ref, x_ref, wga_ref, wua_ref, wda_ref,
                       wgb_ref, wub_ref, wdb_ref, o_ref):
    t = pl.program_id(0)

    @pl.when(valid_ref[t] == 1)
    def _():
        x = x_ref[:, :D].astype(BF16)

        def expert(wg_ref, wu_ref, wd_ref, w):
            hg = jnp.dot(x, wg_ref[...], preferred_element_type=F32)
            hu = jnp.dot(x, wu_ref[...], preferred_element_type=F32)
            act = _silu(hg) * hu * w
            return jnp.dot(act.astype(BF16), wd_ref[...], preferred_element_type=F32)

        o_ref[...] = (expert(wga_ref, wua_ref, wda_ref, x_ref[:, D + 1:D + 2])
                      + expert(wgb_ref, wub_ref, wdb_ref, x_ref[:, D + 2:D + 3]))

    @pl.when(valid_ref[t] == 0)
    def _():
        o_ref[...] = jnp.zeros(o_ref.shape, F32)


def _moe_experts(layer, tile_ea, tile_eb, tile_valid, ui_sorted, w_gate, w_up, w_down):
    n_tiles = tile_ea.shape[0]

    def wspec(shape, which):
        if which == 0:
            return pl.BlockSpec((None, None) + shape, lambda t, ea, eb, va: (layer, ea[t], 0, 0))
        return pl.BlockSpec((None, None) + shape, lambda t, ea, eb, va: (layer, eb[t], 0, 0))

    up = (D, D_EXPERT)
    down = (D_EXPERT, D)
    grid_spec = pltpu.PrefetchScalarGridSpec(
        num_scalar_prefetch=3,
        grid=(n_tiles,),
        in_specs=[pl.BlockSpec((MOE_TILE, D + LANES), lambda t, ea, eb, va: (t, 0)),
                  wspec(up, 0), wspec(up, 0), wspec(down, 0),
                  wspec(up, 1), wspec(up, 1), wspec(down, 1)],
        out_specs=pl.BlockSpec((MOE_TILE, D), lambda t, ea, eb, va: (t, 0)),
    )
    return pl.pallas_call(
        _moe_expert_kernel,
        out_shape=jax.ShapeDtypeStruct((n_tiles * MOE_TILE, D), F32),
        grid_spec=grid_spec,
        compiler_params=_cparams(("arbitrary",)),
        name="moe_experts",
    )(tile_ea, tile_eb, tile_valid, ui_sorted, w_gate, w_up, w_down, w_gate, w_up, w_down)


_PAIR_LO = (0, 0, 0, 1, 1, 2)
_PAIR_HI = (1, 2, 3, 2, 3, 3)


def _lookup(table, idx):
    n = table.shape[0]
    return jnp.sum(jnp.where(idx[:, None] == jnp.arange(n, dtype=jnp.int32)[None, :], table[None, :], 0), axis=1)


def _moe_plan(ui, counts_vec, n_tiles):
    rows = ui.shape[0]
    cls = ui[:, D].astype(jnp.int32)
    rank = ui[:, D + 3].astype(jnp.int32)
    counts = counts_vec[0, :N_CLASSES].astype(jnp.int32)
    padded = ((counts + MOE_TILE - 1) // MOE_TILE) * MOE_TILE
    ends = jnp.cumsum(padded)
    offs = ends - padded
    uoffs = jnp.cumsum(counts) - counts
    dest = _lookup(offs, cls) + rank
    order = jnp.argsort(cls, stable=True).astype(jnp.int32)
    pos = jnp.arange(n_tiles * MOE_TILE, dtype=jnp.int32)
    pcls = jnp.minimum(jnp.sum((pos[:, None] >= ends[None, :]).astype(jnp.int32), axis=1), N_CLASSES - 1)
    within = pos - _lookup(offs, pcls)
    real = (within < _lookup(counts, pcls)) & (pos < ends[-1])
    src = jnp.take(order, jnp.clip(_lookup(uoffs, pcls) + within, 0, rows - 1), mode="clip")
    src = jnp.where(real, src, pos % rows)
    tile_start = jnp.arange(n_tiles, dtype=jnp.int32) * MOE_TILE
    tile_valid = (tile_start < ends[-1]).astype(jnp.int32)
    last_cls = jnp.max(jnp.where(counts > 0, jnp.arange(N_CLASSES, dtype=jnp.int32), 0))
    tcls = jnp.where(tile_valid == 1, pcls[::MOE_TILE], last_cls)
    grp = tcls // N_PAIRS
    pair = tcls % N_PAIRS
    tile_ea = grp * EXPERTS_PER_GROUP + _lookup(jnp.asarray(_PAIR_LO, jnp.int32), pair)
    tile_eb = grp * EXPERTS_PER_GROUP + _lookup(jnp.asarray(_PAIR_HI, jnp.int32), pair)
    return dest, src, tile_ea, tile_eb, tile_valid


def _residual_kernel(x_ref, y_ref, gate_ref, fg_ref, xo_ref, *, final_norm):
    xn = x_ref[...] + gate_ref[...] * y_ref[...]
    if final_norm:
        ms = jnp.mean(xn * xn, axis=-1, keepdims=True)
        xn = xn * lax.rsqrt(ms + NORM_EPS) * fg_ref[...]
    xo_ref[...] = xn


def _residual(x, y_all, gate, fg, per_row, seq_len, row_block0, final_norm):
    rows = x.shape[0]
    tm = _row_tile(rows)
    tps = max(seq_len // tm, 1)
    return pl.pallas_call(
        functools.partial(_residual_kernel, final_norm=final_norm),
        out_shape=jax.ShapeDtypeStruct((rows, D), F32),
        grid=(rows // tm,),
        in_specs=[pl.BlockSpec((tm, D), lambda i: (i, 0)),
                  pl.BlockSpec((tm, D), lambda i: (i + row_block0, 0)),
                  _mod_spec(tm, per_row, tps, 1),
                  pl.BlockSpec((1, D), lambda i: (0, 0))],
        out_specs=pl.BlockSpec((tm, D), lambda i: (i, 0)),
        compiler_params=_cparams(("arbitrary",)),
        name="residual",
    )(x, y_all, gate, fg)


def kernel(x_prompt, x_sample, c_prompt, c_sample, state_a_conv, state_a_h, cache_b_k0, cache_b_v0, cache_b_k1, cache_b_v1, cache_b_k2, cache_b_v2, state_c_conv, norm_mix_g, norm_ffn_g, ada_w, ada_b, final_norm_g, a_w_in_y, a_w_in_x, a_conv_w, a_conv_b, a_gate_r_w, a_gate_r_b, a_gate_i_w, a_gate_i_b, a_lambda, a_w_out, b_w_qkv, b_w_o, conf_w_pw1, conf_b_pw1, conf_w_dw, conf_b_dw, conf_ln_g, conf_ln_b, conf_w_pw2, conf_b_pw2, moe_w_grouter, moe_b_grouter, moe_w_erouter, moe_b_erouter, moe_w_gate, moe_w_up, moe_w_down):
    bsz, seq, _ = x_prompt.shape
    dbsz = x_sample.shape[0]
    depth = ada_w.shape[0]
    rows_p = bsz * seq
    rows_all = rows_p + dbsz
    n_tiles = rows_all // MOE_TILE + N_CLASSES
    caches_k = (cache_b_k0, cache_b_k1, cache_b_k2)
    caches_v = (cache_b_v0, cache_b_v1, cache_b_v2)

    c_rows = -(-(bsz + dbsz) // SUBLANES) * SUBLANES
    c_all = jnp.concatenate([c_prompt, c_sample, jnp.zeros((c_rows - bsz - dbsz, D), F32)], axis=0)
    mods = _ada_mod(c_all, ada_w, ada_b)

    cos_all, sin_all = _rope_table(seq + SUBLANES)
    tm_p = _row_tile(rows_p)
    cos_p = jnp.concatenate([cos_all[:seq], jnp.ones((tm_p, HEAD_DIM), F32)], axis=0)
    sin_p = jnp.concatenate([sin_all[:seq], jnp.zeros((tm_p, HEAD_DIM), F32)], axis=0)
    cos_s = jnp.concatenate([jnp.broadcast_to(cos_all[PAST_LEN:PAST_LEN + 1], (dbsz, HEAD_DIM)),
                             jnp.ones((dbsz, HEAD_DIM), F32)], axis=0)
    sin_s = jnp.concatenate([jnp.broadcast_to(sin_all[PAST_LEN:PAST_LEN + 1], (dbsz, HEAD_DIM)),
                             jnp.zeros((dbsz, HEAD_DIM), F32)], axis=0)

    bf = lambda w: w.astype(BF16)
    wg_bf, wu_bf, wd_bf = bf(moe_w_gate), bf(moe_w_up), bf(moe_w_down)
    a_w_in_y, a_w_in_x, a_w_out = bf(a_w_in_y), bf(a_w_in_x), bf(a_w_out)
    a_gate_r_w, a_gate_i_w = bf(a_gate_r_w), bf(a_gate_i_w)
    b_w_qkv, b_w_o = bf(b_w_qkv), bf(b_w_o)
    conf_w_pw1, conf_w_pw2 = bf(conf_w_pw1), bf(conf_w_pw2)

    xp = x_prompt.reshape(rows_p, D)
    xs = x_sample.reshape(dbsz, D)
    row1 = lambda v: v.reshape(1, -1)
    zero_bias = jnp.zeros((1, D), F32)

    a_conv_p, a_conv_s, a_h_p, a_h_s = [], [], [], []
    kp, vp, ksm, vsm = ([[] for _ in range(N_GROUPS)] for _ in range(4))
    cf_p, cf_s = [], []

    for i in range(depth):
        kind, j = i % 3, i // 3
        mp = [mods[i, :bsz, k * D:(k + 1) * D].reshape(bsz, 1, D) for k in range(6)]
        ms = [mods[i, bsz:bsz + dbsz, k * D:(k + 1) * D] for k in range(6)]
        g_mix = row1(norm_mix_g[i])
        if kind == 0:
            wts = (a_conv_w[j], row1(a_conv_b[j]), a_gate_r_w[j], row1(a_gate_r_b[j]),
                   a_gate_i_w[j], row1(a_gate_i_b[j]), row1(a_lambda[j]), a_w_out[j])
            ybr, xin = _proj_a(xp, g_mix, mp[0], mp[1], a_w_in_y[j], a_w_in_x[j], False, seq)
            xp3, h_last = _rglru_prompt(xin.reshape(bsz, seq, D_RNN), ybr.reshape(bsz, seq, D_RNN),
                                        xp.reshape(bsz, seq, D), mp[2], *wts)
            xp = xp3.reshape(rows_p, D)
            a_conv_p.append(xin.reshape(bsz, seq, D_RNN)[:, seq - (LRU_CONV - 1):])
            a_h_p.append(h_last.reshape(bsz, D_RNN))
            ybr_s, xin_s = _proj_a(xs, g_mix, ms[0], ms[1], a_w_in_y[j], a_w_in_x[j], True, 1)
            hist = state_a_conv[j]
            xs, h_new = _rglru_step(xin_s, ybr_s, xs, ms[2], hist.reshape(dbsz, (LRU_CONV - 1) * D_RNN),
                                    state_a_h[j], *wts)
            a_conv_s.append(jnp.concatenate([hist[:, 1:], xin_s[:, None, :]], axis=1))
            a_h_s.append(h_new)
        elif kind == 1:
            qkv = _proj_qkv(xp, g_mix, mp[0], mp[1], cos_p, sin_p, b_w_qkv[j], False, seq)
            qkv3 = qkv.reshape(bsz, seq, QKV_WIDTH)
            for g, (win, dil) in enumerate(ATTN_GROUPS):
                keep = min(win, seq)
                kcol = (N_GROUPS + g) * D
                vcol = (2 * N_GROUPS + g) * D
                kp[g].append(qkv3[:, seq - keep:, kcol:kcol + D].reshape(bsz, keep, N_HEADS, HEAD_DIM))
                vp[g].append(qkv3[:, seq - keep:, vcol:vcol + D].reshape(bsz, keep, N_HEADS, HEAD_DIM))
            o_p = _attn_prompt(qkv3).reshape(rows_p, D)
            xp = _out_proj(o_p, b_w_o[j], zero_bias, xp, mp[2], False, seq)
            qkv_s = _proj_qkv(xs, g_mix, ms[0], ms[1], cos_s, sin_s, b_w_qkv[j], True, 1)
            o_s = _attn_step(qkv_s, [c[j] for c in caches_k], [c[j] for c in caches_v])
            xs = _out_proj(o_s, b_w_o[j], zero_bias, xs, ms[2], True, 1)
            for g in range(N_GROUPS):
                kcol = (N_GROUPS + g) * D
                vcol = (2 * N_GROUPS + g) * D
                ksm[g].append(qkv_s[:, kcol:kcol + D].reshape(dbsz, 1, N_HEADS, HEAD_DIM))
                vsm[g].append(qkv_s[:, vcol:vcol + D].reshape(dbsz, 1, N_HEADS, HEAD_DIM))
        else:
            wts = (conf_w_dw[j], row1(conf_b_dw[j]), row1(conf_ln_g[j]), row1(conf_ln_b[j]),
                   conf_w_pw2[j], row1(conf_b_pw2[j]))
            b1 = row1(conf_b_pw1[j])
            glu = _proj_glu(xp, g_mix, mp[0], mp[1], conf_w_pw1[j], b1, False, seq)
            glu3 = glu.reshape(bsz, seq, D)
            xp = _conf_prompt(glu3, xp.reshape(bsz, seq, D), mp[2], *wts).reshape(rows_p, D)
            cf_p.append(glu3[:, seq - (CONF_WIDTH - 1):])
            glu_s = _proj_glu(xs, g_mix, ms[0], ms[1], conf_w_pw1[j], b1, True, 1)
            hist = state_c_conv[j]
            xs = _conf_step(glu_s, hist.reshape(dbsz, (CONF_WIDTH - 1) * D), xs, ms[2], *wts)
            cf_s.append(jnp.concatenate([hist[:, 1:], glu_s[:, None, :]], axis=1))

        g_ffn = row1(norm_ffn_g[i])
        wr = jnp.concatenate([moe_w_grouter[i], moe_w_erouter[i],
                              jnp.zeros((D, LANES - N_EGROUPS - N_EXPERTS), F32)], axis=1)
        br = jnp.concatenate([moe_b_grouter[i], moe_b_erouter[i],
                              jnp.zeros((LANES - N_EGROUPS - N_EXPERTS,), F32)]).reshape(1, LANES)
        joint = _moe_route(xp, g_ffn, mp[3], mp[4], wr, br, False, seq, rows_all, 0, None)
        ui_all, counts = _moe_route(xs, g_ffn, ms[3], ms[4], wr, br, True, 1, rows_all, rows_p // dbsz, joint)
        dest, src, tile_ea, tile_eb, tile_valid = _moe_plan(ui_all, counts, n_tiles)
        ui_sorted = jnp.take(ui_all, src, axis=0, mode="clip")
        y_sorted = _moe_experts(i, tile_ea, tile_eb, tile_valid, ui_sorted, wg_bf, wu_bf, wd_bf)
        y_all = jnp.take(y_sorted, dest, axis=0, mode="clip")
        last = i == depth - 1
        fg = row1(final_norm_g)
        xp = _residual(xp, y_all, mp[5], fg, False, seq, 0, last)
        xs = _residual(xs, y_all, ms[5], fg, True, 1, rows_p // dbsz, last)

    y_prompt = xp.reshape(bsz, seq, D)
    y_sample = xs.reshape(dbsz, 1, D)
    return (y_prompt, y_sample,
            jnp.stack(a_conv_p), jnp.stack(a_conv_s), jnp.stack(a_h_p), jnp.stack(a_h_s),
            jnp.stack(kp[0]), jnp.stack(ksm[0]), jnp.stack(vp[0]), jnp.stack(vsm[0]),
            jnp.stack(kp[1]), jnp.stack(ksm[1]), jnp.stack(vp[1]), jnp.stack(vsm[1]),
            jnp.stack(kp[2]), jnp.stack(ksm[2]), jnp.stack(vp[2]), jnp.stack(vsm[2]),
            jnp.stack(cf_p), jnp.stack(cf_s))
```

```python
import functools
import math

import jax
import jax.numpy as jnp
from jax import lax
from jax.experimental import pallas as pl
from jax.experimental.pallas import tpu as pltpu

F32 = jnp.float32
BF16 = jnp.bfloat16
HIGHEST = lax.Precision.HIGHEST

D = 1024
D_RNN = 1280
LRU_BLOCKS = 10
LRU_BLOCK = 128
LRU_CONV = 4
LRU_C = 8.0
ATTN_GROUPS = ((128, 1), (512, 4), (2048, 16))
N_GROUPS = 3
N_HEADS = 8
HEAD_DIM = 128
ROT_DIM = 32
ROPE_THETA = 500000.0
QKV_WIDTH = 9 * D
CONF_WIDTH = 31
N_EGROUPS = 4
EXPERTS_PER_GROUP = 4
N_EXPERTS = 16
D_EXPERT = 512
N_PAIRS = 6
N_CLASSES = N_EGROUPS * N_PAIRS
NORM_EPS = 1e-6
PAST_LEN = 8192

LANES = 128
SUBLANES = 8
ATT_BLOCK = 128
ATT_TOKENS = 2048
QKV_CHUNK = 256
MOE_TILE = 256
VMEM_LIMIT = 56 * 1024 * 1024


def _cparams(sem):
    return pltpu.CompilerParams(dimension_semantics=sem, vmem_limit_bytes=VMEM_LIMIT)


def _sigmoid(x):
    return 1.0 / (1.0 + jnp.exp(-x))


def _silu(x):
    return x * _sigmoid(x)


def _gelu_tanh(x):
    return 0.5 * x * (1.0 + jnp.tanh(math.sqrt(2.0 / math.pi) * (x + 0.044715 * (x * x * x))))


def _softplus(x):
    return jnp.maximum(x, 0.0) + jnp.log1p(jnp.exp(-jnp.abs(x)))


def _bdot(a, b):
    return jnp.dot(a.astype(BF16), b.astype(BF16), preferred_element_type=F32)


def _norm_mod(x, g, shift, scale):
    ms = jnp.mean(x * x, axis=-1, keepdims=True)
    y = x * lax.rsqrt(ms + NORM_EPS) * g
    return y * (1.0 + scale) + shift


def _ada_kernel(c_ref, w_ref, b_ref, o_ref):
    c = c_ref[...]
    o_ref[...] = jnp.dot(_silu(c), w_ref[...], preferred_element_type=F32, precision=HIGHEST) + b_ref[...]


def _ada_mod(c_all, ada_w, ada_b):
    depth, _, n6 = ada_w.shape
    rows = c_all.shape[0]
    tn = 1536
    return pl.pallas_call(
        _ada_kernel,
        out_shape=jax.ShapeDtypeStruct((depth, rows, n6), F32),
        grid=(depth, n6 // tn),
        in_specs=[pl.BlockSpec((rows, D), lambda l, j: (0, 0)),
                  pl.BlockSpec((None, D, tn), lambda l, j: (l, 0, j)),
                  pl.BlockSpec((None, 1, tn), lambda l, j: (l, 0, j))],
        out_specs=pl.BlockSpec((None, rows, tn), lambda l, j: (l, 0, j)),
        compiler_params=_cparams(("arbitrary", "arbitrary")),
        name="ada_mod",
    )(c_all, ada_w, ada_b.reshape(depth, 1, n6))


def _mod_spec(tm, per_row, tiles_per_seq, ngrid):
    if per_row:
        if ngrid == 1:
            return pl.BlockSpec((tm, D), lambda i: (i, 0))
        return pl.BlockSpec((tm, D), lambda i, j: (i, 0))
    if ngrid == 1:
        return pl.BlockSpec((None, 1, D), lambda i: (i // tiles_per_seq, 0, 0))
    return pl.BlockSpec((None, 1, D), lambda i, j: (i // tiles_per_seq, 0, 0))


def _row_tile(rows):
    return 1024 if rows % 1024 == 0 else rows


def _load_x(x_ref, pend_refs, xn_ref):
    if pend_refs is None:
        return x_ref[...]
    y_ref, gate_ref = pend_refs
    x = x_ref[...] + gate_ref[...] * y_ref[...]
    xn_ref[...] = x
    return x


def _pend_specs(pending, tm, per_row, tps, ngrid):
    if pending is None:
        return [], [], [], []
    y_all, gate, row_block0 = pending
    if ngrid == 1:
        yspec = pl.BlockSpec((tm, D), lambda i: (i + row_block0, 0))
        ospec = pl.BlockSpec((tm, D), lambda i: (i, 0))
    else:
        yspec = pl.BlockSpec((tm, D), lambda i, j: (i + row_block0, 0))
        ospec = pl.BlockSpec((tm, D), lambda i, j: (i, 0))
    return [yspec, _mod_spec(tm, per_row, tps, ngrid)], [y_all, gate], [ospec], [D]


def _proj_a_kernel(*refs, fuse):
    x_ref, refs = refs[0], refs[1:]
    pend, refs = (refs[:2], refs[2:]) if fuse else (None, refs)
    g_ref, sh_ref, sc_ref, wy_ref, wx_ref, y_ref, xi_ref = refs[:7]
    x = _load_x(x_ref, pend, refs[7] if fuse else None)
    u = _norm_mod(x, g_ref[...], sh_ref[...], sc_ref[...]).astype(BF16)
    y_ref[...] = _gelu_tanh(jnp.dot(u, wy_ref[...], preferred_element_type=F32))
    xi_ref[...] = jnp.dot(u, wx_ref[...], preferred_element_type=F32)


def _proj_a(x, g, shift, scale, w_y, w_x, per_row, seq_len, pending):
    rows = x.shape[0]
    tm = 512 if rows % 512 == 0 else rows
    tps = max(seq_len // tm, 1)
    mod = _mod_spec(tm, per_row, tps, 1)
    wspec = pl.BlockSpec((D, D_RNN), lambda i: (0, 0))
    ospec = pl.BlockSpec((tm, D_RNN), lambda i: (i, 0))
    pspecs, pargs, pout, pwidth = _pend_specs(pending, tm, per_row, tps, 1)
    outs = pl.pallas_call(
        functools.partial(_proj_a_kernel, fuse=pending is not None),
        out_shape=tuple(jax.ShapeDtypeStruct((rows, n), F32) for n in [D_RNN, D_RNN] + pwidth),
        grid=(rows // tm,),
        in_specs=[pl.BlockSpec((tm, D), lambda i: (i, 0))] + pspecs + [
            pl.BlockSpec((1, D), lambda i: (0, 0)), mod, mod, wspec, wspec],
        out_specs=tuple([ospec, ospec] + pout),
        compiler_params=_cparams(("arbitrary",)),
        name="proj_a",
    )(x, *pargs, g, shift, scale, w_y, w_x)
    return outs[0], outs[1], (outs[2] if pending is not None else x)


def _scan_rows(a, b, h0):
    n, lanes = a.shape
    groups = n // SUBLANES
    a3 = a.reshape(groups, SUBLANES, lanes)
    b3 = b.reshape(groups, SUBLANES, lanes)
    sub = lax.broadcasted_iota(jnp.int32, a3.shape, 1)
    d = 1
    while d < SUBLANES:
        keep = sub >= d
        a_sh = jnp.where(keep, pltpu.roll(a3, d, 1), 1.0)
        b_sh = jnp.where(keep, pltpu.roll(b3, d, 1), 0.0)
        b3 = a3 * b_sh + b3
        a3 = a3 * a_sh
        d *= 2
    hs = []
    h_prev = h0
    for j in range(groups):
        hj = a3[j] * h_prev + b3[j]
        hs.append(hj)
        h_prev = hj[SUBLANES - 1:SUBLANES, :]
    return jnp.concatenate(hs, axis=0), h_prev


def _lru_gates(xc, grw, grb, giw, gib, lam):
    xcb = xc.astype(BF16)
    r = _sigmoid(jnp.dot(xcb, grw, preferred_element_type=F32) + grb)
    i = _sigmoid(jnp.dot(xcb, giw, preferred_element_type=F32) + gib)
    log_a = -LRU_C * r * _softplus(-lam)
    a = jnp.exp(log_a)
    b = jnp.sqrt(jnp.tanh(-log_a) * (a * a + 1.0)) * (i * xc)
    return a, b


def _rglru_kernel(xin_ref, ybr_ref, x_ref, gate_ref, cw_ref, cb_ref, grw_ref, grb_ref, giw_ref, gib_ref,
                  lam_ref, wo_ref, xo_ref, hl_ref, xx_ref, hc_ref, hy_ref, *, tt):
    t = pl.program_id(1)

    @pl.when(t == 0)
    def _():
        xx_ref[0:SUBLANES, :] = jnp.zeros((SUBLANES, D_RNN), F32)
        hc_ref[...] = jnp.zeros((1, D_RNN), F32)

    @pl.when(t > 0)
    def _():
        xx_ref[0:SUBLANES, :] = xx_ref[tt:tt + SUBLANES, :]

    xx_ref[SUBLANES:tt + SUBLANES, :] = xin_ref[...]
    for n in range(LRU_BLOCKS):
        sl = slice(n * LRU_BLOCK, (n + 1) * LRU_BLOCK)
        xc = cb_ref[:, sl]
        for k in range(LRU_CONV):
            off = SUBLANES - (LRU_CONV - 1) + k
            xc = xc + cw_ref[k:k + 1, sl] * xx_ref[off:off + tt, sl]
        a, bt = _lru_gates(xc, grw_ref[n], grb_ref[:, sl], giw_ref[n], gib_ref[:, sl], lam_ref[:, sl])
        h, h_last = _scan_rows(a, bt, hc_ref[:, sl])
        hc_ref[:, sl] = h_last
        hy_ref[:, sl] = (h * ybr_ref[:, sl]).astype(BF16)
    out = jnp.dot(hy_ref[...], wo_ref[...], preferred_element_type=F32)
    xo_ref[...] = x_ref[...] + gate_ref[...] * out
    hl_ref[...] = hc_ref[...]


def _rglru_prompt(xin, ybr, x, gate, cw, cb, grw, grb, giw, gib, lam, wo):
    bsz, seq, _ = xin.shape
    tt = 512
    full2 = lambda b, t: (0, 0)
    full3 = lambda b, t: (0, 0, 0)
    return pl.pallas_call(
        functools.partial(_rglru_kernel, tt=tt),
        out_shape=(jax.ShapeDtypeStruct((bsz, seq, D), F32), jax.ShapeDtypeStruct((bsz, 1, D_RNN), F32)),
        grid=(bsz, seq // tt),
        in_specs=[pl.BlockSpec((None, tt, D_RNN), lambda b, t: (b, t, 0)),
                  pl.BlockSpec((None, tt, D_RNN), lambda b, t: (b, t, 0)),
                  pl.BlockSpec((None, tt, D), lambda b, t: (b, t, 0)),
                  pl.BlockSpec((None, 1, D), lambda b, t: (b, 0, 0)),
                  pl.BlockSpec((LRU_CONV, D_RNN), full2),
                  pl.BlockSpec((1, D_RNN), full2),
                  pl.BlockSpec((LRU_BLOCKS, LRU_BLOCK, LRU_BLOCK), full3),
                  pl.BlockSpec((1, D_RNN), full2),
                  pl.BlockSpec((LRU_BLOCKS, LRU_BLOCK, LRU_BLOCK), full3),
                  pl.BlockSpec((1, D_RNN), full2),
                  pl.BlockSpec((1, D_RNN), full2),
                  pl.BlockSpec((D_RNN, D), full2)],
        out_specs=(pl.BlockSpec((None, tt, D), lambda b, t: (b, t, 0)),
                   pl.BlockSpec((None, 1, D_RNN), lambda b, t: (b, 0, 0))),
        scratch_shapes=[pltpu.VMEM((tt + SUBLANES, D_RNN), F32),
                        pltpu.VMEM((1, D_RNN), F32),
                        pltpu.VMEM((tt, D_RNN), BF16)],
        compiler_params=_cparams(("arbitrary", "arbitrary")),
        name="rglru_prompt",
    )(xin, ybr, x, gate, cw, cb, grw, grb, giw, gib, lam, wo)


def _rglru_step_kernel(xin_ref, ybr_ref, x_ref, gate_ref, hist_ref, hprev_ref, cw_ref, cb_ref, grw_ref, grb_ref,
                       giw_ref, gib_ref, lam_ref, wo_ref, xo_ref, hn_ref, hy_ref):
    for n in range(LRU_BLOCKS):
        sl = slice(n * LRU_BLOCK, (n + 1) * LRU_BLOCK)
        xc = cb_ref[:, sl] + cw_ref[LRU_CONV - 1:LRU_CONV, sl] * xin_ref[:, sl]
        for k in range(LRU_CONV - 1):
            xc = xc + cw_ref[k:k + 1, sl] * hist_ref[:, k * D_RNN + n * LRU_BLOCK:k * D_RNN + (n + 1) * LRU_BLOCK]
        a, bt = _lru_gates(xc, grw_ref[n], grb_ref[:, sl], giw_ref[n], gib_ref[:, sl], lam_ref[:, sl])
        h = a * hprev_ref[:, sl] + bt
        hn_ref[:, sl] = h
        hy_ref[:, sl] = (h * ybr_ref[:, sl]).astype(BF16)
    out = jnp.dot(hy_ref[...], wo_ref[...], preferred_element_type=F32)
    xo_ref[...] = x_ref[...] + gate_ref[...] * out


def _rglru_step(xin, ybr, x, gate, hist, hprev, cw, cb, grw, grb, giw, gib, lam, wo):
    rows = x.shape[0]
    return pl.pallas_call(
        _rglru_step_kernel,
        out_shape=(jax.ShapeDtypeStruct((rows, D), F32), jax.ShapeDtypeStruct((rows, D_RNN), F32)),
        scratch_shapes=[pltpu.VMEM((rows, D_RNN), BF16)],
        compiler_params=pltpu.CompilerParams(vmem_limit_bytes=VMEM_LIMIT),
        name="rglru_step",
    )(xin, ybr, x, gate, hist, hprev, cw, cb, grw, grb, giw, gib, lam, wo)


def _rope_table_kernel(cos_ref, sin_ref):
    half = ROT_DIM // 2
    shape = cos_ref.shape
    pos = lax.broadcasted_iota(jnp.int32, shape, 0).astype(F32)
    lane = lax.broadcasted_iota(jnp.int32, shape, 1)
    fidx = jnp.where(lane < half, lane, lane - half).astype(F32)
    inv_freq = jnp.exp(fidx * (-2.0 * math.log(ROPE_THETA) / ROT_DIM))
    ang = pos * inv_freq
    rot = lane < ROT_DIM
    cos_ref[...] = jnp.where(rot, jnp.cos(ang), 1.0)
    sin_ref[...] = jnp.where(rot, jnp.where(lane < half, -jnp.sin(ang), jnp.sin(ang)), 0.0)


def _rope_table(n_pos):
    return pl.pallas_call(
        _rope_table_kernel,
        out_shape=(jax.ShapeDtypeStruct((n_pos, HEAD_DIM), F32), jax.ShapeDtypeStruct((n_pos, HEAD_DIM), F32)),
        compiler_params=pltpu.CompilerParams(vmem_limit_bytes=VMEM_LIMIT),
        name="rope_table",
    )()


def _proj_qkv_kernel(*refs, tn, fuse, swapped):
    x_ref, refs = refs[0], refs[1:]
    pend, refs = (refs[:2], refs[2:]) if fuse else (None, refs)
    g_ref, sh_ref, sc_ref, cos_ref, sin_ref, w_ref, o_ref = refs[:7]
    u_ref = refs[-1]
    j = pl.program_id(1)

    @pl.when(j == 0)
    def _():
        x = _load_x(x_ref, pend, refs[7] if fuse else None)
        u_ref[...] = _norm_mod(x, g_ref[...], sh_ref[...], sc_ref[...]).astype(BF16)

    half = ROT_DIM // 2
    qscale = jnp.where(j < N_GROUPS * D // tn, HEAD_DIM ** -0.5, 1.0).astype(F32)
    cos = cos_ref[...] * qscale
    sin = sin_ref[...] * qscale
    lane = lax.broadcasted_iota(jnp.int32, cos.shape, 1)
    for c in range(tn // QKV_CHUNK):
        acc = jnp.dot(u_ref[...], w_ref[:, c * QKV_CHUNK:(c + 1) * QKV_CHUNK], preferred_element_type=F32)
        for h in range(QKV_CHUNK // HEAD_DIM):
            xh = acc[:, h * HEAD_DIM:(h + 1) * HEAD_DIM]
            if swapped:
                partner = pltpu.roll(xh, HEAD_DIM // 2, 1)
            else:
                partner = jnp.where(lane < half, pltpu.roll(xh, HEAD_DIM - half, 1), pltpu.roll(xh, half, 1))
            col = c * QKV_CHUNK + h * HEAD_DIM
            o_ref[:, col:col + HEAD_DIM] = xh * cos + partner * sin


def _swap_rotary(a, inverse=False):
    half = ROT_DIM // 2
    mid = HEAD_DIM // 2
    if inverse:
        parts = [a[..., :half], a[..., mid:mid + half], a[..., half:mid], a[..., mid + half:]]
    else:
        parts = [a[..., :half], a[..., ROT_DIM:mid + half], a[..., half:ROT_DIM], a[..., mid + half:]]
    return jnp.concatenate(parts, axis=-1)


def _proj_qkv(x, g, shift, scale, cos_t, sin_t, w, per_row, seq_len, pending, swapped):
    rows = x.shape[0]
    tm = _row_tile(rows)
    tn = 1024
    tps = max(seq_len // tm, 1)
    mod = _mod_spec(tm, per_row, tps, 2)
    n_qk_tiles = 2 * N_GROUPS * D // tn
    n_rope = cos_t.shape[0] // tm - 1
    rope = pl.BlockSpec((tm, HEAD_DIM), lambda i, j: (jnp.where(j < n_qk_tiles, i % n_rope, n_rope), 0))
    pspecs, pargs, pout, pwidth = _pend_specs(pending, tm, per_row, tps, 2)
    outs = pl.pallas_call(
        functools.partial(_proj_qkv_kernel, tn=tn, fuse=pending is not None, swapped=swapped),
        out_shape=tuple(jax.ShapeDtypeStruct((rows, n), F32) for n in [QKV_WIDTH] + pwidth),
        grid=(rows // tm, QKV_WIDTH // tn),
        in_specs=[pl.BlockSpec((tm, D), lambda i, j: (i, 0))] + pspecs + [
            pl.BlockSpec((1, D), lambda i, j: (0, 0)),
            mod, mod, rope, rope,
            pl.BlockSpec((D, tn), lambda i, j: (0, j))],
        out_specs=tuple([pl.BlockSpec((tm, tn), lambda i, j: (i, j))] + pout),
        scratch_shapes=[pltpu.VMEM((tm, D), BF16)],
        compiler_params=_cparams(("arbitrary", "arbitrary")),
        name="proj_qkv",
    )(x, *pargs, g, shift, scale, cos_t, sin_t, w)
    return outs[0], (outs[1] if pending is not None else x)


def _rows(start, dil):
    if dil == 1:
        return pl.ds(start, ATT_BLOCK)
    return pl.ds(start, ATT_BLOCK, stride=dil)


def _attn_prompt_kernel(*refs):
    in_refs, o_ref, og_ref, lg_ref = refs[:15], refs[15], refs[16], refs[17]
    n = ATT_BLOCK
    nblk = ATT_TOKENS // n
    blk = lax.broadcasted_iota(jnp.int32, (nblk, n, n), 0)
    ri = lax.broadcasted_iota(jnp.int32, (nblk, n, n), 1)
    cj = lax.broadcasted_iota(jnp.int32, (nblk, n, n), 2)
    neg = -jnp.inf
    has_prev = pl.program_id(1) > 0
    qk_dims = (((2,), (2,)), ((0,), (0,)))
    pv_dims = (((2,), (1,)), ((0,), (0,)))
    ones = jnp.ones((nblk, n, HEAD_DIM), BF16)
    for g, (win, dil) in enumerate(ATTN_GROUPS):
        q_ref, kc_ref, vc_ref, kp_ref, vp_ref = in_refs[5 * g:5 * g + 5]
        blocks = [(s, r) for s in range(ATT_TOKENS // win) for r in range(dil)]
        qs = [q_ref[_rows(s * win + r, dil), :].astype(BF16) for s, r in blocks]
        kcs = {sr: kc_ref[_rows(sr[0] * win + sr[1], dil), :].astype(BF16) for sr in blocks}
        vcs = {sr: vc_ref[_rows(sr[0] * win + sr[1], dil), :].astype(BF16) for sr in blocks}
        kps = [kcs[(s - 1, r)] if s > 0 else kp_ref[_rows(r, dil), :].astype(BF16) for s, r in blocks]
        vps = [vcs[(s - 1, r)] if s > 0 else vp_ref[_rows(r, dil), :].astype(BF16) for s, r in blocks]
        q = jnp.stack(qs)
        kc = jnp.stack([kcs[sr] for sr in blocks])
        vc = jnp.stack([vcs[sr] for sr in blocks])
        kp = jnp.stack(kps)
        vp = jnp.stack(vps)
        sp = lax.dot_general(q, kp, qk_dims, preferred_element_type=F32)
        sc = lax.dot_general(q, kc, qk_dims, preferred_element_type=F32)
        first_missing = jnp.where(has_prev, 0, dil)
        sp = jnp.where(cj >= ri, jnp.where(blk >= first_missing, sp, neg), neg)
        sc = jnp.where(cj <= ri, sc, neg)
        m = jnp.max(jnp.maximum(sp, sc), axis=-1, keepdims=True)
        ep = jnp.exp(sp - m).astype(BF16)
        ec = jnp.exp(sc - m).astype(BF16)
        l = (lax.dot_general(ep, ones, pv_dims, preferred_element_type=F32)
             + lax.dot_general(ec, ones, pv_dims, preferred_element_type=F32))
        o = (lax.dot_general(ep, vp, pv_dims, preferred_element_type=F32)
             + lax.dot_general(ec, vc, pv_dims, preferred_element_type=F32)) / l
        lse = m + jnp.log(l)
        for idx, (s, r) in enumerate(blocks):
            rows = _rows(s * win + r, dil)
            og_ref[g, rows, :] = o[idx]
            lg_ref[g, rows, :] = lse[idx]
    l0, l1, l2 = lg_ref[0], lg_ref[1], lg_ref[2]
    m = jnp.maximum(jnp.maximum(l0, l1), l2)
    e0, e1, e2 = jnp.exp(l0 - m), jnp.exp(l1 - m), jnp.exp(l2 - m)
    o = (e0 * og_ref[0] + e1 * og_ref[1] + e2 * og_ref[2]) / (e0 + e1 + e2)
    o_ref[...] = o.astype(o_ref.dtype)


def _attn_prompt(qkv):
    bsz, seq, _ = qkv.shape
    tb = ATT_TOKENS
    specs = []
    for g, (win, dil) in enumerate(ATTN_GROUPS):
        per = tb // win

        def cur(which, g=g):
            return pl.BlockSpec((None, tb, HEAD_DIM),
                                lambda b, i, h: (b, i, (which * N_GROUPS + g) * N_HEADS + h))

        def prev(which, g=g, win=win, per=per):
            return pl.BlockSpec((None, win, HEAD_DIM),
                                lambda b, i, h: (b, jnp.maximum(i * per - 1, 0), (which * N_GROUPS + g) * N_HEADS + h))

        specs += [cur(0), cur(1), cur(2), prev(1), prev(2)]
    return pl.pallas_call(
        _attn_prompt_kernel,
        out_shape=jax.ShapeDtypeStruct((bsz, seq, D), BF16),
        grid=(bsz, seq // tb, N_HEADS),
        in_specs=specs,
        out_specs=pl.BlockSpec((None, tb, HEAD_DIM), lambda b, i, h: (b, i, h)),
        scratch_shapes=[pltpu.VMEM((N_GROUPS, tb, HEAD_DIM), F32),
                        pltpu.VMEM((N_GROUPS, tb, HEAD_DIM), F32)],
        compiler_params=_cparams(("arbitrary", "arbitrary", "arbitrary")),
        name="attn_prompt",
    )(*([qkv] * 15))


def _attn_step_kernel(*refs):
    q_refs, kn_refs, vn_refs = refs[0:3], refs[3:6], refs[6:9]
    kc_refs, vc_refs = refs[9:12], refs[12:15]
    o_ref = refs[15]
    outs, lses = [], []
    for g in range(N_GROUPS):
        q = q_refs[g][...]
        kn = kn_refs[g][...]
        vn = vn_refs[g][...]
        s_c = jnp.sum(kc_refs[g][...] * q[None], axis=-1, keepdims=True)
        s_n = jnp.sum(kn * q, axis=-1, keepdims=True)
        m = jnp.maximum(jnp.max(s_c, axis=0), s_n)
        e_c = jnp.exp(s_c - m[None])
        e_n = jnp.exp(s_n - m)
        l = jnp.sum(e_c, axis=0) + e_n
        outs.append((jnp.sum(e_c * vc_refs[g][...], axis=0) + e_n * vn) / l)
        lses.append(m + jnp.log(l))
    m = jnp.maximum(jnp.maximum(lses[0], lses[1]), lses[2])
    es = [jnp.exp(x - m) for x in lses]
    inv = 1.0 / (es[0] + es[1] + es[2])
    o_ref[...] = (es[0] * outs[0] + es[1] * outs[1] + es[2] * outs[2]) * inv


def _attn_step(qkv_s, caches_k, caches_v):
    bsz = qkv_s.shape[0]
    q4 = qkv_s.reshape(bsz, 9, N_HEADS, HEAD_DIM)

    def new_spec(which, g):
        return pl.BlockSpec((None, None, N_HEADS, HEAD_DIM), lambda b: (b, which * N_GROUPS + g, 0, 0))

    specs = [new_spec(w, g) for w in range(3) for g in range(N_GROUPS)]
    cache_spec = pl.BlockSpec((None, ATT_BLOCK, N_HEADS, HEAD_DIM), lambda b: (b, 0, 0, 0))
    ks, vs = [], []
    for g, (win, dil) in enumerate(ATTN_GROUPS):
        ks.append(caches_k[g].reshape(bsz, win // dil, dil * N_HEADS, HEAD_DIM))
        vs.append(caches_v[g].reshape(bsz, win // dil, dil * N_HEADS, HEAD_DIM))
    o = pl.pallas_call(
        _attn_step_kernel,
        out_shape=jax.ShapeDtypeStruct((bsz, N_HEADS, HEAD_DIM), F32),
        grid=(bsz,),
        in_specs=specs + [cache_spec] * 6,
        out_specs=pl.BlockSpec((None, N_HEADS, HEAD_DIM), lambda b: (b, 0, 0)),
        compiler_params=_cparams(("arbitrary",)),
        name="attn_step",
    )(*([q4] * 9), *ks, *vs)
    return o.reshape(bsz, D)


def _out_proj_kernel(a_ref, w_ref, b_ref, x_ref, gate_ref, xo_ref):
    out = jnp.dot(a_ref[...].astype(BF16), w_ref[...], preferred_element_type=F32) + b_ref[...]
    xo_ref[...] = x_ref[...] + gate_ref[...] * out


def _out_proj(a, w, bias, x, gate, per_row, seq_len):
    rows, k = a.shape
    tm = _row_tile(rows)
    tps = max(seq_len // tm, 1)
    return pl.pallas_call(
        _out_proj_kernel,
        out_shape=jax.ShapeDtypeStruct((rows, D), F32),
        grid=(rows // tm,),
        in_specs=[pl.BlockSpec((tm, k), lambda i: (i, 0)),
                  pl.BlockSpec((k, D), lambda i: (0, 0)),
                  pl.BlockSpec((1, D), lambda i: (0, 0)),
                  pl.BlockSpec((tm, D), lambda i: (i, 0)),
                  _mod_spec(tm, per_row, tps, 1)],
        out_specs=pl.BlockSpec((tm, D), lambda i: (i, 0)),
        compiler_params=_cparams(("arbitrary",)),
        name="out_proj",
    )(a, w, bias, x, gate)


def _proj_glu_kernel(*refs, fuse):
    x_ref, refs = refs[0], refs[1:]
    pend, refs = (refs[:2], refs[2:]) if fuse else (None, refs)
    g_ref, sh_ref, sc_ref, wa_ref, wb_ref, ba_ref, bb_ref, o_ref = refs[:8]
    x = _load_x(x_ref, pend, refs[8] if fuse else None)
    u = _norm_mod(x, g_ref[...], sh_ref[...], sc_ref[...]).astype(BF16)
    za = jnp.dot(u, wa_ref[...], preferred_element_type=F32) + ba_ref[...]
    zb = jnp.dot(u, wb_ref[...], preferred_element_type=F32) + bb_ref[...]
    o_ref[...] = za * _sigmoid(zb)


def _proj_glu(x, g, shift, scale, w, bias, per_row, seq_len, pending):
    rows = x.shape[0]
    tm = _row_tile(rows)
    tps = max(seq_len // tm, 1)
    mod = _mod_spec(tm, per_row, tps, 1)
    pspecs, pargs, pout, pwidth = _pend_specs(pending, tm, per_row, tps, 1)
    outs = pl.pallas_call(
        functools.partial(_proj_glu_kernel, fuse=pending is not None),
        out_shape=tuple(jax.ShapeDtypeStruct((rows, n), F32) for n in [D] + pwidth),
        grid=(rows // tm,),
        in_specs=[pl.BlockSpec((tm, D), lambda i: (i, 0))] + pspecs + [
            pl.BlockSpec((1, D), lambda i: (0, 0)),
            mod, mod,
            pl.BlockSpec((D, D), lambda i: (0, 0)),
            pl.BlockSpec((D, D), lambda i: (0, 1)),
            pl.BlockSpec((1, D), lambda i: (0, 0)),
            pl.BlockSpec((1, D), lambda i: (0, 1))],
        out_specs=tuple([pl.BlockSpec((tm, D), lambda i: (i, 0))] + pout),
        compiler_params=_cparams(("arbitrary",)),
        name="proj_glu",
    )(x, *pargs, g, shift, scale, w, w, bias, bias)
    return outs[0], (outs[1] if pending is not None else x)


def _layer_norm(y, g, b):
    mu = jnp.mean(y, axis=-1, keepdims=True)
    yc = y - mu
    var = jnp.mean(yc * yc, axis=-1, keepdims=True)
    return yc * lax.rsqrt(var + NORM_EPS) * g + b


CONF_HALO = 32
CONF_CHUNK = 128


def _conf_kernel(glu_ref, x_ref, gate_ref, wdw_ref, bdw_ref, lg_ref, lb_ref, w2_ref, b2_ref, xo_ref,
                 xx_ref, z_ref, p_ref, *, tt):
    t = pl.program_id(1)

    @pl.when(t == 0)
    def _():
        xx_ref[0:CONF_HALO, :] = jnp.zeros((CONF_HALO, D), F32)

    @pl.when(t > 0)
    def _():
        xx_ref[0:CONF_HALO, :] = xx_ref[tt:tt + CONF_HALO, :]

    xx_ref[CONF_HALO:tt + CONF_HALO, :] = glu_ref[...]
    first = CONF_HALO - (CONF_WIDTH - 1)
    for r0 in range(0, tt, CONF_CHUNK):
        for c in range(D // LANES):
            sl = slice(c * LANES, (c + 1) * LANES)
            acc = jnp.broadcast_to(bdw_ref[:, sl], (CONF_CHUNK, LANES))
            for r in range(SUBLANES):
                taps = [k for k in range(CONF_WIDTH) if (first + k) % SUBLANES == r]
                span = CONF_CHUNK if r == 0 else CONF_CHUNK + SUBLANES
                p = None
                for k in taps:
                    base = r0 + first + k - r
                    term = wdw_ref[k:k + 1, sl] * xx_ref[base:base + span, sl]
                    p = term if p is None else p + term
                if r == 0:
                    acc = acc + p
                else:
                    p_ref[r, :, :] = p
                    acc = acc + p_ref[r, r:r + CONF_CHUNK, :]
            z_ref[r0:r0 + CONF_CHUNK, sl] = acc
    y = _silu(_layer_norm(z_ref[...], lg_ref[...], lb_ref[...]))
    out = jnp.dot(y.astype(BF16), w2_ref[...], preferred_element_type=F32) + b2_ref[...]
    xo_ref[...] = x_ref[...] + gate_ref[...] * out


def _conf_prompt(glu, x, gate, wdw, bdw, lg, lb, w2, b2):
    bsz, seq, _ = glu.shape
    tt = 512
    full2 = lambda b, t: (0, 0)
    vec = pl.BlockSpec((1, D), full2)
    return pl.pallas_call(
        functools.partial(_conf_kernel, tt=tt),
        out_shape=jax.ShapeDtypeStruct((bsz, seq, D), F32),
        grid=(bsz, seq // tt),
        in_specs=[pl.BlockSpec((None, tt, D), lambda b, t: (b, t, 0)),
                  pl.BlockSpec((None, tt, D), lambda b, t: (b, t, 0)),
                  pl.BlockSpec((None, 1, D), lambda b, t: (b, 0, 0)),
                  pl.BlockSpec((CONF_WIDTH, D), full2),
                  vec, vec, vec,
                  pl.BlockSpec((D, D), full2),
                  vec],
        out_specs=pl.BlockSpec((None, tt, D), lambda b, t: (b, t, 0)),
        scratch_shapes=[pltpu.VMEM((tt + CONF_HALO, D), F32),
                        pltpu.VMEM((tt, D), F32),
                        pltpu.VMEM((SUBLANES, CONF_CHUNK + SUBLANES, LANES), F32)],
        compiler_params=_cparams(("arbitrary", "arbitrary")),
        name="conf_prompt",
    )(glu, x, gate, wdw, bdw, lg, lb, w2, b2)


def _conf_step_kernel(glu_ref, hist_ref, x_ref, gate_ref, wdw_ref, bdw_ref, lg_ref, lb_ref, w2_ref, b2_ref, xo_ref):
    acc = bdw_ref[...] + wdw_ref[CONF_WIDTH - 1:CONF_WIDTH, :] * glu_ref[...]
    for k in range(CONF_WIDTH - 1):
        acc = acc + wdw_ref[k:k + 1, :] * hist_ref[:, k * D:(k + 1) * D]
    y = _silu(_layer_norm(acc, lg_ref[...], lb_ref[...]))
    out = jnp.dot(y.astype(BF16), w2_ref[...], preferred_element_type=F32) + b2_ref[...]
    xo_ref[...] = x_ref[...] + gate_ref[...] * out


def _conf_step(glu, hist, x, gate, wdw, bdw, lg, lb, w2, b2):
    return pl.pallas_call(
        _conf_step_kernel,
        out_shape=jax.ShapeDtypeStruct(x.shape, F32),
        compiler_params=pltpu.CompilerParams(vmem_limit_bytes=VMEM_LIMIT),
        name="conf_step",
    )(glu, hist, x, gate, wdw, bdw, lg, lb, w2, b2)


def _moe_route_kernel(x_ref, g_ref, sh_ref, sc_ref, wr_ref, br_ref, tri_ref, cnt0_ref, *rest):
    ui_ref, cnt_ref, carry_ref = rest[-3], rest[-2], rest[-1]

    @pl.when(pl.program_id(0) == 0)
    def _():
        carry_ref[...] = cnt0_ref[...]

    u = _norm_mod(x_ref[...], g_ref[...], sh_ref[...], sc_ref[...])
    ui_ref[:, :D] = u
    u_hi = u.astype(BF16)
    u_lo = (u - u_hi.astype(F32)).astype(BF16)
    logits = (jnp.dot(u_hi, wr_ref[0], preferred_element_type=F32)
              + jnp.dot(u_lo, wr_ref[0], preferred_element_type=F32)
              + jnp.dot(u_hi, wr_ref[1], preferred_element_type=F32)) + br_ref[...]
    lane = lax.broadcasted_iota(jnp.int32, logits.shape, 1)
    neg = -jnp.inf
    big = jnp.int32(LANES)
    gl = jnp.where(lane < N_EGROUPS, logits, neg)
    gmax = jnp.max(gl, axis=-1, keepdims=True)
    gsel = jnp.min(jnp.where(gl == gmax, lane, big), axis=-1, keepdims=True)
    gp = 1.0 / jnp.sum(jnp.exp(gl - gmax), axis=-1, keepdims=True)
    base = N_EGROUPS + EXPERTS_PER_GROUP * gsel
    el = jnp.where((lane >= base) & (lane < base + EXPERTS_PER_GROUP), logits, neg)
    v1 = jnp.max(el, axis=-1, keepdims=True)
    i1 = jnp.min(jnp.where(el == v1, lane, big), axis=-1, keepdims=True)
    el2 = jnp.where(lane == i1, neg, el)
    v2 = jnp.max(el2, axis=-1, keepdims=True)
    i2 = jnp.min(jnp.where(el2 == v2, lane, big), axis=-1, keepdims=True)
    t = jnp.exp(v2 - v1)
    w1 = gp / (1.0 + t)
    w2 = gp * t / (1.0 + t)
    e1 = i1 - base
    e2 = i2 - base
    lo = jnp.minimum(e1, e2)
    hi = jnp.maximum(e1, e2)
    w_lo = jnp.where(e1 < e2, w1, w2)
    w_hi = jnp.where(e1 < e2, w2, w1)
    pair = (lo * (7 - lo)) // 2 + hi - lo - 1
    cls = gsel * N_PAIRS + pair
    onehot = jnp.where(lane == cls, 1.0, 0.0)
    before = jnp.dot(tri_ref[...], onehot.astype(BF16), preferred_element_type=F32) + carry_ref[...]
    rank = jnp.sum(onehot * before, axis=-1, keepdims=True)
    carry_ref[...] = carry_ref[...] + jnp.sum(onehot, axis=0, keepdims=True)
    cnt_ref[...] = carry_ref[...]
    ui_ref[:, D:] = jnp.where(lane == 0, cls.astype(F32),
                              jnp.where(lane == 1, w_lo, jnp.where(lane == 2, w_hi, jnp.where(lane == 3, rank, 0.0))))


def _moe_route(x, g, shift, scale, wr, br, per_row, seq_len, rows_all, row_block0, prev):
    rows = x.shape[0]
    tm = _row_tile(rows)
    tps = max(seq_len // tm, 1)
    mod = _mod_spec(tm, per_row, tps, 1)
    r = lax.broadcasted_iota(jnp.int32, (tm, tm), 0)
    c = lax.broadcasted_iota(jnp.int32, (tm, tm), 1)
    tri = (c < r).astype(BF16)
    cnt0 = jnp.zeros((1, LANES), F32) if prev is None else prev[1]
    in_specs = [pl.BlockSpec((tm, D), lambda i: (i, 0)),
                pl.BlockSpec((1, D), lambda i: (0, 0)),
                mod, mod,
                pl.BlockSpec((2, D, LANES), lambda i: (0, 0, 0)),
                pl.BlockSpec((1, LANES), lambda i: (0, 0)),
                pl.BlockSpec((tm, tm), lambda i: (0, 0)),
                pl.BlockSpec((1, LANES), lambda i: (0, 0))]
    args = [x, g, shift, scale, wr, br, tri, cnt0]
    aliases = {}
    if prev is not None:
        in_specs += [pl.BlockSpec(memory_space=pl.ANY)]
        args += [prev[0]]
        aliases = {8: 0}
    return pl.pallas_call(
        _moe_route_kernel,
        out_shape=(jax.ShapeDtypeStruct((rows_all, D + LANES), F32), jax.ShapeDtypeStruct((1, LANES), F32)),
        grid=(rows // tm,),
        in_specs=in_specs,
        out_specs=(pl.BlockSpec((tm, D + LANES), lambda i: (i + row_block0, 0)),
                   pl.BlockSpec((1, LANES), lambda i: (0, 0))),
        scratch_shapes=[pltpu.VMEM((1, LANES), F32)],
        input_output_aliases=aliases,
        compiler_params=_cparams(("arbitrary",)),
        name="moe_route",
    )(*args)


def _moe_expert_kernel(ea_ref, eb_ref, valid_ref, x_ref, wga_ref, wua_ref, wda_ref,
                       wgb_ref, wub_ref, wdb_ref, o_ref):
    t = pl.program_id(0)

    @pl.when(valid_ref[t] == 1)
    def _():
        x = x_ref[:, :D].astype(BF16)

        def expert(wg_ref, wu_ref, wd_ref, w):
            hg = jnp.dot(x, wg_ref[...], preferred_element_type=F32)
            hu = jnp.dot(x, wu_ref[...], preferred_element_type=F32)
            act = _silu(hg) * hu * w
            return jnp.dot(act.astype(BF16), wd_ref[...], preferred_element_type=F32)

        o_ref[...] = (expert(wga_ref, wua_ref, wda_ref, x_ref[:, D + 1:D + 2])
                      + expert(wgb_ref, wub_ref, wdb_ref, x_ref[:, D + 2:D + 3]))

    @pl.when(valid_ref[t] == 0)
    def _():
        o_ref[...] = jnp.zeros(o_ref.shape, F32)


def _moe_experts(layer, tile_ea, tile_eb, tile_valid, ui_sorted, w_gate, w_up, w_down):
    n_tiles = tile_ea.shape[0]

    def wspec(shape, which):
        if which == 0:
            return pl.BlockSpec((None, None) + shape, lambda t, ea, eb, va: (layer, ea[t], 0, 0))
        return pl.BlockSpec((None, None) + shape, lambda t, ea, eb, va: (layer, eb[t], 0, 0))

    up = (D, D_EXPERT)
    down = (D_EXPERT, D)
    grid_spec = pltpu.PrefetchScalarGridSpec(
        num_scalar_prefetch=3,
        grid=(n_tiles,),
        in_specs=[pl.BlockSpec((MOE_TILE, D + LANES), lambda t, ea, eb, va: (t, 0)),
                  wspec(up, 0), wspec(up, 0), wspec(down, 0),
                  wspec(up, 1), wspec(up, 1), wspec(down, 1)],
        out_specs=pl.BlockSpec((MOE_TILE, D), lambda t, ea, eb, va: (t, 0)),
    )
    return pl.pallas_call(
        _moe_expert_kernel,
        out_shape=jax.ShapeDtypeStruct((n_tiles * MOE_TILE, D), F32),
        grid_spec=grid_spec,
        compiler_params=_cparams(("arbitrary",)),
        name="moe_experts",
    )(tile_ea, tile_eb, tile_valid, ui_sorted, w_gate, w_up, w_down, w_gate, w_up, w_down)


_PAIR_LO = (0, 0, 0, 1, 1, 2)
_PAIR_HI = (1, 2, 3, 2, 3, 3)


def _lookup(table, idx):
    n = table.shape[0]
    return jnp.sum(jnp.where(idx[:, None] == jnp.arange(n, dtype=jnp.int32)[None, :], table[None, :], 0), axis=1)


def _moe_plan(ui, counts_vec, n_tiles):
    rows = ui.shape[0]
    cls = ui[:, D].astype(jnp.int32)
    rank = ui[:, D + 3].astype(jnp.int32)
    counts = counts_vec[0, :N_CLASSES].astype(jnp.int32)
    padded = ((counts + MOE_TILE - 1) // MOE_TILE) * MOE_TILE
    ends = jnp.cumsum(padded)
    offs = ends - padded
    uoffs = jnp.cumsum(counts) - counts
    dest = _lookup(offs, cls) + rank
    order = jnp.argsort(cls, stable=True).astype(jnp.int32)
    pos = jnp.arange(n_tiles * MOE_TILE, dtype=jnp.int32)
    pcls = jnp.minimum(jnp.sum((pos[:, None] >= ends[None, :]).astype(jnp.int32), axis=1), N_CLASSES - 1)
    within = pos - _lookup(offs, pcls)
    real = (within < _lookup(counts, pcls)) & (pos < ends[-1])
    src = jnp.take(order, jnp.clip(_lookup(uoffs, pcls) + within, 0, rows - 1), mode="clip")
    src = jnp.where(real, src, pos % rows)
    tile_start = jnp.arange(n_tiles, dtype=jnp.int32) * MOE_TILE
    tile_valid = (tile_start < ends[-1]).astype(jnp.int32)
    last_cls = jnp.max(jnp.where(counts > 0, jnp.arange(N_CLASSES, dtype=jnp.int32), 0))
    tcls = jnp.where(tile_valid == 1, pcls[::MOE_TILE], last_cls)
    grp = tcls // N_PAIRS
    pair = tcls % N_PAIRS
    tile_ea = grp * EXPERTS_PER_GROUP + _lookup(jnp.asarray(_PAIR_LO, jnp.int32), pair)
    tile_eb = grp * EXPERTS_PER_GROUP + _lookup(jnp.asarray(_PAIR_HI, jnp.int32), pair)
    return dest, src, tile_ea, tile_eb, tile_valid


def _residual_kernel(x_ref, y_ref, gate_ref, fg_ref, xo_ref, *, final_norm):
    xn = x_ref[...] + gate_ref[...] * y_ref[...]
    if final_norm:
        ms = jnp.mean(xn * xn, axis=-1, keepdims=True)
        xn = xn * lax.rsqrt(ms + NORM_EPS) * fg_ref[...]
    xo_ref[...] = xn


def _residual(x, y_all, gate, fg, per_row, seq_len, row_block0, final_norm):
    rows = x.shape[0]
    tm = _row_tile(rows)
    tps = max(seq_len // tm, 1)
    return pl.pallas_call(
        functools.partial(_residual_kernel, final_norm=final_norm),
        out_shape=jax.ShapeDtypeStruct((rows, D), F32),
        grid=(rows // tm,),
        in_specs=[pl.BlockSpec((tm, D), lambda i: (i, 0)),
                  pl.BlockSpec((tm, D), lambda i: (i + row_block0, 0)),
                  _mod_spec(tm, per_row, tps, 1),
                  pl.BlockSpec((1, D), lambda i: (0, 0))],
        out_specs=pl.BlockSpec((tm, D), lambda i: (i, 0)),
        compiler_params=_cparams(("arbitrary",)),
        name="residual",
    )(x, y_all, gate, fg)


def kernel(x_prompt, x_sample, c_prompt, c_sample, state_a_conv, state_a_h, cache_b_k0, cache_b_v0, cache_b_k1, cache_b_v1, cache_b_k2, cache_b_v2, state_c_conv, norm_mix_g, norm_ffn_g, ada_w, ada_b, final_norm_g, a_w_in_y, a_w_in_x, a_conv_w, a_conv_b, a_gate_r_w, a_gate_r_b, a_gate_i_w, a_gate_i_b, a_lambda, a_w_out, b_w_qkv, b_w_o, conf_w_pw1, conf_b_pw1, conf_w_dw, conf_b_dw, conf_ln_g, conf_ln_b, conf_w_pw2, conf_b_pw2, moe_w_grouter, moe_b_grouter, moe_w_erouter, moe_b_erouter, moe_w_gate, moe_w_up, moe_w_down):
    bsz, seq, _ = x_prompt.shape
    dbsz = x_sample.shape[0]
    depth = ada_w.shape[0]
    rows_p = bsz * seq
    rows_all = rows_p + dbsz
    n_tiles = rows_all // MOE_TILE + N_CLASSES
    caches_k = (cache_b_k0, cache_b_k1, cache_b_k2)
    caches_v = (cache_b_v0, cache_b_v1, cache_b_v2)

    c_rows = -(-(bsz + dbsz) // SUBLANES) * SUBLANES
    c_all = jnp.concatenate([c_prompt, c_sample, jnp.zeros((c_rows - bsz - dbsz, D), F32)], axis=0)
    mods = _ada_mod(c_all, ada_w, ada_b)

    cos_all, sin_all = _rope_table(seq + SUBLANES)
    tm_p = _row_tile(rows_p)
    cos_p = jnp.concatenate([_swap_rotary(cos_all[:seq]), jnp.ones((tm_p, HEAD_DIM), F32)], axis=0)
    sin_p = jnp.concatenate([_swap_rotary(sin_all[:seq]), jnp.zeros((tm_p, HEAD_DIM), F32)], axis=0)
    cos_s = jnp.concatenate([jnp.broadcast_to(cos_all[PAST_LEN:PAST_LEN + 1], (dbsz, HEAD_DIM)),
                             jnp.ones((dbsz, HEAD_DIM), F32)], axis=0)
    sin_s = jnp.concatenate([jnp.broadcast_to(sin_all[PAST_LEN:PAST_LEN + 1], (dbsz, HEAD_DIM)),
                             jnp.zeros((dbsz, HEAD_DIM), F32)], axis=0)

    bf = lambda w: w.astype(BF16)
    wg_bf, wu_bf, wd_bf = bf(moe_w_gate), bf(moe_w_up), bf(moe_w_down)
    a_w_in_y, a_w_in_x, a_w_out = bf(a_w_in_y), bf(a_w_in_x), bf(a_w_out)
    a_gate_r_w, a_gate_i_w = bf(a_gate_r_w), bf(a_gate_i_w)
    b_w_qkv, b_w_o = bf(b_w_qkv), bf(b_w_o)
    conf_w_pw1, conf_w_pw2 = bf(conf_w_pw1), bf(conf_w_pw2)

    xp = x_prompt.reshape(rows_p, D)
    xs = x_sample.reshape(dbsz, D)
    row1 = lambda v: v.reshape(1, -1)
    zero_bias = jnp.zeros((1, D), F32)

    a_conv_p, a_conv_s, a_h_p, a_h_s = [], [], [], []
    kp, vp, ksm, vsm = ([[] for _ in range(N_GROUPS)] for _ in range(4))
    cf_p, cf_s = [], []

    pend_p = pend_s = None
    for i in range(depth):
        kind, j = i % 3, i // 3
        mp = [mods[i, :bsz, k * D:(k + 1) * D].reshape(bsz, 1, D) for k in range(6)]
        ms = [mods[i, bsz:bsz + dbsz, k * D:(k + 1) * D] for k in range(6)]
        g_mix = row1(norm_mix_g[i])
        if kind == 0:
            wts = (a_conv_w[j], row1(a_conv_b[j]), a_gate_r_w[j], row1(a_gate_r_b[j]),
                   a_gate_i_w[j], row1(a_gate_i_b[j]), row1(a_lambda[j]), a_w_out[j])
            ybr, xin, xp = _proj_a(xp, g_mix, mp[0], mp[1], a_w_in_y[j], a_w_in_x[j], False, seq, pend_p)
            xp3, h_last = _rglru_prompt(xin.reshape(bsz, seq, D_RNN), ybr.reshape(bsz, seq, D_RNN),
                                        xp.reshape(bsz, seq, D), mp[2], *wts)
            xp = xp3.reshape(rows_p, D)
            a_conv_p.append(xin.reshape(bsz, seq, D_RNN)[:, seq - (LRU_CONV - 1):])
            a_h_p.append(h_last.reshape(bsz, D_RNN))
            ybr_s, xin_s, xs = _proj_a(xs, g_mix, ms[0], ms[1], a_w_in_y[j], a_w_in_x[j], True, 1, pend_s)
            hist = state_a_conv[j]
            xs, h_new = _rglru_step(xin_s, ybr_s, xs, ms[2], hist.reshape(dbsz, (LRU_CONV - 1) * D_RNN),
                                    state_a_h[j], *wts)
            a_conv_s.append(jnp.concatenate([hist[:, 1:], xin_s[:, None, :]], axis=1))
            a_h_s.append(h_new)
        elif kind == 1:
            w_sw = b_w_qkv[j]
            w_qk = _swap_rotary(w_sw[:, :2 * N_GROUPS * D].reshape(D, 2 * N_GROUPS * N_HEADS, HEAD_DIM))
            w_sw = jnp.concatenate([w_qk.reshape(D, 2 * N_GROUPS * D), w_sw[:, 2 * N_GROUPS * D:]], axis=1)
            qkv, xp = _proj_qkv(xp, g_mix, mp[0], mp[1], cos_p, sin_p, w_sw, False, seq, pend_p, True)
            qkv3 = qkv.reshape(bsz, seq, QKV_WIDTH)
            for g, (win, dil) in enumerate(ATTN_GROUPS):
                keep = min(win, seq)
                kcol = (N_GROUPS + g) * D
                vcol = (2 * N_GROUPS + g) * D
                k_tail = qkv3[:, seq - keep:, kcol:kcol + D].reshape(bsz, keep, N_HEADS, HEAD_DIM)
                kp[g].append(_swap_rotary(k_tail, inverse=True))
                vp[g].append(qkv3[:, seq - keep:, vcol:vcol + D].reshape(bsz, keep, N_HEADS, HEAD_DIM))
            o_p = _attn_prompt(qkv3).reshape(rows_p, D)
            xp = _out_proj(o_p, b_w_o[j], zero_bias, xp, mp[2], False, seq)
            qkv_s, xs = _proj_qkv(xs, g_mix, ms[0], ms[1], cos_s, sin_s, b_w_qkv[j], True, 1, pend_s, False)
            o_s = _attn_step(qkv_s, [c[j] for c in caches_k], [c[j] for c in caches_v])
            xs = _out_proj(o_s, b_w_o[j], zero_bias, xs, ms[2], True, 1)
            for g in range(N_GROUPS):
                kcol = (N_GROUPS + g) * D
                vcol = (2 * N_GROUPS + g) * D
                ksm[g].append(qkv_s[:, kcol:kcol + D].reshape(dbsz, 1, N_HEADS, HEAD_DIM))
                vsm[g].append(qkv_s[:, vcol:vcol + D].reshape(dbsz, 1, N_HEADS, HEAD_DIM))
        else:
            wts = (conf_w_dw[j], row1(conf_b_dw[j]), row1(conf_ln_g[j]), row1(conf_ln_b[j]),
                   conf_w_pw2[j], row1(conf_b_pw2[j]))
            b1 = row1(conf_b_pw1[j])
            glu, xp = _proj_glu(xp, g_mix, mp[0], mp[1], conf_w_pw1[j], b1, False, seq, pend_p)
            glu3 = glu.reshape(bsz, seq, D)
            xp = _conf_prompt(glu3, xp.reshape(bsz, seq, D), mp[2], *wts).reshape(rows_p, D)
            cf_p.append(glu3[:, seq - (CONF_WIDTH - 1):])
            glu_s, xs = _proj_glu(xs, g_mix, ms[0], ms[1], conf_w_pw1[j], b1, True, 1, pend_s)
            hist = state_c_conv[j]
            xs = _conf_step(glu_s, hist.reshape(dbsz, (CONF_WIDTH - 1) * D), xs, ms[2], *wts)
            cf_s.append(jnp.concatenate([hist[:, 1:], glu_s[:, None, :]], axis=1))

        g_ffn = row1(norm_ffn_g[i])
        wr = jnp.concatenate([moe_w_grouter[i], moe_w_erouter[i],
                              jnp.zeros((D, LANES - N_EGROUPS - N_EXPERTS), F32)], axis=1)
        wr_hi = wr.astype(BF16)
        wr = jnp.stack([wr_hi, (wr - wr_hi.astype(F32)).astype(BF16)])
        br = jnp.concatenate([moe_b_grouter[i], moe_b_erouter[i],
                              jnp.zeros((LANES - N_EGROUPS - N_EXPERTS,), F32)]).reshape(1, LANES)
        joint = _moe_route(xp, g_ffn, mp[3], mp[4], wr, br, False, seq, rows_all, 0, None)
        ui_all, counts = _moe_route(xs, g_ffn, ms[3], ms[4], wr, br, True, 1, rows_all, rows_p // dbsz, joint)
        dest, src, tile_ea, tile_eb, tile_valid = _moe_plan(ui_all, counts, n_tiles)
        ui_sorted = jnp.take(ui_all, src, axis=0, mode="clip")
        y_sorted = _moe_experts(i, tile_ea, tile_eb, tile_valid, ui_sorted, wg_bf, wu_bf, wd_bf)
        y_all = jnp.take(y_sorted, dest, axis=0, mode="clip")
        if i < depth - 1:
            pend_p = (y_all, mp[5], 0)
            pend_s = (y_all, ms[5], rows_p // dbsz)
        else:
            fg = row1(final_norm_g)
            xp = _residual(xp, y_all, mp[5], fg, False, seq, 0, True)
            xs = _residual(xs, y_all, ms[5], fg, True, 1, rows_p // dbsz, True)

    y_prompt = xp.reshape(bsz, seq, D)
    y_sample = xs.reshape(dbsz, 1, D)
    return (y_prompt, y_sample,
            jnp.stack(a_conv_p), jnp.stack(a_conv_s), jnp.stack(a_h_p), jnp.stack(a_h_s),
            jnp.stack(kp[0]), jnp.stack(ksm[0]), jnp.stack(vp[0]), jnp.stack(vsm[0]),
            jnp.stack(kp[1]), jnp.stack(ksm[1]), jnp.stack(vp[1]), jnp.stack(vsm[1]),
            jnp.stack(kp[2]), jnp.stack(ksm[2]), jnp.stack(vp[2]), jnp.stack(vsm[2]),
            jnp.stack(cf_p), jnp.stack(cf_s))
```

```python
import functools
import math

import jax
import jax.numpy as jnp
from jax import lax
from jax.experimental import pallas as pl
from jax.experimental.pallas import tpu as pltpu

F32 = jnp.float32
BF16 = jnp.bfloat16
HIGHEST = lax.Precision.HIGHEST

D = 1024
D_RNN = 1280
LRU_BLOCKS = 10
LRU_BLOCK = 128
LRU_CONV = 4
LRU_C = 8.0
ATTN_GROUPS = ((128, 1), (512, 4), (2048, 16))
N_GROUPS = 3
N_HEADS = 8
HEAD_DIM = 128
ROT_DIM = 32
ROPE_THETA = 500000.0
QKV_WIDTH = 9 * D
CONF_WIDTH = 31
N_EGROUPS = 4
EXPERTS_PER_GROUP = 4
N_EXPERTS = 16
D_EXPERT = 512
N_PAIRS = 6
N_CLASSES = N_EGROUPS * N_PAIRS
NORM_EPS = 1e-6
PAST_LEN = 8192

LANES = 128
SUBLANES = 8
ATT_BLOCK = 128
ATT_TOKENS = 2048
QKV_CHUNK = 256
ATT_BATCH = 16
MOE_TILE = 256
VMEM_LIMIT = 56 * 1024 * 1024


def _cparams(sem):
    return pltpu.CompilerParams(dimension_semantics=sem, vmem_limit_bytes=VMEM_LIMIT)


def _sigmoid(x):
    return 1.0 / (1.0 + jnp.exp(-x))


def _silu(x):
    return x * _sigmoid(x)


def _gelu_tanh(x):
    return 0.5 * x * (1.0 + jnp.tanh(math.sqrt(2.0 / math.pi) * (x + 0.044715 * (x * x * x))))


def _softplus(x):
    return jnp.maximum(x, 0.0) + jnp.log1p(jnp.exp(-jnp.abs(x)))


def _bdot(a, b):
    return jnp.dot(a.astype(BF16), b.astype(BF16), preferred_element_type=F32)


def _norm_mod(x, g, shift, scale):
    ms = jnp.mean(x * x, axis=-1, keepdims=True)
    y = x * lax.rsqrt(ms + NORM_EPS) * g
    return y * (1.0 + scale) + shift


def _ada_kernel(c_ref, w_ref, b_ref, o_ref):
    c = c_ref[...]
    o_ref[...] = jnp.dot(_silu(c), w_ref[...], preferred_element_type=F32, precision=HIGHEST) + b_ref[...]


def _ada_mod(c_all, ada_w, ada_b):
    depth, _, n6 = ada_w.shape
    rows = c_all.shape[0]
    tn = 3072
    return pl.pallas_call(
        _ada_kernel,
        out_shape=jax.ShapeDtypeStruct((depth, rows, n6), F32),
        grid=(depth, n6 // tn),
        in_specs=[pl.BlockSpec((rows, D), lambda l, j: (0, 0)),
                  pl.BlockSpec((None, D, tn), lambda l, j: (l, 0, j)),
                  pl.BlockSpec((None, 1, tn), lambda l, j: (l, 0, j))],
        out_specs=pl.BlockSpec((None, rows, tn), lambda l, j: (l, 0, j)),
        compiler_params=_cparams(("arbitrary", "arbitrary")),
        name="ada_mod",
    )(c_all, ada_w, ada_b.reshape(depth, 1, n6))


def _mod_spec(tm, per_row, tiles_per_seq, ngrid):
    if per_row:
        if ngrid == 1:
            return pl.BlockSpec((tm, D), lambda i: (i, 0))
        return pl.BlockSpec((tm, D), lambda i, j: (i, 0))
    if ngrid == 1:
        return pl.BlockSpec((None, 1, D), lambda i: (i // tiles_per_seq, 0, 0))
    return pl.BlockSpec((None, 1, D), lambda i, j: (i // tiles_per_seq, 0, 0))


def _row_tile(rows):
    return 1024 if rows % 1024 == 0 else rows


def _load_x(x_ref, pend_refs, xn_ref):
    if pend_refs is None:
        return x_ref[...]
    y_ref, gate_ref = pend_refs
    x = x_ref[...] + gate_ref[...] * y_ref[...]
    xn_ref[...] = x
    return x


def _pend_specs(pending, tm, per_row, tps, ngrid):
    if pending is None:
        return [], [], [], []
    y_all, gate, row_block0 = pending
    if ngrid == 1:
        yspec = pl.BlockSpec((tm, D), lambda i: (i + row_block0, 0))
        ospec = pl.BlockSpec((tm, D), lambda i: (i, 0))
    else:
        yspec = pl.BlockSpec((tm, D), lambda i, j: (i + row_block0, 0))
        ospec = pl.BlockSpec((tm, D), lambda i, j: (i, 0))
    return [yspec, _mod_spec(tm, per_row, tps, ngrid)], [y_all, gate], [ospec], [D]


def _proj_a_kernel(*refs, fuse):
    x_ref, refs = refs[0], refs[1:]
    pend, refs = (refs[:2], refs[2:]) if fuse else (None, refs)
    g_ref, sh_ref, sc_ref, wy_ref, wx_ref, y_ref, xi_ref = refs[:7]
    x = _load_x(x_ref, pend, refs[7] if fuse else None)
    u = _norm_mod(x, g_ref[...], sh_ref[...], sc_ref[...]).astype(BF16)
    y_ref[...] = _gelu_tanh(jnp.dot(u, wy_ref[...], preferred_element_type=F32))
    xi_ref[...] = jnp.dot(u, wx_ref[...], preferred_element_type=F32)


def _proj_a(x, g, shift, scale, w_y, w_x, per_row, seq_len, pending):
    rows = x.shape[0]
    tm = 512 if rows % 512 == 0 else rows
    tps = max(seq_len // tm, 1)
    mod = _mod_spec(tm, per_row, tps, 1)
    wspec = pl.BlockSpec((D, D_RNN), lambda i: (0, 0))
    ospec = pl.BlockSpec((tm, D_RNN), lambda i: (i, 0))
    pspecs, pargs, pout, pwidth = _pend_specs(pending, tm, per_row, tps, 1)
    outs = pl.pallas_call(
        functools.partial(_proj_a_kernel, fuse=pending is not None),
        out_shape=tuple(jax.ShapeDtypeStruct((rows, n), F32) for n in [D_RNN, D_RNN] + pwidth),
        grid=(rows // tm,),
        in_specs=[pl.BlockSpec((tm, D), lambda i: (i, 0))] + pspecs + [
            pl.BlockSpec((1, D), lambda i: (0, 0)), mod, mod, wspec, wspec],
        out_specs=tuple([ospec, ospec] + pout),
        compiler_params=_cparams(("arbitrary",)),
        name="proj_a",
    )(x, *pargs, g, shift, scale, w_y, w_x)
    return outs[0], outs[1], (outs[2] if pending is not None else x)


def _scan_rows(a, b, h0):
    n, lanes = a.shape
    groups = n // SUBLANES
    a3 = a.reshape(groups, SUBLANES, lanes)
    b3 = b.reshape(groups, SUBLANES, lanes)
    sub = lax.broadcasted_iota(jnp.int32, a3.shape, 1)
    d = 1
    while d < SUBLANES:
        keep = sub >= d
        a_sh = jnp.where(keep, pltpu.roll(a3, d, 1), 1.0)
        b_sh = jnp.where(keep, pltpu.roll(b3, d, 1), 0.0)
        b3 = a3 * b_sh + b3
        a3 = a3 * a_sh
        d *= 2
    hs = []
    h_prev = h0
    for j in range(groups):
        hj = a3[j] * h_prev + b3[j]
        hs.append(hj)
        h_prev = hj[SUBLANES - 1:SUBLANES, :]
    return jnp.concatenate(hs, axis=0), h_prev


def _lru_gates(xc, grw, grb, giw, gib, lam):
    xcb = xc.astype(BF16)
    r = _sigmoid(jnp.dot(xcb, grw, preferred_element_type=F32) + grb)
    i = _sigmoid(jnp.dot(xcb, giw, preferred_element_type=F32) + gib)
    log_a = -LRU_C * r * _softplus(-lam)
    a = jnp.exp(log_a)
    b = jnp.sqrt(jnp.tanh(-log_a) * (a * a + 1.0)) * (i * xc)
    return a, b


def _rglru_kernel(xin_ref, ybr_ref, x_ref, gate_ref, cw_ref, cb_ref, grw_ref, grb_ref, giw_ref, gib_ref,
                  lam_ref, wo_ref, xo_ref, hl_ref, xx_ref, hc_ref, hy_ref, *, tt):
    t = pl.program_id(1)

    @pl.when(t == 0)
    def _():
        xx_ref[0:SUBLANES, :] = jnp.zeros((SUBLANES, D_RNN), F32)
        hc_ref[...] = jnp.zeros((1, D_RNN), F32)

    @pl.when(t > 0)
    def _():
        xx_ref[0:SUBLANES, :] = xx_ref[tt:tt + SUBLANES, :]

    xx_ref[SUBLANES:tt + SUBLANES, :] = xin_ref[...]
    for n in range(LRU_BLOCKS):
        sl = slice(n * LRU_BLOCK, (n + 1) * LRU_BLOCK)
        xc = cb_ref[:, sl]
        for k in range(LRU_CONV):
            off = SUBLANES - (LRU_CONV - 1) + k
            xc = xc + cw_ref[k:k + 1, sl] * xx_ref[off:off + tt, sl]
        a, bt = _lru_gates(xc, grw_ref[n], grb_ref[:, sl], giw_ref[n], gib_ref[:, sl], lam_ref[:, sl])
        h, h_last = _scan_rows(a, bt, hc_ref[:, sl])
        hc_ref[:, sl] = h_last
        hy_ref[:, sl] = (h * ybr_ref[:, sl]).astype(BF16)
    out = jnp.dot(hy_ref[...], wo_ref[...], preferred_element_type=F32)
    xo_ref[...] = x_ref[...] + gate_ref[...] * out
    hl_ref[...] = hc_ref[...]


def _rglru_prompt(xin, ybr, x, gate, cw, cb, grw, grb, giw, gib, lam, wo):
    bsz, seq, _ = xin.shape
    tt = 512
    full2 = lambda b, t: (0, 0)
    full3 = lambda b, t: (0, 0, 0)
    return pl.pallas_call(
        functools.partial(_rglru_kernel, tt=tt),
        out_shape=(jax.ShapeDtypeStruct((bsz, seq, D), F32), jax.ShapeDtypeStruct((bsz, 1, D_RNN), F32)),
        grid=(bsz, seq // tt),
        in_specs=[pl.BlockSpec((None, tt, D_RNN), lambda b, t: (b, t, 0)),
                  pl.BlockSpec((None, tt, D_RNN), lambda b, t: (b, t, 0)),
                  pl.BlockSpec((None, tt, D), lambda b, t: (b, t, 0)),
                  pl.BlockSpec((None, 1, D), lambda b, t: (b, 0, 0)),
                  pl.BlockSpec((LRU_CONV, D_RNN), full2),
                  pl.BlockSpec((1, D_RNN), full2),
                  pl.BlockSpec((LRU_BLOCKS, LRU_BLOCK, LRU_BLOCK), full3),
                  pl.BlockSpec((1, D_RNN), full2),
                  pl.BlockSpec((LRU_BLOCKS, LRU_BLOCK, LRU_BLOCK), full3),
                  pl.BlockSpec((1, D_RNN), full2),
                  pl.BlockSpec((1, D_RNN), full2),
                  pl.BlockSpec((D_RNN, D), full2)],
        out_specs=(pl.BlockSpec((None, tt, D), lambda b, t: (b, t, 0)),
                   pl.BlockSpec((None, 1, D_RNN), lambda b, t: (b, 0, 0))),
        scratch_shapes=[pltpu.VMEM((tt + SUBLANES, D_RNN), F32),
                        pltpu.VMEM((1, D_RNN), F32),
                        pltpu.VMEM((tt, D_RNN), BF16)],
        compiler_params=_cparams(("arbitrary", "arbitrary")),
        name="rglru_prompt",
    )(xin, ybr, x, gate, cw, cb, grw, grb, giw, gib, lam, wo)


def _rglru_step_kernel(xin_ref, ybr_ref, x_ref, gate_ref, hist_ref, hprev_ref, cw_ref, cb_ref, grw_ref, grb_ref,
                       giw_ref, gib_ref, lam_ref, wo_ref, xo_ref, hn_ref, hy_ref):
    for n in range(LRU_BLOCKS):
        sl = slice(n * LRU_BLOCK, (n + 1) * LRU_BLOCK)
        xc = cb_ref[:, sl] + cw_ref[LRU_CONV - 1:LRU_CONV, sl] * xin_ref[:, sl]
        for k in range(LRU_CONV - 1):
            xc = xc + cw_ref[k:k + 1, sl] * hist_ref[:, k * D_RNN + n * LRU_BLOCK:k * D_RNN + (n + 1) * LRU_BLOCK]
        a, bt = _lru_gates(xc, grw_ref[n], grb_ref[:, sl], giw_ref[n], gib_ref[:, sl], lam_ref[:, sl])
        h = a * hprev_ref[:, sl] + bt
        hn_ref[:, sl] = h
        hy_ref[:, sl] = (h * ybr_ref[:, sl]).astype(BF16)
    out = jnp.dot(hy_ref[...], wo_ref[...], preferred_element_type=F32)
    xo_ref[...] = x_ref[...] + gate_ref[...] * out


def _rglru_step(xin, ybr, x, gate, hist, hprev, cw, cb, grw, grb, giw, gib, lam, wo):
    rows = x.shape[0]
    return pl.pallas_call(
        _rglru_step_kernel,
        out_shape=(jax.ShapeDtypeStruct((rows, D), F32), jax.ShapeDtypeStruct((rows, D_RNN), F32)),
        scratch_shapes=[pltpu.VMEM((rows, D_RNN), BF16)],
        compiler_params=pltpu.CompilerParams(vmem_limit_bytes=VMEM_LIMIT),
        name="rglru_step",
    )(xin, ybr, x, gate, hist, hprev, cw, cb, grw, grb, giw, gib, lam, wo)


def _rope_table_kernel(cos_ref, sin_ref):
    half = ROT_DIM // 2
    shape = cos_ref.shape
    pos = lax.broadcasted_iota(jnp.int32, shape, 0).astype(F32)
    lane = lax.broadcasted_iota(jnp.int32, shape, 1)
    fidx = jnp.where(lane < half, lane, lane - half).astype(F32)
    inv_freq = jnp.exp(fidx * (-2.0 * math.log(ROPE_THETA) / ROT_DIM))
    ang = pos * inv_freq
    rot = lane < ROT_DIM
    cos_ref[...] = jnp.where(rot, jnp.cos(ang), 1.0)
    sin_ref[...] = jnp.where(rot, jnp.where(lane < half, -jnp.sin(ang), jnp.sin(ang)), 0.0)


def _rope_table(n_pos):
    return pl.pallas_call(
        _rope_table_kernel,
        out_shape=(jax.ShapeDtypeStruct((n_pos, HEAD_DIM), F32), jax.ShapeDtypeStruct((n_pos, HEAD_DIM), F32)),
        compiler_params=pltpu.CompilerParams(vmem_limit_bytes=VMEM_LIMIT),
        name="rope_table",
    )()


def _proj_qkv_kernel(*refs, tn, fuse, swapped):
    x_ref, refs = refs[0], refs[1:]
    pend, refs = (refs[:2], refs[2:]) if fuse else (None, refs)
    g_ref, sh_ref, sc_ref, cos_ref, sin_ref, w_ref, o_ref = refs[:7]
    u_ref = refs[-1]
    j = pl.program_id(1)

    @pl.when(j == 0)
    def _():
        x = _load_x(x_ref, pend, refs[7] if fuse else None)
        u_ref[...] = _norm_mod(x, g_ref[...], sh_ref[...], sc_ref[...]).astype(BF16)

    half = ROT_DIM // 2
    qscale = jnp.where(j < N_GROUPS * D // tn, HEAD_DIM ** -0.5, 1.0).astype(F32)
    cos = cos_ref[...] * qscale
    sin = sin_ref[...] * qscale
    lane = lax.broadcasted_iota(jnp.int32, cos.shape, 1)
    for c in range(tn // QKV_CHUNK):
        acc = jnp.dot(u_ref[...], w_ref[:, c * QKV_CHUNK:(c + 1) * QKV_CHUNK], preferred_element_type=F32)
        for h in range(QKV_CHUNK // HEAD_DIM):
            xh = acc[:, h * HEAD_DIM:(h + 1) * HEAD_DIM]
            if swapped:
                partner = pltpu.roll(xh, HEAD_DIM // 2, 1)
            else:
                partner = jnp.where(lane < half, pltpu.roll(xh, HEAD_DIM - half, 1), pltpu.roll(xh, half, 1))
            col = c * QKV_CHUNK + h * HEAD_DIM
            o_ref[:, col:col + HEAD_DIM] = xh * cos + partner * sin


def _swap_rotary(a, inverse=False):
    half = ROT_DIM // 2
    mid = HEAD_DIM // 2
    if inverse:
        parts = [a[..., :half], a[..., mid:mid + half], a[..., half:mid], a[..., mid + half:]]
    else:
        parts = [a[..., :half], a[..., ROT_DIM:mid + half], a[..., half:ROT_DIM], a[..., mid + half:]]
    return jnp.concatenate(parts, axis=-1)


def _proj_qkv(x, g, shift, scale, cos_t, sin_t, w, per_row, seq_len, pending, swapped):
    rows = x.shape[0]
    tm = _row_tile(rows)
    tn = N_GROUPS * D // 2
    tps = max(seq_len // tm, 1)
    mod = _mod_spec(tm, per_row, tps, 2)
    n_qk_tiles = 2 * N_GROUPS * D // tn
    n_rope = cos_t.shape[0] // tm - 1
    rope = pl.BlockSpec((tm, HEAD_DIM), lambda i, j: (jnp.where(j < n_qk_tiles, i % n_rope, n_rope), 0))
    pspecs, pargs, pout, pwidth = _pend_specs(pending, tm, per_row, tps, 2)
    outs = pl.pallas_call(
        functools.partial(_proj_qkv_kernel, tn=tn, fuse=pending is not None, swapped=swapped),
        out_shape=tuple(jax.ShapeDtypeStruct((rows, n), F32) for n in [QKV_WIDTH] + pwidth),
        grid=(rows // tm, QKV_WIDTH // tn),
        in_specs=[pl.BlockSpec((tm, D), lambda i, j: (i, 0))] + pspecs + [
            pl.BlockSpec((1, D), lambda i, j: (0, 0)),
            mod, mod, rope, rope,
            pl.BlockSpec((D, tn), lambda i, j: (0, j))],
        out_specs=tuple([pl.BlockSpec((tm, tn), lambda i, j: (i, j))] + pout),
        scratch_shapes=[pltpu.VMEM((tm, D), BF16)],
        compiler_params=_cparams(("arbitrary", "arbitrary")),
        name="proj_qkv",
    )(x, *pargs, g, shift, scale, cos_t, sin_t, w)
    return outs[0], (outs[1] if pending is not None else x)


def _rows(start, dil):
    if dil == 1:
        return pl.ds(start, ATT_BLOCK)
    return pl.ds(start, ATT_BLOCK, stride=dil)


def _attn_prompt_kernel(*refs):
    in_refs, o_ref, og_ref, lg_ref = refs[:15], refs[15], refs[16], refs[17]
    n = ATT_BLOCK
    nblk = ATT_BATCH
    ri = lax.broadcasted_iota(jnp.int32, (nblk, n, n), 1)
    cj = lax.broadcasted_iota(jnp.int32, (nblk, n, n), 2)
    neg = -jnp.inf
    has_prev = pl.program_id(1) > 0
    qk_dims = (((2,), (2,)), ((0,), (0,)))
    pv_dims = (((2,), (1,)), ((0,), (0,)))
    ones = jnp.ones((nblk, n, HEAD_DIM), BF16)
    for g, (win, dil) in enumerate(ATTN_GROUPS):
        q_ref, kc_ref, vc_ref, kp_ref, vp_ref = in_refs[5 * g:5 * g + 5]
        all_blocks = [(s, r) for s in range(ATT_TOKENS // win) for r in range(dil)]
        for b0 in range(0, len(all_blocks), nblk):
            blocks = all_blocks[b0:b0 + nblk]

            def load(ref, s, r):
                return ref[_rows(s * win + r, dil), :].astype(BF16)

            q = jnp.stack([load(q_ref, s, r) for s, r in blocks])
            kc = jnp.stack([load(kc_ref, s, r) for s, r in blocks])
            vc = jnp.stack([load(vc_ref, s, r) for s, r in blocks])
            kp = jnp.stack([load(kc_ref, s - 1, r) if s > 0 else kp_ref[_rows(r, dil), :].astype(BF16)
                            for s, r in blocks])
            vp = jnp.stack([load(vc_ref, s - 1, r) if s > 0 else vp_ref[_rows(r, dil), :].astype(BF16)
                            for s, r in blocks])
            sp = lax.dot_general(q, kp, qk_dims, preferred_element_type=F32)
            sc = lax.dot_general(q, kc, qk_dims, preferred_element_type=F32)
            if blocks[0][0] == 0:
                n_first = sum(1 for s, _ in blocks if s == 0)
                blk = lax.broadcasted_iota(jnp.int32, (nblk, n, n), 0)
                sp = jnp.where(blk >= jnp.where(has_prev, 0, n_first), sp, neg)
            sp = jnp.where(cj >= ri, sp, neg)
            sc = jnp.where(cj <= ri, sc, neg)
            m = jnp.max(jnp.maximum(sp, sc), axis=-1, keepdims=True)
            ep = jnp.exp(sp - m).astype(BF16)
            ec = jnp.exp(sc - m).astype(BF16)
            l = (lax.dot_general(ep, ones, pv_dims, preferred_element_type=F32)
                 + lax.dot_general(ec, ones, pv_dims, preferred_element_type=F32))
            o = (lax.dot_general(ep, vp, pv_dims, preferred_element_type=F32)
                 + lax.dot_general(ec, vc, pv_dims, preferred_element_type=F32)) / l
            lse = m + jnp.log(l)
            for idx, (s, r) in enumerate(blocks):
                rows = _rows(s * win + r, dil)
                og_ref[g, rows, :] = o[idx]
                lg_ref[g, rows, :] = lse[idx]
    l0, l1, l2 = lg_ref[0], lg_ref[1], lg_ref[2]
    m = jnp.maximum(jnp.maximum(l0, l1), l2)
    e0, e1, e2 = jnp.exp(l0 - m), jnp.exp(l1 - m), jnp.exp(l2 - m)
    o = (e0 * og_ref[0] + e1 * og_ref[1] + e2 * og_ref[2]) / (e0 + e1 + e2)
    o_ref[...] = o.astype(o_ref.dtype)


def _attn_prompt(qkv):
    bsz, seq, _ = qkv.shape
    tb = ATT_TOKENS
    specs = []
    for g, (win, dil) in enumerate(ATTN_GROUPS):
        per = tb // win

        def cur(which, g=g):
            return pl.BlockSpec((None, tb, HEAD_DIM),
                                lambda b, i, h: (b, i, (which * N_GROUPS + g) * N_HEADS + h))

        def prev(which, g=g, win=win, per=per):
            return pl.BlockSpec((None, win, HEAD_DIM),
                                lambda b, i, h: (b, jnp.maximum(i * per - 1, 0), (which * N_GROUPS + g) * N_HEADS + h))

        specs += [cur(0), cur(1), cur(2), prev(1), prev(2)]
    return pl.pallas_call(
        _attn_prompt_kernel,
        out_shape=jax.ShapeDtypeStruct((bsz, seq, D), BF16),
        grid=(bsz, seq // tb, N_HEADS),
        in_specs=specs,
        out_specs=pl.BlockSpec((None, tb, HEAD_DIM), lambda b, i, h: (b, i, h)),
        scratch_shapes=[pltpu.VMEM((N_GROUPS, tb, HEAD_DIM), F32),
                        pltpu.VMEM((N_GROUPS, tb, HEAD_DIM), F32)],
        compiler_params=_cparams(("arbitrary", "arbitrary", "arbitrary")),
        name="attn_prompt",
    )(*([qkv] * 15))


def _attn_step_kernel(*refs):
    q_refs, kn_refs, vn_refs = refs[0:3], refs[3:6], refs[6:9]
    kc_refs, vc_refs = refs[9:12], refs[12:15]
    o_ref = refs[15]
    outs, lses = [], []
    for g in range(N_GROUPS):
        q = q_refs[g][...]
        kn = kn_refs[g][...]
        vn = vn_refs[g][...]
        s_c = jnp.sum(kc_refs[g][...] * q[None], axis=-1, keepdims=True)
        s_n = jnp.sum(kn * q, axis=-1, keepdims=True)
        m = jnp.maximum(jnp.max(s_c, axis=0), s_n)
        e_c = jnp.exp(s_c - m[None])
        e_n = jnp.exp(s_n - m)
        l = jnp.sum(e_c, axis=0) + e_n
        outs.append((jnp.sum(e_c * vc_refs[g][...], axis=0) + e_n * vn) / l)
        lses.append(m + jnp.log(l))
    m = jnp.maximum(jnp.maximum(lses[0], lses[1]), lses[2])
    es = [jnp.exp(x - m) for x in lses]
    inv = 1.0 / (es[0] + es[1] + es[2])
    o_ref[...] = (es[0] * outs[0] + es[1] * outs[1] + es[2] * outs[2]) * inv


def _attn_step(qkv_s, caches_k, caches_v):
    bsz = qkv_s.shape[0]
    q4 = qkv_s.reshape(bsz, 9, N_HEADS, HEAD_DIM)

    def new_spec(which, g):
        return pl.BlockSpec((None, None, N_HEADS, HEAD_DIM), lambda b: (b, which * N_GROUPS + g, 0, 0))

    specs = [new_spec(w, g) for w in range(3) for g in range(N_GROUPS)]
    cache_spec = pl.BlockSpec((None, ATT_BLOCK, N_HEADS, HEAD_DIM), lambda b: (b, 0, 0, 0))
    ks, vs = [], []
    for g, (win, dil) in enumerate(ATTN_GROUPS):
        ks.append(caches_k[g].reshape(bsz, win // dil, dil * N_HEADS, HEAD_DIM))
        vs.append(caches_v[g].reshape(bsz, win // dil, dil * N_HEADS, HEAD_DIM))
    o = pl.pallas_call(
        _attn_step_kernel,
        out_shape=jax.ShapeDtypeStruct((bsz, N_HEADS, HEAD_DIM), F32),
        grid=(bsz,),
        in_specs=specs + [cache_spec] * 6,
        out_specs=pl.BlockSpec((None, N_HEADS, HEAD_DIM), lambda b: (b, 0, 0)),
        compiler_params=_cparams(("arbitrary",)),
        name="attn_step",
    )(*([q4] * 9), *ks, *vs)
    return o.reshape(bsz, D)


def _out_proj_kernel(a_ref, w_ref, b_ref, x_ref, gate_ref, xo_ref):
    out = jnp.dot(a_ref[...].astype(BF16), w_ref[...], preferred_element_type=F32) + b_ref[...]
    xo_ref[...] = x_ref[...] + gate_ref[...] * out


def _out_proj(a, w, bias, x, gate, per_row, seq_len):
    rows, k = a.shape
    tm = _row_tile(rows)
    tps = max(seq_len // tm, 1)
    return pl.pallas_call(
        _out_proj_kernel,
        out_shape=jax.ShapeDtypeStruct((rows, D), F32),
        grid=(rows // tm,),
        in_specs=[pl.BlockSpec((tm, k), lambda i: (i, 0)),
                  pl.BlockSpec((k, D), lambda i: (0, 0)),
                  pl.BlockSpec((1, D), lambda i: (0, 0)),
                  pl.BlockSpec((tm, D), lambda i: (i, 0)),
                  _mod_spec(tm, per_row, tps, 1)],
        out_specs=pl.BlockSpec((tm, D), lambda i: (i, 0)),
        compiler_params=_cparams(("arbitrary",)),
        name="out_proj",
    )(a, w, bias, x, gate)


def _proj_glu_kernel(*refs, fuse):
    x_ref, refs = refs[0], refs[1:]
    pend, refs = (refs[:2], refs[2:]) if fuse else (None, refs)
    g_ref, sh_ref, sc_ref, wa_ref, wb_ref, ba_ref, bb_ref, o_ref = refs[:8]
    x = _load_x(x_ref, pend, refs[8] if fuse else None)
    u = _norm_mod(x, g_ref[...], sh_ref[...], sc_ref[...]).astype(BF16)
    za = jnp.dot(u, wa_ref[...], preferred_element_type=F32) + ba_ref[...]
    zb = jnp.dot(u, wb_ref[...], preferred_element_type=F32) + bb_ref[...]
    o_ref[...] = za * _sigmoid(zb)


def _proj_glu(x, g, shift, scale, w, bias, per_row, seq_len, pending):
    rows = x.shape[0]
    tm = _row_tile(rows)
    tps = max(seq_len // tm, 1)
    mod = _mod_spec(tm, per_row, tps, 1)
    pspecs, pargs, pout, pwidth = _pend_specs(pending, tm, per_row, tps, 1)
    outs = pl.pallas_call(
        functools.partial(_proj_glu_kernel, fuse=pending is not None),
        out_shape=tuple(jax.ShapeDtypeStruct((rows, n), F32) for n in [D] + pwidth),
        grid=(rows // tm,),
        in_specs=[pl.BlockSpec((tm, D), lambda i: (i, 0))] + pspecs + [
            pl.BlockSpec((1, D), lambda i: (0, 0)),
            mod, mod,
            pl.BlockSpec((D, D), lambda i: (0, 0)),
            pl.BlockSpec((D, D), lambda i: (0, 1)),
            pl.BlockSpec((1, D), lambda i: (0, 0)),
            pl.BlockSpec((1, D), lambda i: (0, 1))],
        out_specs=tuple([pl.BlockSpec((tm, D), lambda i: (i, 0))] + pout),
        compiler_params=_cparams(("arbitrary",)),
        name="proj_glu",
    )(x, *pargs, g, shift, scale, w, w, bias, bias)
    return outs[0], (outs[1] if pending is not None else x)


def _layer_norm(y, g, b):
    mu = jnp.mean(y, axis=-1, keepdims=True)
    yc = y - mu
    var = jnp.mean(yc * yc, axis=-1, keepdims=True)
    return yc * lax.rsqrt(var + NORM_EPS) * g + b


CONF_HALO = 32
CONF_CHUNK = 128


def _conf_kernel(glu_ref, x_ref, gate_ref, wdw_ref, bdw_ref, lg_ref, lb_ref, w2_ref, b2_ref, xo_ref,
                 xx_ref, z_ref, p_ref, *, tt):
    t = pl.program_id(1)

    @pl.when(t == 0)
    def _():
        xx_ref[0:CONF_HALO, :] = jnp.zeros((CONF_HALO, D), F32)

    @pl.when(t > 0)
    def _():
        xx_ref[0:CONF_HALO, :] = xx_ref[tt:tt + CONF_HALO, :]

    xx_ref[CONF_HALO:tt + CONF_HALO, :] = glu_ref[...]
    first = CONF_HALO - (CONF_WIDTH - 1)
    for r0 in range(0, tt, CONF_CHUNK):
        for c in range(D // LANES):
            sl = slice(c * LANES, (c + 1) * LANES)
            acc = jnp.broadcast_to(bdw_ref[:, sl], (CONF_CHUNK, LANES))
            for r in range(SUBLANES):
                taps = [k for k in range(CONF_WIDTH) if (first + k) % SUBLANES == r]
                span = CONF_CHUNK if r == 0 else CONF_CHUNK + SUBLANES
                p = None
                for k in taps:
                    base = r0 + first + k - r
                    term = wdw_ref[k:k + 1, sl] * xx_ref[base:base + span, sl]
                    p = term if p is None else p + term
                if r == 0:
                    acc = acc + p
                else:
                    p_ref[r, :, :] = p
                    acc = acc + p_ref[r, r:r + CONF_CHUNK, :]
            z_ref[r0:r0 + CONF_CHUNK, sl] = acc
    y = _silu(_layer_norm(z_ref[...], lg_ref[...], lb_ref[...]))
    out = jnp.dot(y.astype(BF16), w2_ref[...], preferred_element_type=F32) + b2_ref[...]
    xo_ref[...] = x_ref[...] + gate_ref[...] * out


def _conf_prompt(glu, x, gate, wdw, bdw, lg, lb, w2, b2):
    bsz, seq, _ = glu.shape
    tt = 512
    full2 = lambda b, t: (0, 0)
    vec = pl.BlockSpec((1, D), full2)
    return pl.pallas_call(
        functools.partial(_conf_kernel, tt=tt),
        out_shape=jax.ShapeDtypeStruct((bsz, seq, D), F32),
        grid=(bsz, seq // tt),
        in_specs=[pl.BlockSpec((None, tt, D), lambda b, t: (b, t, 0)),
                  pl.BlockSpec((None, tt, D), lambda b, t: (b, t, 0)),
                  pl.BlockSpec((None, 1, D), lambda b, t: (b, 0, 0)),
                  pl.BlockSpec((CONF_WIDTH, D), full2),
                  vec, vec, vec,
                  pl.BlockSpec((D, D), full2),
                  vec],
        out_specs=pl.BlockSpec((None, tt, D), lambda b, t: (b, t, 0)),
        scratch_shapes=[pltpu.VMEM((tt + CONF_HALO, D), F32),
                        pltpu.VMEM((tt, D), F32),
                        pltpu.VMEM((SUBLANES, CONF_CHUNK + SUBLANES, LANES), F32)],
        compiler_params=_cparams(("arbitrary", "arbitrary")),
        name="conf_prompt",
    )(glu, x, gate, wdw, bdw, lg, lb, w2, b2)


def _conf_step_kernel(glu_ref, hist_ref, x_ref, gate_ref, wdw_ref, bdw_ref, lg_ref, lb_ref, w2_ref, b2_ref, xo_ref):
    acc = bdw_ref[...] + wdw_ref[CONF_WIDTH - 1:CONF_WIDTH, :] * glu_ref[...]
    for k in range(CONF_WIDTH - 1):
        acc = acc + wdw_ref[k:k + 1, :] * hist_ref[:, k * D:(k + 1) * D]
    y = _silu(_layer_norm(acc, lg_ref[...], lb_ref[...]))
    out = jnp.dot(y.astype(BF16), w2_ref[...], preferred_element_type=F32) + b2_ref[...]
    xo_ref[...] = x_ref[...] + gate_ref[...] * out


def _conf_step(glu, hist, x, gate, wdw, bdw, lg, lb, w2, b2):
    return pl.pallas_call(
        _conf_step_kernel,
        out_shape=jax.ShapeDtypeStruct(x.shape, F32),
        compiler_params=pltpu.CompilerParams(vmem_limit_bytes=VMEM_LIMIT),
        name="conf_step",
    )(glu, hist, x, gate, wdw, bdw, lg, lb, w2, b2)


def _moe_route_kernel(x_ref, g_ref, sh_ref, sc_ref, wr_ref, br_ref, tri_ref, cnt0_ref, *rest):
    ui_ref, cnt_ref, carry_ref = rest[-3], rest[-2], rest[-1]

    @pl.when(pl.program_id(0) == 0)
    def _():
        carry_ref[...] = cnt0_ref[...]

    u = _norm_mod(x_ref[...], g_ref[...], sh_ref[...], sc_ref[...])
    ui_ref[:, :D] = u
    u_hi = u.astype(BF16)
    u_lo = (u - u_hi.astype(F32)).astype(BF16)
    logits = (jnp.dot(u_hi, wr_ref[0], preferred_element_type=F32)
              + jnp.dot(u_lo, wr_ref[0], preferred_element_type=F32)
              + jnp.dot(u_hi, wr_ref[1], preferred_element_type=F32)) + br_ref[...]
    lane = lax.broadcasted_iota(jnp.int32, logits.shape, 1)
    neg = -jnp.inf
    big = jnp.int32(LANES)
    gl = jnp.where(lane < N_EGROUPS, logits, neg)
    gmax = jnp.max(gl, axis=-1, keepdims=True)
    gsel = jnp.min(jnp.where(gl == gmax, lane, big), axis=-1, keepdims=True)
    gp = 1.0 / jnp.sum(jnp.exp(gl - gmax), axis=-1, keepdims=True)
    base = N_EGROUPS + EXPERTS_PER_GROUP * gsel
    el = jnp.where((lane >= base) & (lane < base + EXPERTS_PER_GROUP), logits, neg)
    v1 = jnp.max(el, axis=-1, keepdims=True)
    i1 = jnp.min(jnp.where(el == v1, lane, big), axis=-1, keepdims=True)
    el2 = jnp.where(lane == i1, neg, el)
    v2 = jnp.max(el2, axis=-1, keepdims=True)
    i2 = jnp.min(jnp.where(el2 == v2, lane, big), axis=-1, keepdims=True)
    t = jnp.exp(v2 - v1)
    w1 = gp / (1.0 + t)
    w2 = gp * t / (1.0 + t)
    e1 = i1 - base
    e2 = i2 - base
    lo = jnp.minimum(e1, e2)
    hi = jnp.maximum(e1, e2)
    w_lo = jnp.where(e1 < e2, w1, w2)
    w_hi = jnp.where(e1 < e2, w2, w1)
    pair = (lo * (7 - lo)) // 2 + hi - lo - 1
    pair = jnp.where(pair == 3, 4, jnp.where(pair == 4, 3, pair))
    w_a = jnp.where(pair == 5, w_hi, w_lo)
    w_b = jnp.where(pair == 5, w_lo, w_hi)
    cls = gsel * N_PAIRS + pair
    onehot = jnp.where(lane == cls, 1.0, 0.0)
    before = jnp.dot(tri_ref[...], onehot.astype(BF16), preferred_element_type=F32) + carry_ref[...]
    rank = jnp.sum(onehot * before, axis=-1, keepdims=True)
    carry_ref[...] = carry_ref[...] + jnp.sum(onehot, axis=0, keepdims=True)
    cnt_ref[...] = carry_ref[...]
    ui_ref[:, D:] = jnp.where(lane == 0, cls.astype(F32),
                              jnp.where(lane == 1, w_a, jnp.where(lane == 2, w_b, jnp.where(lane == 3, rank, 0.0))))


def _moe_route(x, g, shift, scale, wr, br, per_row, seq_len, rows_all, row_block0, prev):
    rows = x.shape[0]
    tm = _row_tile(rows)
    tps = max(seq_len // tm, 1)
    mod = _mod_spec(tm, per_row, tps, 1)
    r = lax.broadcasted_iota(jnp.int32, (tm, tm), 0)
    c = lax.broadcasted_iota(jnp.int32, (tm, tm), 1)
    tri = (c < r).astype(BF16)
    cnt0 = jnp.zeros((1, LANES), F32) if prev is None else prev[1]
    in_specs = [pl.BlockSpec((tm, D), lambda i: (i, 0)),
                pl.BlockSpec((1, D), lambda i: (0, 0)),
                mod, mod,
                pl.BlockSpec((2, D, LANES), lambda i: (0, 0, 0)),
                pl.BlockSpec((1, LANES), lambda i: (0, 0)),
                pl.BlockSpec((tm, tm), lambda i: (0, 0)),
                pl.BlockSpec((1, LANES), lambda i: (0, 0))]
    args = [x, g, shift, scale, wr, br, tri, cnt0]
    aliases = {}
    if prev is not None:
        in_specs += [pl.BlockSpec(memory_space=pl.ANY)]
        args += [prev[0]]
        aliases = {8: 0}
    return pl.pallas_call(
        _moe_route_kernel,
        out_shape=(jax.ShapeDtypeStruct((rows_all, D + LANES), F32), jax.ShapeDtypeStruct((1, LANES), F32)),
        grid=(rows // tm,),
        in_specs=in_specs,
        out_specs=(pl.BlockSpec((tm, D + LANES), lambda i: (i + row_block0, 0)),
                   pl.BlockSpec((1, LANES), lambda i: (0, 0))),
        scratch_shapes=[pltpu.VMEM((1, LANES), F32)],
        input_output_aliases=aliases,
        compiler_params=_cparams(("arbitrary",)),
        name="moe_route",
    )(*args)


def _moe_expert_kernel(ea_ref, eb_ref, valid_ref, newa_ref, newb_ref, x_ref, wga_ref, wua_ref, wda_ref,
                       wgb_ref, wub_ref, wdb_ref, o_ref, ga_ref, ua_ref, da_ref, gb_ref, ub_ref, db_ref):
    t = pl.program_id(0)

    @pl.when(newa_ref[t] == 1)
    def _():
        ga_ref[...] = wga_ref[...].astype(BF16)
        ua_ref[...] = wua_ref[...].astype(BF16)
        da_ref[...] = wda_ref[...].astype(BF16)

    @pl.when(newb_ref[t] == 1)
    def _():
        gb_ref[...] = wgb_ref[...].astype(BF16)
        ub_ref[...] = wub_ref[...].astype(BF16)
        db_ref[...] = wdb_ref[...].astype(BF16)

    @pl.when(valid_ref[t] == 1)
    def _():
        x = x_ref[:, :D].astype(BF16)

        def expert(wg_ref, wu_ref, wd_ref, w):
            hg = jnp.dot(x, wg_ref[...], preferred_element_type=F32)
            hu = jnp.dot(x, wu_ref[...], preferred_element_type=F32)
            act = _silu(hg) * hu * w
            return jnp.dot(act.astype(BF16), wd_ref[...], preferred_element_type=F32)

        o_ref[...] = (expert(ga_ref, ua_ref, da_ref, x_ref[:, D + 1:D + 2])
                      + expert(gb_ref, ub_ref, db_ref, x_ref[:, D + 2:D + 3]))

    @pl.when(valid_ref[t] == 0)
    def _():
        o_ref[...] = jnp.zeros(o_ref.shape, F32)


def _moe_experts(layer, tile_ea, tile_eb, tile_valid, tile_newa, tile_newb, ui_sorted, w_gate, w_up, w_down):
    n_tiles = tile_ea.shape[0]

    def wspec(shape, which):
        if which == 0:
            return pl.BlockSpec((None, None) + shape, lambda t, ea, eb, va, na, nb: (layer, ea[t], 0, 0))
        return pl.BlockSpec((None, None) + shape, lambda t, ea, eb, va, na, nb: (layer, eb[t], 0, 0))

    up = (D, D_EXPERT)
    down = (D_EXPERT, D)
    grid_spec = pltpu.PrefetchScalarGridSpec(
        num_scalar_prefetch=5,
        grid=(n_tiles,),
        in_specs=[pl.BlockSpec((MOE_TILE, D + LANES), lambda t, ea, eb, va, na, nb: (t, 0)),
                  wspec(up, 0), wspec(up, 0), wspec(down, 0),
                  wspec(up, 1), wspec(up, 1), wspec(down, 1)],
        out_specs=pl.BlockSpec((MOE_TILE, D), lambda t, ea, eb, va, na, nb: (t, 0)),
        scratch_shapes=[pltpu.VMEM(up, BF16), pltpu.VMEM(up, BF16), pltpu.VMEM(down, BF16),
                        pltpu.VMEM(up, BF16), pltpu.VMEM(up, BF16), pltpu.VMEM(down, BF16)],
    )
    return pl.pallas_call(
        _moe_expert_kernel,
        out_shape=jax.ShapeDtypeStruct((n_tiles * MOE_TILE, D), F32),
        grid_spec=grid_spec,
        compiler_params=_cparams(("arbitrary",)),
        name="moe_experts",
    )(tile_ea, tile_eb, tile_valid, tile_newa, tile_newb, ui_sorted,
      w_gate, w_up, w_down, w_gate, w_up, w_down)


_PAIR_A = (0, 0, 0, 1, 1, 3)
_PAIR_B = (1, 2, 3, 3, 2, 2)


def _lookup(table, idx):
    n = table.shape[0]
    return jnp.sum(jnp.where(idx[:, None] == jnp.arange(n, dtype=jnp.int32)[None, :], table[None, :], 0), axis=1)


def _moe_plan(ui, counts_vec, n_tiles):
    rows = ui.shape[0]
    cls = ui[:, D].astype(jnp.int32)
    rank = ui[:, D + 3].astype(jnp.int32)
    counts = counts_vec[0, :N_CLASSES].astype(jnp.int32)
    padded = ((counts + MOE_TILE - 1) // MOE_TILE) * MOE_TILE
    ends = jnp.cumsum(padded)
    offs = ends - padded
    uoffs = jnp.cumsum(counts) - counts
    dest = _lookup(offs, cls) + rank
    order = jnp.argsort(cls, stable=True).astype(jnp.int32)
    pos = jnp.arange(n_tiles * MOE_TILE, dtype=jnp.int32)
    pcls = jnp.minimum(jnp.sum((pos[:, None] >= ends[None, :]).astype(jnp.int32), axis=1), N_CLASSES - 1)
    within = pos - _lookup(offs, pcls)
    real = (within < _lookup(counts, pcls)) & (pos < ends[-1])
    src = jnp.take(order, jnp.clip(_lookup(uoffs, pcls) + within, 0, rows - 1), mode="clip")
    src = jnp.where(real, src, pos % rows)
    tile_start = jnp.arange(n_tiles, dtype=jnp.int32) * MOE_TILE
    tile_valid = (tile_start < ends[-1]).astype(jnp.int32)
    last_cls = jnp.max(jnp.where(counts > 0, jnp.arange(N_CLASSES, dtype=jnp.int32), 0))
    tcls = jnp.where(tile_valid == 1, pcls[::MOE_TILE], last_cls)
    grp = tcls // N_PAIRS
    pair = tcls % N_PAIRS
    tile_ea = grp * EXPERTS_PER_GROUP + _lookup(jnp.asarray(_PAIR_A, jnp.int32), pair)
    tile_eb = grp * EXPERTS_PER_GROUP + _lookup(jnp.asarray(_PAIR_B, jnp.int32), pair)
    one = jnp.ones((1,), jnp.int32)
    tile_newa = jnp.concatenate([one, (tile_ea[1:] != tile_ea[:-1]).astype(jnp.int32)])
    tile_newb = jnp.concatenate([one, (tile_eb[1:] != tile_eb[:-1]).astype(jnp.int32)])
    return dest, src, (tile_ea, tile_eb, tile_valid, tile_newa, tile_newb)


def _residual_kernel(x_ref, y_ref, gate_ref, fg_ref, xo_ref, *, final_norm):
    xn = x_ref[...] + gate_ref[...] * y_ref[...]
    if final_norm:
        ms = jnp.mean(xn * xn, axis=-1, keepdims=True)
        xn = xn * lax.rsqrt(ms + NORM_EPS) * fg_ref[...]
    xo_ref[...] = xn


def _residual(x, y_all, gate, fg, per_row, seq_len, row_block0, final_norm):
    rows = x.shape[0]
    tm = _row_tile(rows)
    tps = max(seq_len // tm, 1)
    return pl.pallas_call(
        functools.partial(_residual_kernel, final_norm=final_norm),
        out_shape=jax.ShapeDtypeStruct((rows, D), F32),
        grid=(rows // tm,),
        in_specs=[pl.BlockSpec((tm, D), lambda i: (i, 0)),
                  pl.BlockSpec((tm, D), lambda i: (i + row_block0, 0)),
                  _mod_spec(tm, per_row, tps, 1),
                  pl.BlockSpec((1, D), lambda i: (0, 0))],
        out_specs=pl.BlockSpec((tm, D), lambda i: (i, 0)),
        compiler_params=_cparams(("arbitrary",)),
        name="residual",
    )(x, y_all, gate, fg)


def kernel(x_prompt, x_sample, c_prompt, c_sample, state_a_conv, state_a_h, cache_b_k0, cache_b_v0, cache_b_k1, cache_b_v1, cache_b_k2, cache_b_v2, state_c_conv, norm_mix_g, norm_ffn_g, ada_w, ada_b, final_norm_g, a_w_in_y, a_w_in_x, a_conv_w, a_conv_b, a_gate_r_w, a_gate_r_b, a_gate_i_w, a_gate_i_b, a_lambda, a_w_out, b_w_qkv, b_w_o, conf_w_pw1, conf_b_pw1, conf_w_dw, conf_b_dw, conf_ln_g, conf_ln_b, conf_w_pw2, conf_b_pw2, moe_w_grouter, moe_b_grouter, moe_w_erouter, moe_b_erouter, moe_w_gate, moe_w_up, moe_w_down):
    bsz, seq, _ = x_prompt.shape
    dbsz = x_sample.shape[0]
    depth = ada_w.shape[0]
    rows_p = bsz * seq
    rows_all = rows_p + dbsz
    n_tiles = rows_all // MOE_TILE + N_CLASSES
    caches_k = (cache_b_k0, cache_b_k1, cache_b_k2)
    caches_v = (cache_b_v0, cache_b_v1, cache_b_v2)

    c_rows = -(-(bsz + dbsz) // SUBLANES) * SUBLANES
    c_all = jnp.concatenate([c_prompt, c_sample, jnp.zeros((c_rows - bsz - dbsz, D), F32)], axis=0)
    mods = _ada_mod(c_all, ada_w, ada_b)

    cos_all, sin_all = _rope_table(seq + SUBLANES)
    tm_p = _row_tile(rows_p)
    cos_p = jnp.concatenate([_swap_rotary(cos_all[:seq]), jnp.ones((tm_p, HEAD_DIM), F32)], axis=0)
    sin_p = jnp.concatenate([_swap_rotary(sin_all[:seq]), jnp.zeros((tm_p, HEAD_DIM), F32)], axis=0)
    cos_s = jnp.concatenate([jnp.broadcast_to(cos_all[PAST_LEN:PAST_LEN + 1], (dbsz, HEAD_DIM)),
                             jnp.ones((dbsz, HEAD_DIM), F32)], axis=0)
    sin_s = jnp.concatenate([jnp.broadcast_to(sin_all[PAST_LEN:PAST_LEN + 1], (dbsz, HEAD_DIM)),
                             jnp.zeros((dbsz, HEAD_DIM), F32)], axis=0)

    bf = lambda w: w.astype(BF16)
    a_w_in_y, a_w_in_x, a_w_out = bf(a_w_in_y), bf(a_w_in_x), bf(a_w_out)
    a_gate_r_w, a_gate_i_w = bf(a_gate_r_w), bf(a_gate_i_w)
    b_w_qkv, b_w_o = bf(b_w_qkv), bf(b_w_o)
    conf_w_pw1, conf_w_pw2 = bf(conf_w_pw1), bf(conf_w_pw2)

    xp = x_prompt.reshape(rows_p, D)
    xs = x_sample.reshape(dbsz, D)
    row1 = lambda v: v.reshape(1, -1)
    zero_bias = jnp.zeros((1, D), F32)

    a_conv_p, a_conv_s, a_h_p, a_h_s = [], [], [], []
    kp, vp, ksm, vsm = ([[] for _ in range(N_GROUPS)] for _ in range(4))
    cf_p, cf_s = [], []

    pend_p = pend_s = None
    for i in range(depth):
        kind, j = i % 3, i // 3
        mp = [mods[i, :bsz, k * D:(k + 1) * D].reshape(bsz, 1, D) for k in range(6)]
        ms = [mods[i, bsz:bsz + dbsz, k * D:(k + 1) * D] for k in range(6)]
        g_mix = row1(norm_mix_g[i])
        if kind == 0:
            wts = (a_conv_w[j], row1(a_conv_b[j]), a_gate_r_w[j], row1(a_gate_r_b[j]),
                   a_gate_i_w[j], row1(a_gate_i_b[j]), row1(a_lambda[j]), a_w_out[j])
            ybr, xin, xp = _proj_a(xp, g_mix, mp[0], mp[1], a_w_in_y[j], a_w_in_x[j], False, seq, pend_p)
            xp3, h_last = _rglru_prompt(xin.reshape(bsz, seq, D_RNN), ybr.reshape(bsz, seq, D_RNN),
                                        xp.reshape(bsz, seq, D), mp[2], *wts)
            xp = xp3.reshape(rows_p, D)
            a_conv_p.append(xin.reshape(bsz, seq, D_RNN)[:, seq - (LRU_CONV - 1):])
            a_h_p.append(h_last.reshape(bsz, D_RNN))
            ybr_s, xin_s, xs = _proj_a(xs, g_mix, ms[0], ms[1], a_w_in_y[j], a_w_in_x[j], True, 1, pend_s)
            hist = state_a_conv[j]
            xs, h_new = _rglru_step(xin_s, ybr_s, xs, ms[2], hist.reshape(dbsz, (LRU_CONV - 1) * D_RNN),
                                    state_a_h[j], *wts)
            a_conv_s.append(jnp.concatenate([hist[:, 1:], xin_s[:, None, :]], axis=1))
            a_h_s.append(h_new)
        elif kind == 1:
            w_sw = b_w_qkv[j]
            w_qk = _swap_rotary(w_sw[:, :2 * N_GROUPS * D].reshape(D, 2 * N_GROUPS * N_HEADS, HEAD_DIM))
            w_sw = jnp.concatenate([w_qk.reshape(D, 2 * N_GROUPS * D), w_sw[:, 2 * N_GROUPS * D:]], axis=1)
            qkv, xp = _proj_qkv(xp, g_mix, mp[0], mp[1], cos_p, sin_p, w_sw, False, seq, pend_p, True)
            qkv3 = qkv.reshape(bsz, seq, QKV_WIDTH)
            for g, (win, dil) in enumerate(ATTN_GROUPS):
                keep = min(win, seq)
                kcol = (N_GROUPS + g) * D
                vcol = (2 * N_GROUPS + g) * D
                k_tail = qkv3[:, seq - keep:, kcol:kcol + D].reshape(bsz, keep, N_HEADS, HEAD_DIM)
                kp[g].append(_swap_rotary(k_tail, inverse=True))
                vp[g].append(qkv3[:, seq - keep:, vcol:vcol + D].reshape(bsz, keep, N_HEADS, HEAD_DIM))
            o_p = _attn_prompt(qkv3).reshape(rows_p, D)
            xp = _out_proj(o_p, b_w_o[j], zero_bias, xp, mp[2], False, seq)
            qkv_s, xs = _proj_qkv(xs, g_mix, ms[0], ms[1], cos_s, sin_s, b_w_qkv[j], True, 1, pend_s, False)
            o_s = _attn_step(qkv_s, [c[j] for c in caches_k], [c[j] for c in caches_v])
            xs = _out_proj(o_s, b_w_o[j], zero_bias, xs, ms[2], True, 1)
            for g in range(N_GROUPS):
                kcol = (N_GROUPS + g) * D
                vcol = (2 * N_GROUPS + g) * D
                ksm[g].append(qkv_s[:, kcol:kcol + D].reshape(dbsz, 1, N_HEADS, HEAD_DIM))
                vsm[g].append(qkv_s[:, vcol:vcol + D].reshape(dbsz, 1, N_HEADS, HEAD_DIM))
        else:
            wts = (conf_w_dw[j], row1(conf_b_dw[j]), row1(conf_ln_g[j]), row1(conf_ln_b[j]),
                   conf_w_pw2[j], row1(conf_b_pw2[j]))
            b1 = row1(conf_b_pw1[j])
            glu, xp = _proj_glu(xp, g_mix, mp[0], mp[1], conf_w_pw1[j], b1, False, seq, pend_p)
            glu3 = glu.reshape(bsz, seq, D)
            xp = _conf_prompt(glu3, xp.reshape(bsz, seq, D), mp[2], *wts).reshape(rows_p, D)
            cf_p.append(glu3[:, seq - (CONF_WIDTH - 1):])
            glu_s, xs = _proj_glu(xs, g_mix, ms[0], ms[1], conf_w_pw1[j], b1, True, 1, pend_s)
            hist = state_c_conv[j]
            xs = _conf_step(glu_s, hist.reshape(dbsz, (CONF_WIDTH - 1) * D), xs, ms[2], *wts)
            cf_s.append(jnp.concatenate([hist[:, 1:], glu_s[:, None, :]], axis=1))

        g_ffn = row1(norm_ffn_g[i])
        wr = jnp.concatenate([moe_w_grouter[i], moe_w_erouter[i],
                              jnp.zeros((D, LANES - N_EGROUPS - N_EXPERTS), F32)], axis=1)
        wr_hi = wr.astype(BF16)
        wr = jnp.stack([wr_hi, (wr - wr_hi.astype(F32)).astype(BF16)])
        br = jnp.concatenate([moe_b_grouter[i], moe_b_erouter[i],
                              jnp.zeros((LANES - N_EGROUPS - N_EXPERTS,), F32)]).reshape(1, LANES)
        joint = _moe_route(xp, g_ffn, mp[3], mp[4], wr, br, False, seq, rows_all, 0, None)
        ui_all, counts = _moe_route(xs, g_ffn, ms[3], ms[4], wr, br, True, 1, rows_all, rows_p // dbsz, joint)
        dest, src, tiles = _moe_plan(ui_all, counts, n_tiles)
        ui_sorted = jnp.take(ui_all, src, axis=0, mode="clip")
        y_sorted = _moe_experts(i, *tiles, ui_sorted, moe_w_gate, moe_w_up, moe_w_down)
        y_all = jnp.take(y_sorted, dest, axis=0, mode="clip")
        if i < depth - 1:
            pend_p = (y_all, mp[5], 0)
            pend_s = (y_all, ms[5], rows_p // dbsz)
        else:
            fg = row1(final_norm_g)
            xp = _residual(xp, y_all, mp[5], fg, False, seq, 0, True)
            xs = _residual(xs, y_all, ms[5], fg, True, 1, rows_p // dbsz, True)

    y_prompt = xp.reshape(bsz, seq, D)
    y_sample = xs.reshape(dbsz, 1, D)
    return (y_prompt, y_sample,
            jnp.stack(a_conv_p), jnp.stack(a_conv_s), jnp.stack(a_h_p), jnp.stack(a_h_s),
            jnp.stack(kp[0]), jnp.stack(ksm[0]), jnp.stack(vp[0]), jnp.stack(vsm[0]),
            jnp.stack(kp[1]), jnp.stack(ksm[1]), jnp.stack(vp[1]), jnp.stack(vsm[1]),
            jnp.stack(kp[2]), jnp.stack(ksm[2]), jnp.stack(vp[2]), jnp.stack(vsm[2]),
            jnp.stack(cf_p), jnp.stack(cf_s))
```

```python
import functools
import math

import jax
import jax.numpy as jnp
from jax import lax
from jax.experimental import pallas as pl
from jax.experimental.pallas import tpu as pltpu

F32 = jnp.float32
BF16 = jnp.bfloat16
HIGHEST = lax.Precision.HIGHEST

D = 1024
D_RNN = 1280
LRU_BLOCKS = 10
LRU_BLOCK = 128
LRU_CONV = 4
LRU_C = 8.0
ATTN_GROUPS = ((128, 1), (512, 4), (2048, 16))
N_GROUPS = 3
N_HEADS = 8
HEAD_DIM = 128
ROT_DIM = 32
ROPE_THETA = 500000.0
QKV_WIDTH = 9 * D
CONF_WIDTH = 31
N_EGROUPS = 4
EXPERTS_PER_GROUP = 4
N_EXPERTS = 16
D_EXPERT = 512
N_PAIRS = 6
N_CLASSES = N_EGROUPS * N_PAIRS
NORM_EPS = 1e-6
PAST_LEN = 8192

LANES = 128
SUBLANES = 8
ATT_BLOCK = 128
ATT_TOKENS = 2048
QKV_CHUNK = 256
ATT_BATCH = 16
MOE_TILE = 256
VMEM_LIMIT = 56 * 1024 * 1024


def _cparams(sem):
    return pltpu.CompilerParams(dimension_semantics=sem, vmem_limit_bytes=VMEM_LIMIT)


def _sigmoid(x):
    return 1.0 / (1.0 + jnp.exp(-x))


def _silu(x):
    return x * _sigmoid(x)


def _gelu_tanh(x):
    return 0.5 * x * (1.0 + jnp.tanh(math.sqrt(2.0 / math.pi) * (x + 0.044715 * (x * x * x))))


def _softplus(x):
    return jnp.maximum(x, 0.0) + jnp.log1p(jnp.exp(-jnp.abs(x)))


def _bdot(a, b):
    return jnp.dot(a.astype(BF16), b.astype(BF16), preferred_element_type=F32)


def _norm_mod(x, g, shift, scale):
    ms = jnp.mean(x * x, axis=-1, keepdims=True)
    y = x * lax.rsqrt(ms + NORM_EPS) * g
    return y * (1.0 + scale) + shift


def _ada_kernel(c_ref, w_ref, b_ref, o_ref):
    c = c_ref[...]
    o_ref[...] = jnp.dot(_silu(c), w_ref[...], preferred_element_type=F32, precision=HIGHEST) + b_ref[...]


def _ada_mod(c_all, ada_w, ada_b):
    depth, _, n6 = ada_w.shape
    rows = c_all.shape[0]
    tn = 3072
    return pl.pallas_call(
        _ada_kernel,
        out_shape=jax.ShapeDtypeStruct((depth, rows, n6), F32),
        grid=(depth, n6 // tn),
        in_specs=[pl.BlockSpec((rows, D), lambda l, j: (0, 0)),
                  pl.BlockSpec((None, D, tn), lambda l, j: (l, 0, j)),
                  pl.BlockSpec((None, 1, tn), lambda l, j: (l, 0, j))],
        out_specs=pl.BlockSpec((None, rows, tn), lambda l, j: (l, 0, j)),
        compiler_params=_cparams(("arbitrary", "arbitrary")),
        name="ada_mod",
    )(c_all, ada_w, ada_b.reshape(depth, 1, n6))


def _mod_spec(tm, per_row, tiles_per_seq, ngrid):
    if per_row:
        if ngrid == 1:
            return pl.BlockSpec((tm, D), lambda i: (i, 0))
        return pl.BlockSpec((tm, D), lambda i, j: (i, 0))
    if ngrid == 1:
        return pl.BlockSpec((None, 1, D), lambda i: (i // tiles_per_seq, 0, 0))
    return pl.BlockSpec((None, 1, D), lambda i, j: (i // tiles_per_seq, 0, 0))


def _row_tile(rows):
    return 1024 if rows % 1024 == 0 else rows


def _load_x(x_ref, pend_refs, xn_ref):
    if pend_refs is None:
        return x_ref[...]
    y_ref, gate_ref = pend_refs
    x = x_ref[...] + gate_ref[...] * y_ref[...]
    xn_ref[...] = x
    return x


def _pend_specs(pending, tm, per_row, tps, ngrid):
    if pending is None:
        return [], [], [], []
    y_all, gate, row_block0 = pending
    if ngrid == 1:
        yspec = pl.BlockSpec((tm, D), lambda i: (i + row_block0, 0))
        ospec = pl.BlockSpec((tm, D), lambda i: (i, 0))
    else:
        yspec = pl.BlockSpec((tm, D), lambda i, j: (i + row_block0, 0))
        ospec = pl.BlockSpec((tm, D), lambda i, j: (i, 0))
    return [yspec, _mod_spec(tm, per_row, tps, ngrid)], [y_all, gate], [ospec], [D]


def _proj_a_kernel(*refs, fuse):
    x_ref, refs = refs[0], refs[1:]
    pend, refs = (refs[:2], refs[2:]) if fuse else (None, refs)
    g_ref, sh_ref, sc_ref, wy_ref, wx_ref, y_ref, xi_ref = refs[:7]
    x = _load_x(x_ref, pend, refs[7] if fuse else None)
    u = _norm_mod(x, g_ref[...], sh_ref[...], sc_ref[...]).astype(BF16)
    y_ref[...] = _gelu_tanh(jnp.dot(u, wy_ref[...], preferred_element_type=F32))
    xi_ref[...] = jnp.dot(u, wx_ref[...], preferred_element_type=F32)


def _proj_a(x, g, shift, scale, w_y, w_x, per_row, seq_len, pending):
    rows = x.shape[0]
    tm = 512 if rows % 512 == 0 else rows
    tps = max(seq_len // tm, 1)
    mod = _mod_spec(tm, per_row, tps, 1)
    wspec = pl.BlockSpec((D, D_RNN), lambda i: (0, 0))
    ospec = pl.BlockSpec((tm, D_RNN), lambda i: (i, 0))
    pspecs, pargs, pout, pwidth = _pend_specs(pending, tm, per_row, tps, 1)
    outs = pl.pallas_call(
        functools.partial(_proj_a_kernel, fuse=pending is not None),
        out_shape=tuple(jax.ShapeDtypeStruct((rows, n), F32) for n in [D_RNN, D_RNN] + pwidth),
        grid=(rows // tm,),
        in_specs=[pl.BlockSpec((tm, D), lambda i: (i, 0))] + pspecs + [
            pl.BlockSpec((1, D), lambda i: (0, 0)), mod, mod, wspec, wspec],
        out_specs=tuple([ospec, ospec] + pout),
        compiler_params=_cparams(("arbitrary",)),
        name="proj_a",
    )(x, *pargs, g, shift, scale, w_y, w_x)
    return outs[0], outs[1], (outs[2] if pending is not None else x)


def _scan_rows(a, b, h0):
    n, lanes = a.shape
    groups = n // SUBLANES
    a3 = a.reshape(groups, SUBLANES, lanes)
    b3 = b.reshape(groups, SUBLANES, lanes)
    sub = lax.broadcasted_iota(jnp.int32, a3.shape, 1)
    d = 1
    while d < SUBLANES:
        keep = sub >= d
        a_sh = jnp.where(keep, pltpu.roll(a3, d, 1), 1.0)
        b_sh = jnp.where(keep, pltpu.roll(b3, d, 1), 0.0)
        b3 = a3 * b_sh + b3
        a3 = a3 * a_sh
        d *= 2
    hs = []
    h_prev = h0
    for j in range(groups):
        hj = a3[j] * h_prev + b3[j]
        hs.append(hj)
        h_prev = hj[SUBLANES - 1:SUBLANES, :]
    return jnp.concatenate(hs, axis=0), h_prev


def _lru_gates(xc, grw, grb, giw, gib, lam):
    xcb = xc.astype(BF16)
    r = _sigmoid(jnp.dot(xcb, grw, preferred_element_type=F32) + grb)
    i = _sigmoid(jnp.dot(xcb, giw, preferred_element_type=F32) + gib)
    log_a = -LRU_C * r * _softplus(-lam)
    a = jnp.exp(log_a)
    b = jnp.sqrt(jnp.tanh(-log_a) * (a * a + 1.0)) * (i * xc)
    return a, b


def _rglru_kernel(xin_ref, ybr_ref, x_ref, gate_ref, cw_ref, cb_ref, grw_ref, grb_ref, giw_ref, gib_ref,
                  lam_ref, wo_ref, xo_ref, hl_ref, xx_ref, hc_ref, hy_ref, *, tt):
    t = pl.program_id(1)

    @pl.when(t == 0)
    def _():
        xx_ref[0:SUBLANES, :] = jnp.zeros((SUBLANES, D_RNN), F32)
        hc_ref[...] = jnp.zeros((1, D_RNN), F32)

    @pl.when(t > 0)
    def _():
        xx_ref[0:SUBLANES, :] = xx_ref[tt:tt + SUBLANES, :]

    xx_ref[SUBLANES:tt + SUBLANES, :] = xin_ref[...]
    for n in range(LRU_BLOCKS):
        sl = slice(n * LRU_BLOCK, (n + 1) * LRU_BLOCK)
        xc = cb_ref[:, sl]
        for k in range(LRU_CONV):
            off = SUBLANES - (LRU_CONV - 1) + k
            xc = xc + cw_ref[k:k + 1, sl] * xx_ref[off:off + tt, sl]
        a, bt = _lru_gates(xc, grw_ref[n], grb_ref[:, sl], giw_ref[n], gib_ref[:, sl], lam_ref[:, sl])
        h, h_last = _scan_rows(a, bt, hc_ref[:, sl])
        hc_ref[:, sl] = h_last
        hy_ref[:, sl] = (h * ybr_ref[:, sl]).astype(BF16)
    out = jnp.dot(hy_ref[...], wo_ref[...], preferred_element_type=F32)
    xo_ref[...] = x_ref[...] + gate_ref[...] * out
    hl_ref[...] = hc_ref[...]


def _rglru_prompt(xin, ybr, x, gate, cw, cb, grw, grb, giw, gib, lam, wo):
    bsz, seq, _ = xin.shape
    tt = 512
    full2 = lambda b, t: (0, 0)
    full3 = lambda b, t: (0, 0, 0)
    return pl.pallas_call(
        functools.partial(_rglru_kernel, tt=tt),
        out_shape=(jax.ShapeDtypeStruct((bsz, seq, D), F32), jax.ShapeDtypeStruct((bsz, 1, D_RNN), F32)),
        grid=(bsz, seq // tt),
        in_specs=[pl.BlockSpec((None, tt, D_RNN), lambda b, t: (b, t, 0)),
                  pl.BlockSpec((None, tt, D_RNN), lambda b, t: (b, t, 0)),
                  pl.BlockSpec((None, tt, D), lambda b, t: (b, t, 0)),
                  pl.BlockSpec((None, 1, D), lambda b, t: (b, 0, 0)),
                  pl.BlockSpec((LRU_CONV, D_RNN), full2),
                  pl.BlockSpec((1, D_RNN), full2),
                  pl.BlockSpec((LRU_BLOCKS, LRU_BLOCK, LRU_BLOCK), full3),
                  pl.BlockSpec((1, D_RNN), full2),
                  pl.BlockSpec((LRU_BLOCKS, LRU_BLOCK, LRU_BLOCK), full3),
                  pl.BlockSpec((1, D_RNN), full2),
                  pl.BlockSpec((1, D_RNN), full2),
                  pl.BlockSpec((D_RNN, D), full2)],
        out_specs=(pl.BlockSpec((None, tt, D), lambda b, t: (b, t, 0)),
                   pl.BlockSpec((None, 1, D_RNN), lambda b, t: (b, 0, 0))),
        scratch_shapes=[pltpu.VMEM((tt + SUBLANES, D_RNN), F32),
                        pltpu.VMEM((1, D_RNN), F32),
                        pltpu.VMEM((tt, D_RNN), BF16)],
        compiler_params=_cparams(("arbitrary", "arbitrary")),
        name="rglru_prompt",
    )(xin, ybr, x, gate, cw, cb, grw, grb, giw, gib, lam, wo)


def _rglru_step_kernel(xin_ref, ybr_ref, x_ref, gate_ref, hist_ref, hprev_ref, cw_ref, cb_ref, grw_ref, grb_ref,
                       giw_ref, gib_ref, lam_ref, wo_ref, xo_ref, hn_ref, hy_ref):
    for n in range(LRU_BLOCKS):
        sl = slice(n * LRU_BLOCK, (n + 1) * LRU_BLOCK)
        xc = cb_ref[:, sl] + cw_ref[LRU_CONV - 1:LRU_CONV, sl] * xin_ref[:, sl]
        for k in range(LRU_CONV - 1):
            xc = xc + cw_ref[k:k + 1, sl] * hist_ref[:, k * D_RNN + n * LRU_BLOCK:k * D_RNN + (n + 1) * LRU_BLOCK]
        a, bt = _lru_gates(xc, grw_ref[n], grb_ref[:, sl], giw_ref[n], gib_ref[:, sl], lam_ref[:, sl])
        h = a * hprev_ref[:, sl] + bt
        hn_ref[:, sl] = h
        hy_ref[:, sl] = (h * ybr_ref[:, sl]).astype(BF16)
    out = jnp.dot(hy_ref[...], wo_ref[...], preferred_element_type=F32)
    xo_ref[...] = x_ref[...] + gate_ref[...] * out


def _rglru_step(xin, ybr, x, gate, hist, hprev, cw, cb, grw, grb, giw, gib, lam, wo):
    rows = x.shape[0]
    return pl.pallas_call(
        _rglru_step_kernel,
        out_shape=(jax.ShapeDtypeStruct((rows, D), F32), jax.ShapeDtypeStruct((rows, D_RNN), F32)),
        scratch_shapes=[pltpu.VMEM((rows, D_RNN), BF16)],
        compiler_params=pltpu.CompilerParams(vmem_limit_bytes=VMEM_LIMIT),
        name="rglru_step",
    )(xin, ybr, x, gate, hist, hprev, cw, cb, grw, grb, giw, gib, lam, wo)


def _rope_table_kernel(cos_ref, sin_ref):
    half = ROT_DIM // 2
    shape = cos_ref.shape
    pos = lax.broadcasted_iota(jnp.int32, shape, 0).astype(F32)
    lane = lax.broadcasted_iota(jnp.int32, shape, 1)
    fidx = jnp.where(lane < half, lane, lane - half).astype(F32)
    inv_freq = jnp.exp(fidx * (-2.0 * math.log(ROPE_THETA) / ROT_DIM))
    ang = pos * inv_freq
    rot = lane < ROT_DIM
    cos_ref[...] = jnp.where(rot, jnp.cos(ang), 1.0)
    sin_ref[...] = jnp.where(rot, jnp.where(lane < half, -jnp.sin(ang), jnp.sin(ang)), 0.0)


def _rope_table(n_pos):
    return pl.pallas_call(
        _rope_table_kernel,
        out_shape=(jax.ShapeDtypeStruct((n_pos, HEAD_DIM), F32), jax.ShapeDtypeStruct((n_pos, HEAD_DIM), F32)),
        compiler_params=pltpu.CompilerParams(vmem_limit_bytes=VMEM_LIMIT),
        name="rope_table",
    )()


def _proj_qkv_kernel(*refs, tn, fuse):
    x_ref, refs = refs[0], refs[1:]
    pend, refs = (refs[:2], refs[2:]) if fuse else (None, refs)
    g_ref, sh_ref, sc_ref, cos_ref, sin_ref, w_ref, o_ref = refs[:7]
    u_ref = refs[-1]
    j = pl.program_id(1)

    @pl.when(j == 0)
    def _():
        x = _load_x(x_ref, pend, refs[7] if fuse else None)
        u_ref[...] = _norm_mod(x, g_ref[...], sh_ref[...], sc_ref[...]).astype(BF16)

    def chunk(c):
        return jnp.dot(u_ref[...], w_ref[:, c * QKV_CHUNK:(c + 1) * QKV_CHUNK], preferred_element_type=F32)

    @pl.when(j < 2 * N_GROUPS * D // tn)
    def _():
        half = ROT_DIM // 2
        qscale = jnp.where(j < N_GROUPS * D // tn, HEAD_DIM ** -0.5, 1.0).astype(F32)
        cos = cos_ref[...] * qscale
        sin = sin_ref[...] * qscale
        lane = lax.broadcasted_iota(jnp.int32, cos.shape, 1)
        for c in range(tn // QKV_CHUNK):
            acc = chunk(c)
            for h in range(QKV_CHUNK // HEAD_DIM):
                xh = acc[:, h * HEAD_DIM:(h + 1) * HEAD_DIM]
                partner = jnp.where(lane < half, pltpu.roll(xh, HEAD_DIM - half, 1), pltpu.roll(xh, half, 1))
                col = c * QKV_CHUNK + h * HEAD_DIM
                o_ref[:, col:col + HEAD_DIM] = xh * cos + partner * sin

    @pl.when(j >= 2 * N_GROUPS * D // tn)
    def _():
        for c in range(tn // QKV_CHUNK):
            o_ref[:, c * QKV_CHUNK:(c + 1) * QKV_CHUNK] = chunk(c)


def _proj_qkv(x, g, shift, scale, cos_t, sin_t, w, per_row, seq_len, pending):
    rows = x.shape[0]
    tm = _row_tile(rows)
    tn = N_GROUPS * D // 2
    tps = max(seq_len // tm, 1)
    mod = _mod_spec(tm, per_row, tps, 2)
    n_rope = cos_t.shape[0] // tm
    rope = pl.BlockSpec((tm, HEAD_DIM), lambda i, j: (i % n_rope, 0))
    pspecs, pargs, pout, pwidth = _pend_specs(pending, tm, per_row, tps, 2)
    outs = pl.pallas_call(
        functools.partial(_proj_qkv_kernel, tn=tn, fuse=pending is not None),
        out_shape=tuple(jax.ShapeDtypeStruct((rows, n), F32) for n in [QKV_WIDTH] + pwidth),
        grid=(rows // tm, QKV_WIDTH // tn),
        in_specs=[pl.BlockSpec((tm, D), lambda i, j: (i, 0))] + pspecs + [
            pl.BlockSpec((1, D), lambda i, j: (0, 0)),
            mod, mod, rope, rope,
            pl.BlockSpec((D, tn), lambda i, j: (0, j))],
        out_specs=tuple([pl.BlockSpec((tm, tn), lambda i, j: (i, j))] + pout),
        scratch_shapes=[pltpu.VMEM((tm, D), BF16)],
        compiler_params=_cparams(("arbitrary", "arbitrary")),
        name="proj_qkv",
    )(x, *pargs, g, shift, scale, cos_t, sin_t, w)
    return outs[0], (outs[1] if pending is not None else x)


def _rows(start, dil):
    if dil == 1:
        return pl.ds(start, ATT_BLOCK)
    return pl.ds(start, ATT_BLOCK, stride=dil)


def _attn_prompt_kernel(*refs):
    in_refs, o_ref, og_ref, lg_ref = refs[:15], refs[15], refs[16], refs[17]
    n = ATT_BLOCK
    nblk = ATT_BATCH
    ri = lax.broadcasted_iota(jnp.int32, (nblk, n, n), 1)
    cj = lax.broadcasted_iota(jnp.int32, (nblk, n, n), 2)
    neg = -jnp.inf
    has_prev = pl.program_id(1) > 0
    qk_dims = (((2,), (2,)), ((0,), (0,)))
    pv_dims = (((2,), (1,)), ((0,), (0,)))
    ones = jnp.ones((nblk, n, HEAD_DIM), BF16)
    for g, (win, dil) in enumerate(ATTN_GROUPS):
        q_ref, kc_ref, vc_ref, kp_ref, vp_ref = in_refs[5 * g:5 * g + 5]
        all_blocks = [(s, r) for s in range(ATT_TOKENS // win) for r in range(dil)]
        for b0 in range(0, len(all_blocks), nblk):
            blocks = all_blocks[b0:b0 + nblk]

            def load(ref, s, r):
                return ref[_rows(s * win + r, dil), :].astype(BF16)

            q = jnp.stack([load(q_ref, s, r) for s, r in blocks])
            kc = jnp.stack([load(kc_ref, s, r) for s, r in blocks])
            vc = jnp.stack([load(vc_ref, s, r) for s, r in blocks])
            kp = jnp.stack([load(kc_ref, s - 1, r) if s > 0 else kp_ref[_rows(r, dil), :].astype(BF16)
                            for s, r in blocks])
            vp = jnp.stack([load(vc_ref, s - 1, r) if s > 0 else vp_ref[_rows(r, dil), :].astype(BF16)
                            for s, r in blocks])
            sp = lax.dot_general(q, kp, qk_dims, preferred_element_type=F32)
            sc = lax.dot_general(q, kc, qk_dims, preferred_element_type=F32)
            if blocks[0][0] == 0:
                n_first = sum(1 for s, _ in blocks if s == 0)
                blk = lax.broadcasted_iota(jnp.int32, (nblk, n, n), 0)
                sp = jnp.where(blk >= jnp.where(has_prev, 0, n_first), sp, neg)
            sp = jnp.where(cj >= ri, sp, neg)
            sc = jnp.where(cj <= ri, sc, neg)
            m = jnp.max(jnp.maximum(sp, sc), axis=-1, keepdims=True)
            ep = jnp.exp(sp - m).astype(BF16)
            ec = jnp.exp(sc - m).astype(BF16)
            l = (lax.dot_general(ep, ones, pv_dims, preferred_element_type=F32)
                 + lax.dot_general(ec, ones, pv_dims, preferred_element_type=F32))
            o = (lax.dot_general(ep, vp, pv_dims, preferred_element_type=F32)
                 + lax.dot_general(ec, vc, pv_dims, preferred_element_type=F32)) / l
            lse = m + jnp.log(l)
            for idx, (s, r) in enumerate(blocks):
                rows = _rows(s * win + r, dil)
                og_ref[g, rows, :] = o[idx]
                lg_ref[g, rows, :] = lse[idx]
    l0, l1, l2 = lg_ref[0], lg_ref[1], lg_ref[2]
    m = jnp.maximum(jnp.maximum(l0, l1), l2)
    e0, e1, e2 = jnp.exp(l0 - m), jnp.exp(l1 - m), jnp.exp(l2 - m)
    o = (e0 * og_ref[0] + e1 * og_ref[1] + e2 * og_ref[2]) / (e0 + e1 + e2)
    o_ref[...] = o.astype(o_ref.dtype)


def _attn_prompt(qkv):
    bsz, seq, _ = qkv.shape
    tb = ATT_TOKENS
    specs = []
    for g, (win, dil) in enumerate(ATTN_GROUPS):
        per = tb // win

        def cur(which, g=g):
            return pl.BlockSpec((None, tb, HEAD_DIM),
                                lambda b, i, h: (b, i, (which * N_GROUPS + g) * N_HEADS + h))

        def prev(which, g=g, win=win, per=per):
            return pl.BlockSpec((None, win, HEAD_DIM),
                                lambda b, i, h: (b, jnp.maximum(i * per - 1, 0), (which * N_GROUPS + g) * N_HEADS + h))

        specs += [cur(0), cur(1), cur(2), prev(1), prev(2)]
    return pl.pallas_call(
        _attn_prompt_kernel,
        out_shape=jax.ShapeDtypeStruct((bsz, seq, D), BF16),
        grid=(bsz, seq // tb, N_HEADS),
        in_specs=specs,
        out_specs=pl.BlockSpec((None, tb, HEAD_DIM), lambda b, i, h: (b, i, h)),
        scratch_shapes=[pltpu.VMEM((N_GROUPS, tb, HEAD_DIM), F32),
                        pltpu.VMEM((N_GROUPS, tb, HEAD_DIM), F32)],
        compiler_params=_cparams(("arbitrary", "arbitrary", "arbitrary")),
        name="attn_prompt",
    )(*([qkv] * 15))


def _attn_step_kernel(*refs):
    q_refs, kn_refs, vn_refs = refs[0:3], refs[3:6], refs[6:9]
    kc_refs, vc_refs = refs[9:12], refs[12:15]
    o_ref = refs[15]
    outs, lses = [], []
    for g in range(N_GROUPS):
        q = q_refs[g][...]
        kn = kn_refs[g][...]
        vn = vn_refs[g][...]
        s_c = jnp.sum(kc_refs[g][...] * q[None], axis=-1, keepdims=True)
        s_n = jnp.sum(kn * q, axis=-1, keepdims=True)
        m = jnp.maximum(jnp.max(s_c, axis=0), s_n)
        e_c = jnp.exp(s_c - m[None])
        e_n = jnp.exp(s_n - m)
        l = jnp.sum(e_c, axis=0) + e_n
        outs.append((jnp.sum(e_c * vc_refs[g][...], axis=0) + e_n * vn) / l)
        lses.append(m + jnp.log(l))
    m = jnp.maximum(jnp.maximum(lses[0], lses[1]), lses[2])
    es = [jnp.exp(x - m) for x in lses]
    inv = 1.0 / (es[0] + es[1] + es[2])
    o_ref[...] = (es[0] * outs[0] + es[1] * outs[1] + es[2] * outs[2]) * inv


def _attn_step(qkv_s, caches_k, caches_v):
    bsz = qkv_s.shape[0]
    q4 = qkv_s.reshape(bsz, 9, N_HEADS, HEAD_DIM)

    def new_spec(which, g):
        return pl.BlockSpec((None, None, N_HEADS, HEAD_DIM), lambda b: (b, which * N_GROUPS + g, 0, 0))

    specs = [new_spec(w, g) for w in range(3) for g in range(N_GROUPS)]
    cache_spec = pl.BlockSpec((None, ATT_BLOCK, N_HEADS, HEAD_DIM), lambda b: (b, 0, 0, 0))
    ks, vs = [], []
    for g, (win, dil) in enumerate(ATTN_GROUPS):
        ks.append(caches_k[g].reshape(bsz, win // dil, dil * N_HEADS, HEAD_DIM))
        vs.append(caches_v[g].reshape(bsz, win // dil, dil * N_HEADS, HEAD_DIM))
    o = pl.pallas_call(
        _attn_step_kernel,
        out_shape=jax.ShapeDtypeStruct((bsz, N_HEADS, HEAD_DIM), F32),
        grid=(bsz,),
        in_specs=specs + [cache_spec] * 6,
        out_specs=pl.BlockSpec((None, N_HEADS, HEAD_DIM), lambda b: (b, 0, 0)),
        compiler_params=_cparams(("arbitrary",)),
        name="attn_step",
    )(*([q4] * 9), *ks, *vs)
    return o.reshape(bsz, D)


def _out_proj_kernel(a_ref, w_ref, b_ref, x_ref, gate_ref, xo_ref):
    out = jnp.dot(a_ref[...].astype(BF16), w_ref[...], preferred_element_type=F32) + b_ref[...]
    xo_ref[...] = x_ref[...] + gate_ref[...] * out


def _out_proj(a, w, bias, x, gate, per_row, seq_len):
    rows, k = a.shape
    tm = _row_tile(rows)
    tps = max(seq_len // tm, 1)
    return pl.pallas_call(
        _out_proj_kernel,
        out_shape=jax.ShapeDtypeStruct((rows, D), F32),
        grid=(rows // tm,),
        in_specs=[pl.BlockSpec((tm, k), lambda i: (i, 0)),
                  pl.BlockSpec((k, D), lambda i: (0, 0)),
                  pl.BlockSpec((1, D), lambda i: (0, 0)),
                  pl.BlockSpec((tm, D), lambda i: (i, 0)),
                  _mod_spec(tm, per_row, tps, 1)],
        out_specs=pl.BlockSpec((tm, D), lambda i: (i, 0)),
        compiler_params=_cparams(("arbitrary",)),
        name="out_proj",
    )(a, w, bias, x, gate)


def _proj_glu_kernel(*refs, fuse):
    x_ref, refs = refs[0], refs[1:]
    pend, refs = (refs[:2], refs[2:]) if fuse else (None, refs)
    g_ref, sh_ref, sc_ref, wa_ref, wb_ref, ba_ref, bb_ref, o_ref = refs[:8]
    x = _load_x(x_ref, pend, refs[8] if fuse else None)
    u = _norm_mod(x, g_ref[...], sh_ref[...], sc_ref[...]).astype(BF16)
    za = jnp.dot(u, wa_ref[...], preferred_element_type=F32) + ba_ref[...]
    zb = jnp.dot(u, wb_ref[...], preferred_element_type=F32) + bb_ref[...]
    o_ref[...] = za * _sigmoid(zb)


def _proj_glu(x, g, shift, scale, w, bias, per_row, seq_len, pending):
    rows = x.shape[0]
    tm = _row_tile(rows)
    tps = max(seq_len // tm, 1)
    mod = _mod_spec(tm, per_row, tps, 1)
    pspecs, pargs, pout, pwidth = _pend_specs(pending, tm, per_row, tps, 1)
    outs = pl.pallas_call(
        functools.partial(_proj_glu_kernel, fuse=pending is not None),
        out_shape=tuple(jax.ShapeDtypeStruct((rows, n), F32) for n in [D] + pwidth),
        grid=(rows // tm,),
        in_specs=[pl.BlockSpec((tm, D), lambda i: (i, 0))] + pspecs + [
            pl.BlockSpec((1, D), lambda i: (0, 0)),
            mod, mod,
            pl.BlockSpec((D, D), lambda i: (0, 0)),
            pl.BlockSpec((D, D), lambda i: (0, 1)),
            pl.BlockSpec((1, D), lambda i: (0, 0)),
            pl.BlockSpec((1, D), lambda i: (0, 1))],
        out_specs=tuple([pl.BlockSpec((tm, D), lambda i: (i, 0))] + pout),
        compiler_params=_cparams(("arbitrary",)),
        name="proj_glu",
    )(x, *pargs, g, shift, scale, w, w, bias, bias)
    return outs[0], (outs[1] if pending is not None else x)


def _layer_norm(y, g, b):
    mu = jnp.mean(y, axis=-1, keepdims=True)
    yc = y - mu
    var = jnp.mean(yc * yc, axis=-1, keepdims=True)
    return yc * lax.rsqrt(var + NORM_EPS) * g + b


CONF_HALO = 32
CONF_CHUNK = 128


def _conf_kernel(glu_ref, x_ref, gate_ref, wdw_ref, bdw_ref, lg_ref, lb_ref, w2_ref, b2_ref, xo_ref,
                 xx_ref, z_ref, p_ref, *, tt):
    t = pl.program_id(1)

    @pl.when(t == 0)
    def _():
        xx_ref[0:CONF_HALO, :] = jnp.zeros((CONF_HALO, D), F32)

    @pl.when(t > 0)
    def _():
        xx_ref[0:CONF_HALO, :] = xx_ref[tt:tt + CONF_HALO, :]

    xx_ref[CONF_HALO:tt + CONF_HALO, :] = glu_ref[...]
    first = CONF_HALO - (CONF_WIDTH - 1)
    for r0 in range(0, tt, CONF_CHUNK):
        for c in range(D // LANES):
            sl = slice(c * LANES, (c + 1) * LANES)
            acc = jnp.broadcast_to(bdw_ref[:, sl], (CONF_CHUNK, LANES))
            for r in range(SUBLANES):
                taps = [k for k in range(CONF_WIDTH) if (first + k) % SUBLANES == r]
                span = CONF_CHUNK if r == 0 else CONF_CHUNK + SUBLANES
                p = None
                for k in taps:
                    base = r0 + first + k - r
                    term = wdw_ref[k:k + 1, sl] * xx_ref[base:base + span, sl]
                    p = term if p is None else p + term
                if r == 0:
                    acc = acc + p
                else:
                    p_ref[r, :, :] = p
                    acc = acc + p_ref[r, r:r + CONF_CHUNK, :]
            z_ref[r0:r0 + CONF_CHUNK, sl] = acc
    y = _silu(_layer_norm(z_ref[...], lg_ref[...], lb_ref[...]))
    out = jnp.dot(y.astype(BF16), w2_ref[...], preferred_element_type=F32) + b2_ref[...]
    xo_ref[...] = x_ref[...] + gate_ref[...] * out


def _conf_prompt(glu, x, gate, wdw, bdw, lg, lb, w2, b2):
    bsz, seq, _ = glu.shape
    tt = 512
    full2 = lambda b, t: (0, 0)
    vec = pl.BlockSpec((1, D), full2)
    return pl.pallas_call(
        functools.partial(_conf_kernel, tt=tt),
        out_shape=jax.ShapeDtypeStruct((bsz, seq, D), F32),
        grid=(bsz, seq // tt),
        in_specs=[pl.BlockSpec((None, tt, D), lambda b, t: (b, t, 0)),
                  pl.BlockSpec((None, tt, D), lambda b, t: (b, t, 0)),
                  pl.BlockSpec((None, 1, D), lambda b, t: (b, 0, 0)),
                  pl.BlockSpec((CONF_WIDTH, D), full2),
                  vec, vec, vec,
                  pl.BlockSpec((D, D), full2),
                  vec],
        out_specs=pl.BlockSpec((None, tt, D), lambda b, t: (b, t, 0)),
        scratch_shapes=[pltpu.VMEM((tt + CONF_HALO, D), F32),
                        pltpu.VMEM((tt, D), F32),
                        pltpu.VMEM((SUBLANES, CONF_CHUNK + SUBLANES, LANES), F32)],
        compiler_params=_cparams(("arbitrary", "arbitrary")),
        name="conf_prompt",
    )(glu, x, gate, wdw, bdw, lg, lb, w2, b2)


def _conf_step_kernel(glu_ref, hist_ref, x_ref, gate_ref, wdw_ref, bdw_ref, lg_ref, lb_ref, w2_ref, b2_ref, xo_ref):
    acc = bdw_ref[...] + wdw_ref[CONF_WIDTH - 1:CONF_WIDTH, :] * glu_ref[...]
    for k in range(CONF_WIDTH - 1):
        acc = acc + wdw_ref[k:k + 1, :] * hist_ref[:, k * D:(k + 1) * D]
    y = _silu(_layer_norm(acc, lg_ref[...], lb_ref[...]))
    out = jnp.dot(y.astype(BF16), w2_ref[...], preferred_element_type=F32) + b2_ref[...]
    xo_ref[...] = x_ref[...] + gate_ref[...] * out


def _conf_step(glu, hist, x, gate, wdw, bdw, lg, lb, w2, b2):
    return pl.pallas_call(
        _conf_step_kernel,
        out_shape=jax.ShapeDtypeStruct(x.shape, F32),
        compiler_params=pltpu.CompilerParams(vmem_limit_bytes=VMEM_LIMIT),
        name="conf_step",
    )(glu, hist, x, gate, wdw, bdw, lg, lb, w2, b2)


def _moe_route_kernel(x_ref, g_ref, sh_ref, sc_ref, wr_ref, br_ref, tri_ref, cnt0_ref, *rest):
    ui_ref, cnt_ref, carry_ref = rest[-3], rest[-2], rest[-1]

    @pl.when(pl.program_id(0) == 0)
    def _():
        carry_ref[...] = cnt0_ref[...]

    u = _norm_mod(x_ref[...], g_ref[...], sh_ref[...], sc_ref[...])
    ui_ref[:, :D] = u
    u_hi = u.astype(BF16)
    u_lo = (u - u_hi.astype(F32)).astype(BF16)
    logits = (jnp.dot(u_hi, wr_ref[0], preferred_element_type=F32)
              + jnp.dot(u_lo, wr_ref[0], preferred_element_type=F32)
              + jnp.dot(u_hi, wr_ref[1], preferred_element_type=F32)) + br_ref[...]
    lane = lax.broadcasted_iota(jnp.int32, logits.shape, 1)
    neg = -jnp.inf
    big = jnp.int32(LANES)
    gl = jnp.where(lane < N_EGROUPS, logits, neg)
    gmax = jnp.max(gl, axis=-1, keepdims=True)
    gsel = jnp.min(jnp.where(gl == gmax, lane, big), axis=-1, keepdims=True)
    gp = 1.0 / jnp.sum(jnp.exp(gl - gmax), axis=-1, keepdims=True)
    base = N_EGROUPS + EXPERTS_PER_GROUP * gsel
    el = jnp.where((lane >= base) & (lane < base + EXPERTS_PER_GROUP), logits, neg)
    v1 = jnp.max(el, axis=-1, keepdims=True)
    i1 = jnp.min(jnp.where(el == v1, lane, big), axis=-1, keepdims=True)
    el2 = jnp.where(lane == i1, neg, el)
    v2 = jnp.max(el2, axis=-1, keepdims=True)
    i2 = jnp.min(jnp.where(el2 == v2, lane, big), axis=-1, keepdims=True)
    t = jnp.exp(v2 - v1)
    w1 = gp / (1.0 + t)
    w2 = gp * t / (1.0 + t)
    e1 = i1 - base
    e2 = i2 - base
    lo = jnp.minimum(e1, e2)
    hi = jnp.maximum(e1, e2)
    w_lo = jnp.where(e1 < e2, w1, w2)
    w_hi = jnp.where(e1 < e2, w2, w1)
    pair = (lo * (7 - lo)) // 2 + hi - lo - 1
    pair = jnp.where(pair == 3, 4, jnp.where(pair == 4, 3, pair))
    w_a = jnp.where(pair == 5, w_hi, w_lo)
    w_b = jnp.where(pair == 5, w_lo, w_hi)
    cls = gsel * N_PAIRS + pair
    onehot = jnp.where(lane == cls, 1.0, 0.0)
    before = jnp.dot(tri_ref[...], onehot.astype(BF16), preferred_element_type=F32) + carry_ref[...]
    rank = jnp.sum(onehot * before, axis=-1, keepdims=True)
    carry_ref[...] = carry_ref[...] + jnp.sum(onehot, axis=0, keepdims=True)
    cnt_ref[...] = carry_ref[...]
    ui_ref[:, D:] = jnp.where(lane == 0, cls.astype(F32),
                              jnp.where(lane == 1, w_a, jnp.where(lane == 2, w_b, jnp.where(lane == 3, rank, 0.0))))


def _moe_route(x, g, shift, scale, wr, br, per_row, seq_len, rows_all, row_block0, prev):
    rows = x.shape[0]
    tm = _row_tile(rows)
    tps = max(seq_len // tm, 1)
    mod = _mod_spec(tm, per_row, tps, 1)
    r = lax.broadcasted_iota(jnp.int32, (tm, tm), 0)
    c = lax.broadcasted_iota(jnp.int32, (tm, tm), 1)
    tri = (c < r).astype(BF16)
    cnt0 = jnp.zeros((1, LANES), F32) if prev is None else prev[1]
    in_specs = [pl.BlockSpec((tm, D), lambda i: (i, 0)),
                pl.BlockSpec((1, D), lambda i: (0, 0)),
                mod, mod,
                pl.BlockSpec((2, D, LANES), lambda i: (0, 0, 0)),
                pl.BlockSpec((1, LANES), lambda i: (0, 0)),
                pl.BlockSpec((tm, tm), lambda i: (0, 0)),
                pl.BlockSpec((1, LANES), lambda i: (0, 0))]
    args = [x, g, shift, scale, wr, br, tri, cnt0]
    aliases = {}
    if prev is not None:
        in_specs += [pl.BlockSpec(memory_space=pl.ANY)]
        args += [prev[0]]
        aliases = {8: 0}
    return pl.pallas_call(
        _moe_route_kernel,
        out_shape=(jax.ShapeDtypeStruct((rows_all, D + LANES), F32), jax.ShapeDtypeStruct((1, LANES), F32)),
        grid=(rows // tm,),
        in_specs=in_specs,
        out_specs=(pl.BlockSpec((tm, D + LANES), lambda i: (i + row_block0, 0)),
                   pl.BlockSpec((1, LANES), lambda i: (0, 0))),
        scratch_shapes=[pltpu.VMEM((1, LANES), F32)],
        input_output_aliases=aliases,
        compiler_params=_cparams(("arbitrary",)),
        name="moe_route",
    )(*args)


def _moe_expert_kernel(ea_ref, eb_ref, valid_ref, newa_ref, newb_ref, x_ref, wga_ref, wua_ref, wda_ref,
                       wgb_ref, wub_ref, wdb_ref, o_ref, ga_ref, ua_ref, da_ref, gb_ref, ub_ref, db_ref):
    t = pl.program_id(0)

    @pl.when(newa_ref[t] == 1)
    def _():
        ga_ref[...] = wga_ref[...].astype(BF16)
        ua_ref[...] = wua_ref[...].astype(BF16)
        da_ref[...] = wda_ref[...].astype(BF16)

    @pl.when(newb_ref[t] == 1)
    def _():
        gb_ref[...] = wgb_ref[...].astype(BF16)
        ub_ref[...] = wub_ref[...].astype(BF16)
        db_ref[...] = wdb_ref[...].astype(BF16)

    @pl.when(valid_ref[t] == 1)
    def _():
        x = x_ref[:, :D].astype(BF16)

        def expert(wg_ref, wu_ref, wd_ref, w):
            hg = jnp.dot(x, wg_ref[...], preferred_element_type=F32)
            hu = jnp.dot(x, wu_ref[...], preferred_element_type=F32)
            act = _silu(hg) * hu * w
            return jnp.dot(act.astype(BF16), wd_ref[...], preferred_element_type=F32)

        o_ref[...] = (expert(ga_ref, ua_ref, da_ref, x_ref[:, D + 1:D + 2])
                      + expert(gb_ref, ub_ref, db_ref, x_ref[:, D + 2:D + 3]))

    @pl.when(valid_ref[t] == 0)
    def _():
        o_ref[...] = jnp.zeros(o_ref.shape, F32)


def _moe_experts(layer, tile_ea, tile_eb, tile_valid, tile_newa, tile_newb, ui_sorted, w_gate, w_up, w_down):
    n_tiles = tile_ea.shape[0]

    def wspec(shape, which):
        if which == 0:
            return pl.BlockSpec((None, None) + shape, lambda t, ea, eb, va, na, nb: (layer, ea[t], 0, 0))
        return pl.BlockSpec((None, None) + shape, lambda t, ea, eb, va, na, nb: (layer, eb[t], 0, 0))

    up = (D, D_EXPERT)
    down = (D_EXPERT, D)
    grid_spec = pltpu.PrefetchScalarGridSpec(
        num_scalar_prefetch=5,
        grid=(n_tiles,),
        in_specs=[pl.BlockSpec((MOE_TILE, D + LANES), lambda t, ea, eb, va, na, nb: (t, 0)),
                  wspec(up, 0), wspec(up, 0), wspec(down, 0),
                  wspec(up, 1), wspec(up, 1), wspec(down, 1)],
        out_specs=pl.BlockSpec((MOE_TILE, D), lambda t, ea, eb, va, na, nb: (t, 0)),
        scratch_shapes=[pltpu.VMEM(up, BF16), pltpu.VMEM(up, BF16), pltpu.VMEM(down, BF16),
                        pltpu.VMEM(up, BF16), pltpu.VMEM(up, BF16), pltpu.VMEM(down, BF16)],
    )
    return pl.pallas_call(
        _moe_expert_kernel,
        out_shape=jax.ShapeDtypeStruct((n_tiles * MOE_TILE, D), F32),
        grid_spec=grid_spec,
        compiler_params=_cparams(("arbitrary",)),
        name="moe_experts",
    )(tile_ea, tile_eb, tile_valid, tile_newa, tile_newb, ui_sorted,
      w_gate, w_up, w_down, w_gate, w_up, w_down)


_PAIR_A = (0, 0, 0, 1, 1, 3)
_PAIR_B = (1, 2, 3, 3, 2, 2)


def _lookup(table, idx):
    n = table.shape[0]
    return jnp.sum(jnp.where(idx[:, None] == jnp.arange(n, dtype=jnp.int32)[None, :], table[None, :], 0), axis=1)


def _moe_plan(ui, counts_vec, n_tiles):
    rows = ui.shape[0]
    cls = ui[:, D].astype(jnp.int32)
    rank = ui[:, D + 3].astype(jnp.int32)
    counts = counts_vec[0, :N_CLASSES].astype(jnp.int32)
    padded = ((counts + MOE_TILE - 1) // MOE_TILE) * MOE_TILE
    ends = jnp.cumsum(padded)
    offs = ends - padded
    uoffs = jnp.cumsum(counts) - counts
    dest = _lookup(offs, cls) + rank
    order = jnp.argsort(cls, stable=True).astype(jnp.int32)
    pos = jnp.arange(n_tiles * MOE_TILE, dtype=jnp.int32)
    pcls = jnp.minimum(jnp.sum((pos[:, None] >= ends[None, :]).astype(jnp.int32), axis=1), N_CLASSES - 1)
    within = pos - _lookup(offs, pcls)
    real = (within < _lookup(counts, pcls)) & (pos < ends[-1])
    src = jnp.take(order, jnp.clip(_lookup(uoffs, pcls) + within, 0, rows - 1), mode="clip")
    src = jnp.where(real, src, pos % rows)
    tile_start = jnp.arange(n_tiles, dtype=jnp.int32) * MOE_TILE
    tile_valid = (tile_start < ends[-1]).astype(jnp.int32)
    last_cls = jnp.max(jnp.where(counts > 0, jnp.arange(N_CLASSES, dtype=jnp.int32), 0))
    tcls = jnp.where(tile_valid == 1, pcls[::MOE_TILE], last_cls)
    grp = tcls // N_PAIRS
    pair = tcls % N_PAIRS
    tile_ea = grp * EXPERTS_PER_GROUP + _lookup(jnp.asarray(_PAIR_A, jnp.int32), pair)
    tile_eb = grp * EXPERTS_PER_GROUP + _lookup(jnp.asarray(_PAIR_B, jnp.int32), pair)
    one = jnp.ones((1,), jnp.int32)
    tile_newa = jnp.concatenate([one, (tile_ea[1:] != tile_ea[:-1]).astype(jnp.int32)])
    tile_newb = jnp.concatenate([one, (tile_eb[1:] != tile_eb[:-1]).astype(jnp.int32)])
    return dest, src, (tile_ea, tile_eb, tile_valid, tile_newa, tile_newb)


def _residual_kernel(x_ref, y_ref, gate_ref, fg_ref, xo_ref, *, final_norm):
    xn = x_ref[...] + gate_ref[...] * y_ref[...]
    if final_norm:
        ms = jnp.mean(xn * xn, axis=-1, keepdims=True)
        xn = xn * lax.rsqrt(ms + NORM_EPS) * fg_ref[...]
    xo_ref[...] = xn


def _residual(x, y_all, gate, fg, per_row, seq_len, row_block0, final_norm):
    rows = x.shape[0]
    tm = _row_tile(rows)
    tps = max(seq_len // tm, 1)
    return pl.pallas_call(
        functools.partial(_residual_kernel, final_norm=final_norm),
        out_shape=jax.ShapeDtypeStruct((rows, D), F32),
        grid=(rows // tm,),
        in_specs=[pl.BlockSpec((tm, D), lambda i: (i, 0)),
                  pl.BlockSpec((tm, D), lambda i: (i + row_block0, 0)),
                  _mod_spec(tm, per_row, tps, 1),
                  pl.BlockSpec((1, D), lambda i: (0, 0))],
        out_specs=pl.BlockSpec((tm, D), lambda i: (i, 0)),
        compiler_params=_cparams(("arbitrary",)),
        name="residual",
    )(x, y_all, gate, fg)


def kernel(x_prompt, x_sample, c_prompt, c_sample, state_a_conv, state_a_h, cache_b_k0, cache_b_v0, cache_b_k1, cache_b_v1, cache_b_k2, cache_b_v2, state_c_conv, norm_mix_g, norm_ffn_g, ada_w, ada_b, final_norm_g, a_w_in_y, a_w_in_x, a_conv_w, a_conv_b, a_gate_r_w, a_gate_r_b, a_gate_i_w, a_gate_i_b, a_lambda, a_w_out, b_w_qkv, b_w_o, conf_w_pw1, conf_b_pw1, conf_w_dw, conf_b_dw, conf_ln_g, conf_ln_b, conf_w_pw2, conf_b_pw2, moe_w_grouter, moe_b_grouter, moe_w_erouter, moe_b_erouter, moe_w_gate, moe_w_up, moe_w_down):
    bsz, seq, _ = x_prompt.shape
    dbsz = x_sample.shape[0]
    depth = ada_w.shape[0]
    rows_p = bsz * seq
    rows_all = rows_p + dbsz
    n_tiles = rows_all // MOE_TILE + N_CLASSES
    caches_k = (cache_b_k0, cache_b_k1, cache_b_k2)
    caches_v = (cache_b_v0, cache_b_v1, cache_b_v2)

    c_rows = -(-(bsz + dbsz) // SUBLANES) * SUBLANES
    c_all = jnp.concatenate([c_prompt, c_sample, jnp.zeros((c_rows - bsz - dbsz, D), F32)], axis=0)
    mods = _ada_mod(c_all, ada_w, ada_b)

    cos_all, sin_all = _rope_table(seq + SUBLANES)
    cos_p, sin_p = cos_all[:seq], sin_all[:seq]
    cos_s = jnp.broadcast_to(cos_all[PAST_LEN:PAST_LEN + 1], (dbsz, HEAD_DIM))
    sin_s = jnp.broadcast_to(sin_all[PAST_LEN:PAST_LEN + 1], (dbsz, HEAD_DIM))

    bf = lambda w: w.astype(BF16)
    a_w_in_y, a_w_in_x, a_w_out = bf(a_w_in_y), bf(a_w_in_x), bf(a_w_out)
    a_gate_r_w, a_gate_i_w = bf(a_gate_r_w), bf(a_gate_i_w)
    b_w_qkv, b_w_o = bf(b_w_qkv), bf(b_w_o)
    conf_w_pw1, conf_w_pw2 = bf(conf_w_pw1), bf(conf_w_pw2)

    xp = x_prompt.reshape(rows_p, D)
    xs = x_sample.reshape(dbsz, D)
    row1 = lambda v: v.reshape(1, -1)
    zero_bias = jnp.zeros((1, D), F32)

    a_conv_p, a_conv_s, a_h_p, a_h_s = [], [], [], []
    kp, vp, ksm, vsm = ([[] for _ in range(N_GROUPS)] for _ in range(4))
    cf_p, cf_s = [], []

    pend_p = pend_s = None
    for i in range(depth):
        kind, j = i % 3, i // 3
        mp = [mods[i, :bsz, k * D:(k + 1) * D].reshape(bsz, 1, D) for k in range(6)]
        ms = [mods[i, bsz:bsz + dbsz, k * D:(k + 1) * D] for k in range(6)]
        g_mix = row1(norm_mix_g[i])
        if kind == 0:
            wts = (a_conv_w[j], row1(a_conv_b[j]), a_gate_r_w[j], row1(a_gate_r_b[j]),
                   a_gate_i_w[j], row1(a_gate_i_b[j]), row1(a_lambda[j]), a_w_out[j])
            ybr, xin, xp = _proj_a(xp, g_mix, mp[0], mp[1], a_w_in_y[j], a_w_in_x[j], False, seq, pend_p)
            xp3, h_last = _rglru_prompt(xin.reshape(bsz, seq, D_RNN), ybr.reshape(bsz, seq, D_RNN),
                                        xp.reshape(bsz, seq, D), mp[2], *wts)
            xp = xp3.reshape(rows_p, D)
            a_conv_p.append(xin.reshape(bsz, seq, D_RNN)[:, seq - (LRU_CONV - 1):])
            a_h_p.append(h_last.reshape(bsz, D_RNN))
            ybr_s, xin_s, xs = _proj_a(xs, g_mix, ms[0], ms[1], a_w_in_y[j], a_w_in_x[j], True, 1, pend_s)
            hist = state_a_conv[j]
            xs, h_new = _rglru_step(xin_s, ybr_s, xs, ms[2], hist.reshape(dbsz, (LRU_CONV - 1) * D_RNN),
                                    state_a_h[j], *wts)
            a_conv_s.append(jnp.concatenate([hist[:, 1:], xin_s[:, None, :]], axis=1))
            a_h_s.append(h_new)
        elif kind == 1:
            qkv, xp = _proj_qkv(xp, g_mix, mp[0], mp[1], cos_p, sin_p, b_w_qkv[j], False, seq, pend_p)
            qkv3 = qkv.reshape(bsz, seq, QKV_WIDTH)
            for g, (win, dil) in enumerate(ATTN_GROUPS):
                keep = min(win, seq)
                kcol = (N_GROUPS + g) * D
                vcol = (2 * N_GROUPS + g) * D
                kp[g].append(qkv3[:, seq - keep:, kcol:kcol + D].reshape(bsz, keep, N_HEADS, HEAD_DIM))
                vp[g].append(qkv3[:, seq - keep:, vcol:vcol + D].reshape(bsz, keep, N_HEADS, HEAD_DIM))
            o_p = _attn_prompt(qkv3).reshape(rows_p, D)
            xp = _out_proj(o_p, b_w_o[j], zero_bias, xp, mp[2], False, seq)
            qkv_s, xs = _proj_qkv(xs, g_mix, ms[0], ms[1], cos_s, sin_s, b_w_qkv[j], True, 1, pend_s)
            o_s = _attn_step(qkv_s, [c[j] for c in caches_k], [c[j] for c in caches_v])
            xs = _out_proj(o_s, b_w_o[j], zero_bias, xs, ms[2], True, 1)
            for g in range(N_GROUPS):
                kcol = (N_GROUPS + g) * D
                vcol = (2 * N_GROUPS + g) * D
                ksm[g].append(qkv_s[:, kcol:kcol + D].reshape(dbsz, 1, N_HEADS, HEAD_DIM))
                vsm[g].append(qkv_s[:, vcol:vcol + D].reshape(dbsz, 1, N_HEADS, HEAD_DIM))
        else:
            wts = (conf_w_dw[j], row1(conf_b_dw[j]), row1(conf_ln_g[j]), row1(conf_ln_b[j]),
                   conf_w_pw2[j], row1(conf_b_pw2[j]))
            b1 = row1(conf_b_pw1[j])
            glu, xp = _proj_glu(xp, g_mix, mp[0], mp[1], conf_w_pw1[j], b1, False, seq, pend_p)
            glu3 = glu.reshape(bsz, seq, D)
            xp = _conf_prompt(glu3, xp.reshape(bsz, seq, D), mp[2], *wts).reshape(rows_p, D)
            cf_p.append(glu3[:, seq - (CONF_WIDTH - 1):])
            glu_s, xs = _proj_glu(xs, g_mix, ms[0], ms[1], conf_w_pw1[j], b1, True, 1, pend_s)
            hist = state_c_conv[j]
            xs = _conf_step(glu_s, hist.reshape(dbsz, (CONF_WIDTH - 1) * D), xs, ms[2], *wts)
            cf_s.append(jnp.concatenate([hist[:, 1:], glu_s[:, None, :]], axis=1))

        g_ffn = row1(norm_ffn_g[i])
        wr = jnp.concatenate([moe_w_grouter[i], moe_w_erouter[i],
                              jnp.zeros((D, LANES - N_EGROUPS - N_EXPERTS), F32)], axis=1)
        wr_hi = wr.astype(BF16)
        wr = jnp.stack([wr_hi, (wr - wr_hi.astype(F32)).astype(BF16)])
        br = jnp.concatenate([moe_b_grouter[i], moe_b_erouter[i],
                              jnp.zeros((LANES - N_EGROUPS - N_EXPERTS,), F32)]).reshape(1, LANES)
        joint = _moe_route(xp, g_ffn, mp[3], mp[4], wr, br, False, seq, rows_all, 0, None)
        ui_all, counts = _moe_route(xs, g_ffn, ms[3], ms[4], wr, br, True, 1, rows_all, rows_p // dbsz, joint)
        dest, src, tiles = _moe_plan(ui_all, counts, n_tiles)
        ui_sorted = jnp.take(ui_all, src, axis=0, mode="clip")
        y_sorted = _moe_experts(i, *tiles, ui_sorted, moe_w_gate, moe_w_up, moe_w_down)
        y_all = jnp.take(y_sorted, dest, axis=0, mode="clip")
        if i < depth - 1:
            pend_p = (y_all, mp[5], 0)
            pend_s = (y_all, ms[5], rows_p // dbsz)
        else:
            fg = row1(final_norm_g)
            xp = _residual(xp, y_all, mp[5], fg, False, seq, 0, True)
            xs = _residual(xs, y_all, ms[5], fg, True, 1, rows_p // dbsz, True)

    y_prompt = xp.reshape(bsz, seq, D)
    y_sample = xs.reshape(dbsz, 1, D)
    return (y_prompt, y_sample,
            jnp.stack(a_conv_p), jnp.stack(a_conv_s), jnp.stack(a_h_p), jnp.stack(a_h_s),
            jnp.stack(kp[0]), jnp.stack(ksm[0]), jnp.stack(vp[0]), jnp.stack(vsm[0]),
            jnp.stack(kp[1]), jnp.stack(ksm[1]), jnp.stack(vp[1]), jnp.stack(vsm[1]),
            jnp.stack(kp[2]), jnp.stack(ksm[2]), jnp.stack(vp[2]), jnp.stack(vsm[2]),
            jnp.stack(cf_p), jnp.stack(cf_s))
```

```python
import functools
import math

import jax
import jax.numpy as jnp
from jax import lax
from jax.experimental import pallas as pl
from jax.experimental.pallas import tpu as pltpu

F32 = jnp.float32
BF16 = jnp.bfloat16
HIGHEST = lax.Precision.HIGHEST

D = 1024
D_RNN = 1280
LRU_BLOCKS = 10
LRU_BLOCK = 128
LRU_CONV = 4
LRU_C = 8.0
ATTN_GROUPS = ((128, 1), (512, 4), (2048, 16))
N_GROUPS = 3
N_HEADS = 8
HEAD_DIM = 128
ROT_DIM = 32
ROPE_THETA = 500000.0
QKV_WIDTH = 9 * D
CONF_WIDTH = 31
N_EGROUPS = 4
EXPERTS_PER_GROUP = 4
N_EXPERTS = 16
D_EXPERT = 512
N_PAIRS = 6
N_CLASSES = N_EGROUPS * N_PAIRS
NORM_EPS = 1e-6
PAST_LEN = 8192

LANES = 128
SUBLANES = 8
ATT_BLOCK = 128
ATT_TOKENS = 2048
QKV_CHUNK = 256
ATT_BATCH = 16
MOE_TILE = 256
VMEM_LIMIT = 56 * 1024 * 1024


def _cparams(sem):
    return pltpu.CompilerParams(dimension_semantics=sem, vmem_limit_bytes=VMEM_LIMIT)


def _sigmoid(x):
    return 1.0 / (1.0 + jnp.exp(-x))


def _silu(x):
    return x * _sigmoid(x)


def _gelu_tanh(x):
    return 0.5 * x * (1.0 + jnp.tanh(math.sqrt(2.0 / math.pi) * (x + 0.044715 * (x * x * x))))


def _softplus(x):
    return jnp.maximum(x, 0.0) + jnp.log1p(jnp.exp(-jnp.abs(x)))


def _bdot(a, b):
    return jnp.dot(a.astype(BF16), b.astype(BF16), preferred_element_type=F32)


def _norm_mod(x, g, shift, scale):
    ms = jnp.mean(x * x, axis=-1, keepdims=True)
    y = x * lax.rsqrt(ms + NORM_EPS) * g
    return y * (1.0 + scale) + shift


def _ada_kernel(c_ref, w_ref, b_ref, o_ref):
    c = c_ref[...]
    o_ref[...] = jnp.dot(_silu(c), w_ref[...], preferred_element_type=F32, precision=HIGHEST) + b_ref[...]


def _ada_mod(c_all, ada_w, ada_b):
    depth, _, n6 = ada_w.shape
    rows = c_all.shape[0]
    tn = 3072
    return pl.pallas_call(
        _ada_kernel,
        out_shape=jax.ShapeDtypeStruct((depth, rows, n6), F32),
        grid=(depth, n6 // tn),
        in_specs=[pl.BlockSpec((rows, D), lambda l, j: (0, 0)),
                  pl.BlockSpec((None, D, tn), lambda l, j: (l, 0, j)),
                  pl.BlockSpec((None, 1, tn), lambda l, j: (l, 0, j))],
        out_specs=pl.BlockSpec((None, rows, tn), lambda l, j: (l, 0, j)),
        compiler_params=_cparams(("arbitrary", "arbitrary")),
        name="ada_mod",
    )(c_all, ada_w, ada_b.reshape(depth, 1, n6))


def _mod_spec(tm, per_row, tiles_per_seq, ngrid):
    if per_row:
        if ngrid == 1:
            return pl.BlockSpec((tm, D), lambda i: (i, 0))
        return pl.BlockSpec((tm, D), lambda i, j: (i, 0))
    if ngrid == 1:
        return pl.BlockSpec((None, 1, D), lambda i: (i // tiles_per_seq, 0, 0))
    return pl.BlockSpec((None, 1, D), lambda i, j: (i // tiles_per_seq, 0, 0))


def _row_tile(rows):
    return 1024 if rows % 1024 == 0 else rows


def _load_x(x_ref, pend_refs, xn_ref):
    if pend_refs is None:
        return x_ref[...]
    y_ref, gate_ref = pend_refs
    x = x_ref[...] + gate_ref[...] * y_ref[...]
    xn_ref[...] = x
    return x


def _pend_specs(pending, tm, per_row, tps, ngrid):
    if pending is None:
        return [], [], [], []
    y_all, gate, row_block0 = pending
    if ngrid == 1:
        yspec = pl.BlockSpec((tm, D), lambda i: (i + row_block0, 0))
        ospec = pl.BlockSpec((tm, D), lambda i: (i, 0))
    else:
        yspec = pl.BlockSpec((tm, D), lambda i, j: (i + row_block0, 0))
        ospec = pl.BlockSpec((tm, D), lambda i, j: (i, 0))
    return [yspec, _mod_spec(tm, per_row, tps, ngrid)], [y_all, gate], [ospec], [D]


def _proj_a_kernel(*refs, fuse):
    x_ref, refs = refs[0], refs[1:]
    pend, refs = (refs[:2], refs[2:]) if fuse else (None, refs)
    g_ref, sh_ref, sc_ref, wy_ref, wx_ref, y_ref, xi_ref = refs[:7]
    x = _load_x(x_ref, pend, refs[7] if fuse else None)
    u = _norm_mod(x, g_ref[...], sh_ref[...], sc_ref[...]).astype(BF16)
    y_ref[...] = _gelu_tanh(jnp.dot(u, wy_ref[...], preferred_element_type=F32))
    xi_ref[...] = jnp.dot(u, wx_ref[...], preferred_element_type=F32)


def _proj_a(x, g, shift, scale, w_y, w_x, per_row, seq_len, pending):
    rows = x.shape[0]
    tm = 512 if rows % 512 == 0 else rows
    tps = max(seq_len // tm, 1)
    mod = _mod_spec(tm, per_row, tps, 1)
    wspec = pl.BlockSpec((D, D_RNN), lambda i: (0, 0))
    ospec = pl.BlockSpec((tm, D_RNN), lambda i: (i, 0))
    pspecs, pargs, pout, pwidth = _pend_specs(pending, tm, per_row, tps, 1)
    outs = pl.pallas_call(
        functools.partial(_proj_a_kernel, fuse=pending is not None),
        out_shape=tuple(jax.ShapeDtypeStruct((rows, n), F32) for n in [D_RNN, D_RNN] + pwidth),
        grid=(rows // tm,),
        in_specs=[pl.BlockSpec((tm, D), lambda i: (i, 0))] + pspecs + [
            pl.BlockSpec((1, D), lambda i: (0, 0)), mod, mod, wspec, wspec],
        out_specs=tuple([ospec, ospec] + pout),
        compiler_params=_cparams(("arbitrary",)),
        name="proj_a",
    )(x, *pargs, g, shift, scale, w_y, w_x)
    return outs[0], outs[1], (outs[2] if pending is not None else x)


def _scan_rows(a, b, h0):
    n, lanes = a.shape
    groups = n // SUBLANES
    a3 = a.reshape(groups, SUBLANES, lanes)
    b3 = b.reshape(groups, SUBLANES, lanes)
    sub = lax.broadcasted_iota(jnp.int32, a3.shape, 1)
    d = 1
    while d < SUBLANES:
        keep = sub >= d
        a_sh = jnp.where(keep, pltpu.roll(a3, d, 1), 1.0)
        b_sh = jnp.where(keep, pltpu.roll(b3, d, 1), 0.0)
        b3 = a3 * b_sh + b3
        a3 = a3 * a_sh
        d *= 2
    hs = []
    h_prev = h0
    for j in range(groups):
        hj = a3[j] * h_prev + b3[j]
        hs.append(hj)
        h_prev = hj[SUBLANES - 1:SUBLANES, :]
    return jnp.concatenate(hs, axis=0), h_prev


def _lru_gates(xc, grw, grb, giw, gib, lam):
    xcb = xc.astype(BF16)
    r = _sigmoid(jnp.dot(xcb, grw, preferred_element_type=F32) + grb)
    i = _sigmoid(jnp.dot(xcb, giw, preferred_element_type=F32) + gib)
    log_a = -LRU_C * r * _softplus(-lam)
    a = jnp.exp(log_a)
    b = jnp.sqrt(jnp.tanh(-log_a) * (a * a + 1.0)) * (i * xc)
    return a, b


def _rglru_kernel(xin_ref, ybr_ref, x_ref, gate_ref, cw_ref, cb_ref, grw_ref, grb_ref, giw_ref, gib_ref,
                  lam_ref, wo_ref, xo_ref, hl_ref, xx_ref, hc_ref, hy_ref, *, tt):
    t = pl.program_id(1)

    @pl.when(t == 0)
    def _():
        xx_ref[0:SUBLANES, :] = jnp.zeros((SUBLANES, D_RNN), F32)
        hc_ref[...] = jnp.zeros((1, D_RNN), F32)

    @pl.when(t > 0)
    def _():
        xx_ref[0:SUBLANES, :] = xx_ref[tt:tt + SUBLANES, :]

    xx_ref[SUBLANES:tt + SUBLANES, :] = xin_ref[...]
    for n in range(LRU_BLOCKS):
        sl = slice(n * LRU_BLOCK, (n + 1) * LRU_BLOCK)
        xc = cb_ref[:, sl]
        for k in range(LRU_CONV):
            off = SUBLANES - (LRU_CONV - 1) + k
            xc = xc + cw_ref[k:k + 1, sl] * xx_ref[off:off + tt, sl]
        a, bt = _lru_gates(xc, grw_ref[n], grb_ref[:, sl], giw_ref[n], gib_ref[:, sl], lam_ref[:, sl])
        h, h_last = _scan_rows(a, bt, hc_ref[:, sl])
        hc_ref[:, sl] = h_last
        hy_ref[:, sl] = (h * ybr_ref[:, sl]).astype(BF16)
    out = jnp.dot(hy_ref[...], wo_ref[...], preferred_element_type=F32)
    xo_ref[...] = x_ref[...] + gate_ref[...] * out
    hl_ref[...] = hc_ref[...]


def _rglru_prompt(xin, ybr, x, gate, cw, cb, grw, grb, giw, gib, lam, wo):
    bsz, seq, _ = xin.shape
    tt = 512
    full2 = lambda b, t: (0, 0)
    full3 = lambda b, t: (0, 0, 0)
    return pl.pallas_call(
        functools.partial(_rglru_kernel, tt=tt),
        out_shape=(jax.ShapeDtypeStruct((bsz, seq, D), F32), jax.ShapeDtypeStruct((bsz, 1, D_RNN), F32)),
        grid=(bsz, seq // tt),
        in_specs=[pl.BlockSpec((None, tt, D_RNN), lambda b, t: (b, t, 0)),
                  pl.BlockSpec((None, tt, D_RNN), lambda b, t: (b, t, 0)),
                  pl.BlockSpec((None, tt, D), lambda b, t: (b, t, 0)),
                  pl.BlockSpec((None, 1, D), lambda b, t: (b, 0, 0)),
                  pl.BlockSpec((LRU_CONV, D_RNN), full2),
                  pl.BlockSpec((1, D_RNN), full2),
                  pl.BlockSpec((LRU_BLOCKS, LRU_BLOCK, LRU_BLOCK), full3),
                  pl.BlockSpec((1, D_RNN), full2),
                  pl.BlockSpec((LRU_BLOCKS, LRU_BLOCK, LRU_BLOCK), full3),
                  pl.BlockSpec((1, D_RNN), full2),
                  pl.BlockSpec((1, D_RNN), full2),
                  pl.BlockSpec((D_RNN, D), full2)],
        out_specs=(pl.BlockSpec((None, tt, D), lambda b, t: (b, t, 0)),
                   pl.BlockSpec((None, 1, D_RNN), lambda b, t: (b, 0, 0))),
        scratch_shapes=[pltpu.VMEM((tt + SUBLANES, D_RNN), F32),
                        pltpu.VMEM((1, D_RNN), F32),
                        pltpu.VMEM((tt, D_RNN), BF16)],
        compiler_params=_cparams(("arbitrary", "arbitrary")),
        name="rglru_prompt",
    )(xin, ybr, x, gate, cw, cb, grw, grb, giw, gib, lam, wo)


def _rglru_step_kernel(xin_ref, ybr_ref, x_ref, gate_ref, hist_ref, hprev_ref, cw_ref, cb_ref, grw_ref, grb_ref,
                       giw_ref, gib_ref, lam_ref, wo_ref, xo_ref, hn_ref, hy_ref):
    for n in range(LRU_BLOCKS):
        sl = slice(n * LRU_BLOCK, (n + 1) * LRU_BLOCK)
        xc = cb_ref[:, sl] + cw_ref[LRU_CONV - 1:LRU_CONV, sl] * xin_ref[:, sl]
        for k in range(LRU_CONV - 1):
            xc = xc + cw_ref[k:k + 1, sl] * hist_ref[:, k * D_RNN + n * LRU_BLOCK:k * D_RNN + (n + 1) * LRU_BLOCK]
        a, bt = _lru_gates(xc, grw_ref[n], grb_ref[:, sl], giw_ref[n], gib_ref[:, sl], lam_ref[:, sl])
        h = a * hprev_ref[:, sl] + bt
        hn_ref[:, sl] = h
        hy_ref[:, sl] = (h * ybr_ref[:, sl]).astype(BF16)
    out = jnp.dot(hy_ref[...], wo_ref[...], preferred_element_type=F32)
    xo_ref[...] = x_ref[...] + gate_ref[...] * out


def _rglru_step(xin, ybr, x, gate, hist, hprev, cw, cb, grw, grb, giw, gib, lam, wo):
    rows = x.shape[0]
    return pl.pallas_call(
        _rglru_step_kernel,
        out_shape=(jax.ShapeDtypeStruct((rows, D), F32), jax.ShapeDtypeStruct((rows, D_RNN), F32)),
        scratch_shapes=[pltpu.VMEM((rows, D_RNN), BF16)],
        compiler_params=pltpu.CompilerParams(vmem_limit_bytes=VMEM_LIMIT),
        name="rglru_step",
    )(xin, ybr, x, gate, hist, hprev, cw, cb, grw, grb, giw, gib, lam, wo)


def _rope_table_kernel(cos_ref, sin_ref):
    half = ROT_DIM // 2
    shape = cos_ref.shape
    pos = lax.broadcasted_iota(jnp.int32, shape, 0).astype(F32)
    lane = lax.broadcasted_iota(jnp.int32, shape, 1)
    fidx = jnp.where(lane < half, lane, lane - half).astype(F32)
    inv_freq = jnp.exp(fidx * (-2.0 * math.log(ROPE_THETA) / ROT_DIM))
    ang = pos * inv_freq
    rot = lane < ROT_DIM
    cos_ref[...] = jnp.where(rot, jnp.cos(ang), 1.0)
    sin_ref[...] = jnp.where(rot, jnp.where(lane < half, -jnp.sin(ang), jnp.sin(ang)), 0.0)


def _rope_table(n_pos):
    return pl.pallas_call(
        _rope_table_kernel,
        out_shape=(jax.ShapeDtypeStruct((n_pos, HEAD_DIM), F32), jax.ShapeDtypeStruct((n_pos, HEAD_DIM), F32)),
        compiler_params=pltpu.CompilerParams(vmem_limit_bytes=VMEM_LIMIT),
        name="rope_table",
    )()


def _proj_qkv_kernel(*refs, tn, fuse):
    x_ref, refs = refs[0], refs[1:]
    pend, refs = (refs[:2], refs[2:]) if fuse else (None, refs)
    g_ref, sh_ref, sc_ref, cos_ref, sin_ref, w_ref, o_ref = refs[:7]
    u_ref = refs[-1]
    j = pl.program_id(1)

    @pl.when(j == 0)
    def _():
        x = _load_x(x_ref, pend, refs[7] if fuse else None)
        u_ref[...] = _norm_mod(x, g_ref[...], sh_ref[...], sc_ref[...]).astype(BF16)

    def chunk(c):
        return jnp.dot(u_ref[...], w_ref[:, c * QKV_CHUNK:(c + 1) * QKV_CHUNK], preferred_element_type=F32)

    @pl.when(j < 2 * N_GROUPS * D // tn)
    def _():
        half = ROT_DIM // 2
        qscale = jnp.where(j < N_GROUPS * D // tn, HEAD_DIM ** -0.5, 1.0).astype(F32)
        cos = cos_ref[...] * qscale
        sin = sin_ref[...] * qscale
        lane = lax.broadcasted_iota(jnp.int32, cos.shape, 1)
        for c in range(tn // QKV_CHUNK):
            acc = chunk(c)
            for h in range(QKV_CHUNK // HEAD_DIM):
                xh = acc[:, h * HEAD_DIM:(h + 1) * HEAD_DIM]
                partner = jnp.where(lane < half, pltpu.roll(xh, HEAD_DIM - half, 1), pltpu.roll(xh, half, 1))
                col = c * QKV_CHUNK + h * HEAD_DIM
                o_ref[:, col:col + HEAD_DIM] = xh * cos + partner * sin

    @pl.when(j >= 2 * N_GROUPS * D // tn)
    def _():
        for c in range(tn // QKV_CHUNK):
            o_ref[:, c * QKV_CHUNK:(c + 1) * QKV_CHUNK] = chunk(c)


def _proj_qkv(x, g, shift, scale, cos_t, sin_t, w, per_row, seq_len, pending):
    rows = x.shape[0]
    tm = _row_tile(rows)
    tn = N_GROUPS * D // 2
    tps = max(seq_len // tm, 1)
    mod = _mod_spec(tm, per_row, tps, 2)
    n_rope = cos_t.shape[0] // tm
    rope = pl.BlockSpec((tm, HEAD_DIM), lambda i, j: (i % n_rope, 0))
    pspecs, pargs, pout, pwidth = _pend_specs(pending, tm, per_row, tps, 2)
    outs = pl.pallas_call(
        functools.partial(_proj_qkv_kernel, tn=tn, fuse=pending is not None),
        out_shape=tuple(jax.ShapeDtypeStruct((rows, n), F32) for n in [QKV_WIDTH] + pwidth),
        grid=(rows // tm, QKV_WIDTH // tn),
        in_specs=[pl.BlockSpec((tm, D), lambda i, j: (i, 0))] + pspecs + [
            pl.BlockSpec((1, D), lambda i, j: (0, 0)),
            mod, mod, rope, rope,
            pl.BlockSpec((D, tn), lambda i, j: (0, j))],
        out_specs=tuple([pl.BlockSpec((tm, tn), lambda i, j: (i, j))] + pout),
        scratch_shapes=[pltpu.VMEM((tm, D), BF16)],
        compiler_params=_cparams(("arbitrary", "arbitrary")),
        name="proj_qkv",
    )(x, *pargs, g, shift, scale, cos_t, sin_t, w)
    return outs[0], (outs[1] if pending is not None else x)


def _rows(start, dil):
    if dil == 1:
        return pl.ds(start, ATT_BLOCK)
    return pl.ds(start, ATT_BLOCK, stride=dil)


def _attn_prompt_kernel(*refs):
    in_refs, o_ref, og_ref, lg_ref = refs[:15], refs[15], refs[16], refs[17]
    n = ATT_BLOCK
    nblk = ATT_BATCH
    ri = lax.broadcasted_iota(jnp.int32, (nblk, n, n), 1)
    cj = lax.broadcasted_iota(jnp.int32, (nblk, n, n), 2)
    neg = -jnp.inf
    has_prev = pl.program_id(1) > 0
    qk_dims = (((2,), (2,)), ((0,), (0,)))
    pv_dims = (((2,), (1,)), ((0,), (0,)))
    ones = jnp.ones((nblk, n, HEAD_DIM), BF16)
    for g, (win, dil) in enumerate(ATTN_GROUPS):
        q_ref, kc_ref, vc_ref, kp_ref, vp_ref = in_refs[5 * g:5 * g + 5]
        all_blocks = [(s, r) for s in range(ATT_TOKENS // win) for r in range(dil)]
        for b0 in range(0, len(all_blocks), nblk):
            blocks = all_blocks[b0:b0 + nblk]

            def load(ref, s, r):
                return ref[_rows(s * win + r, dil), :].astype(BF16)

            q = jnp.stack([load(q_ref, s, r) for s, r in blocks])
            kc = jnp.stack([load(kc_ref, s, r) for s, r in blocks])
            vc = jnp.stack([load(vc_ref, s, r) for s, r in blocks])
            kp = jnp.stack([load(kc_ref, s - 1, r) if s > 0 else kp_ref[_rows(r, dil), :].astype(BF16)
                            for s, r in blocks])
            vp = jnp.stack([load(vc_ref, s - 1, r) if s > 0 else vp_ref[_rows(r, dil), :].astype(BF16)
                            for s, r in blocks])
            sp = lax.dot_general(q, kp, qk_dims, preferred_element_type=F32)
            sc = lax.dot_general(q, kc, qk_dims, preferred_element_type=F32)
            if blocks[0][0] == 0:
                n_first = sum(1 for s, _ in blocks if s == 0)
                blk = lax.broadcasted_iota(jnp.int32, (nblk, n, n), 0)
                sp = jnp.where(blk >= jnp.where(has_prev, 0, n_first), sp, neg)
            sp = jnp.where(cj >= ri, sp, neg)
            sc = jnp.where(cj <= ri, sc, neg)
            m = jnp.max(jnp.maximum(sp, sc), axis=-1, keepdims=True)
            ep = jnp.exp(sp - m).astype(BF16)
            ec = jnp.exp(sc - m).astype(BF16)
            l = (lax.dot_general(ep, ones, pv_dims, preferred_element_type=F32)
                 + lax.dot_general(ec, ones, pv_dims, preferred_element_type=F32))
            o = (lax.dot_general(ep, vp, pv_dims, preferred_element_type=F32)
                 + lax.dot_general(ec, vc, pv_dims, preferred_element_type=F32)) / l
            lse = m + jnp.log(l)
            for idx, (s, r) in enumerate(blocks):
                rows = _rows(s * win + r, dil)
                og_ref[g, rows, :] = o[idx]
                lg_ref[g, rows, :] = lse[idx]
    l0, l1, l2 = lg_ref[0], lg_ref[1], lg_ref[2]
    m = jnp.maximum(jnp.maximum(l0, l1), l2)
    e0, e1, e2 = jnp.exp(l0 - m), jnp.exp(l1 - m), jnp.exp(l2 - m)
    o = (e0 * og_ref[0] + e1 * og_ref[1] + e2 * og_ref[2]) / (e0 + e1 + e2)
    o_ref[...] = o.astype(o_ref.dtype)


def _attn_prompt(qkv):
    bsz, seq, _ = qkv.shape
    tb = ATT_TOKENS
    specs = []
    for g, (win, dil) in enumerate(ATTN_GROUPS):
        per = tb // win

        def cur(which, g=g):
            return pl.BlockSpec((None, tb, HEAD_DIM),
                                lambda b, i, h: (b, i, (which * N_GROUPS + g) * N_HEADS + h))

        def prev(which, g=g, win=win, per=per):
            return pl.BlockSpec((None, win, HEAD_DIM),
                                lambda b, i, h: (b, jnp.maximum(i * per - 1, 0), (which * N_GROUPS + g) * N_HEADS + h))

        specs += [cur(0), cur(1), cur(2), prev(1), prev(2)]
    return pl.pallas_call(
        _attn_prompt_kernel,
        out_shape=jax.ShapeDtypeStruct((bsz, seq, D), BF16),
        grid=(bsz, seq // tb, N_HEADS),
        in_specs=specs,
        out_specs=pl.BlockSpec((None, tb, HEAD_DIM), lambda b, i, h: (b, i, h)),
        scratch_shapes=[pltpu.VMEM((N_GROUPS, tb, HEAD_DIM), F32),
                        pltpu.VMEM((N_GROUPS, tb, HEAD_DIM), F32)],
        compiler_params=_cparams(("arbitrary", "arbitrary", "arbitrary")),
        name="attn_prompt",
    )(*([qkv] * 15))


def _attn_step_kernel(*refs):
    q_refs, kn_refs, vn_refs = refs[0:3], refs[3:6], refs[6:9]
    kc_refs, vc_refs = refs[9:12], refs[12:15]
    o_ref = refs[15]
    outs, lses = [], []
    for g in range(N_GROUPS):
        q = q_refs[g][...]
        kn = kn_refs[g][...]
        vn = vn_refs[g][...]
        s_c = jnp.sum(kc_refs[g][...] * q[None], axis=-1, keepdims=True)
        s_n = jnp.sum(kn * q, axis=-1, keepdims=True)
        m = jnp.maximum(jnp.max(s_c, axis=0), s_n)
        e_c = jnp.exp(s_c - m[None])
        e_n = jnp.exp(s_n - m)
        l = jnp.sum(e_c, axis=0) + e_n
        outs.append((jnp.sum(e_c * vc_refs[g][...], axis=0) + e_n * vn) / l)
        lses.append(m + jnp.log(l))
    m = jnp.maximum(jnp.maximum(lses[0], lses[1]), lses[2])
    es = [jnp.exp(x - m) for x in lses]
    inv = 1.0 / (es[0] + es[1] + es[2])
    o_ref[...] = (es[0] * outs[0] + es[1] * outs[1] + es[2] * outs[2]) * inv


def _attn_step(qkv_s, caches_k, caches_v):
    bsz = qkv_s.shape[0]
    q4 = qkv_s.reshape(bsz, 9, N_HEADS, HEAD_DIM)

    def new_spec(which, g):
        return pl.BlockSpec((None, None, N_HEADS, HEAD_DIM), lambda b: (b, which * N_GROUPS + g, 0, 0))

    specs = [new_spec(w, g) for w in range(3) for g in range(N_GROUPS)]
    cache_spec = pl.BlockSpec((None, ATT_BLOCK, N_HEADS, HEAD_DIM), lambda b: (b, 0, 0, 0))
    ks, vs = [], []
    for g, (win, dil) in enumerate(ATTN_GROUPS):
        ks.append(caches_k[g].reshape(bsz, win // dil, dil * N_HEADS, HEAD_DIM))
        vs.append(caches_v[g].reshape(bsz, win // dil, dil * N_HEADS, HEAD_DIM))
    o = pl.pallas_call(
        _attn_step_kernel,
        out_shape=jax.ShapeDtypeStruct((bsz, N_HEADS, HEAD_DIM), F32),
        grid=(bsz,),
        in_specs=specs + [cache_spec] * 6,
        out_specs=pl.BlockSpec((None, N_HEADS, HEAD_DIM), lambda b: (b, 0, 0)),
        compiler_params=_cparams(("arbitrary",)),
        name="attn_step",
    )(*([q4] * 9), *ks, *vs)
    return o.reshape(bsz, D)


def _out_proj_kernel(a_ref, w_ref, b_ref, x_ref, gate_ref, xo_ref):
    out = jnp.dot(a_ref[...].astype(BF16), w_ref[...], preferred_element_type=F32) + b_ref[...]
    xo_ref[...] = x_ref[...] + gate_ref[...] * out


def _out_proj(a, w, bias, x, gate, per_row, seq_len):
    rows, k = a.shape
    tm = _row_tile(rows)
    tps = max(seq_len // tm, 1)
    return pl.pallas_call(
        _out_proj_kernel,
        out_shape=jax.ShapeDtypeStruct((rows, D), F32),
        grid=(rows // tm,),
        in_specs=[pl.BlockSpec((tm, k), lambda i: (i, 0)),
                  pl.BlockSpec((k, D), lambda i: (0, 0)),
                  pl.BlockSpec((1, D), lambda i: (0, 0)),
                  pl.BlockSpec((tm, D), lambda i: (i, 0)),
                  _mod_spec(tm, per_row, tps, 1)],
        out_specs=pl.BlockSpec((tm, D), lambda i: (i, 0)),
        compiler_params=_cparams(("arbitrary",)),
        name="out_proj",
    )(a, w, bias, x, gate)


def _proj_glu_kernel(*refs, fuse):
    x_ref, refs = refs[0], refs[1:]
    pend, refs = (refs[:2], refs[2:]) if fuse else (None, refs)
    g_ref, sh_ref, sc_ref, wa_ref, wb_ref, ba_ref, bb_ref, o_ref = refs[:8]
    x = _load_x(x_ref, pend, refs[8] if fuse else None)
    u = _norm_mod(x, g_ref[...], sh_ref[...], sc_ref[...]).astype(BF16)
    za = jnp.dot(u, wa_ref[...], preferred_element_type=F32) + ba_ref[...]
    zb = jnp.dot(u, wb_ref[...], preferred_element_type=F32) + bb_ref[...]
    o_ref[...] = za * _sigmoid(zb)


def _proj_glu(x, g, shift, scale, w, bias, per_row, seq_len, pending):
    rows = x.shape[0]
    tm = _row_tile(rows)
    tps = max(seq_len // tm, 1)
    mod = _mod_spec(tm, per_row, tps, 1)
    pspecs, pargs, pout, pwidth = _pend_specs(pending, tm, per_row, tps, 1)
    outs = pl.pallas_call(
        functools.partial(_proj_glu_kernel, fuse=pending is not None),
        out_shape=tuple(jax.ShapeDtypeStruct((rows, n), F32) for n in [D] + pwidth),
        grid=(rows // tm,),
        in_specs=[pl.BlockSpec((tm, D), lambda i: (i, 0))] + pspecs + [
            pl.BlockSpec((1, D), lambda i: (0, 0)),
            mod, mod,
            pl.BlockSpec((D, D), lambda i: (0, 0)),
            pl.BlockSpec((D, D), lambda i: (0, 1)),
            pl.BlockSpec((1, D), lambda i: (0, 0)),
            pl.BlockSpec((1, D), lambda i: (0, 1))],
        out_specs=tuple([pl.BlockSpec((tm, D), lambda i: (i, 0))] + pout),
        compiler_params=_cparams(("arbitrary",)),
        name="proj_glu",
    )(x, *pargs, g, shift, scale, w, w, bias, bias)
    return outs[0], (outs[1] if pending is not None else x)


def _layer_norm(y, g, b):
    mu = jnp.mean(y, axis=-1, keepdims=True)
    yc = y - mu
    var = jnp.mean(yc * yc, axis=-1, keepdims=True)
    return yc * lax.rsqrt(var + NORM_EPS) * g + b


CONF_HALO = 32
CONF_CHUNK = 128


def _conf_kernel(glu_ref, x_ref, gate_ref, wdw_ref, bdw_ref, lg_ref, lb_ref, w2_ref, b2_ref, xo_ref,
                 xx_ref, z_ref, p_ref, *, tt):
    t = pl.program_id(1)

    @pl.when(t == 0)
    def _():
        xx_ref[0:CONF_HALO, :] = jnp.zeros((CONF_HALO, D), F32)

    @pl.when(t > 0)
    def _():
        xx_ref[0:CONF_HALO, :] = xx_ref[tt:tt + CONF_HALO, :]

    xx_ref[CONF_HALO:tt + CONF_HALO, :] = glu_ref[...]
    first = CONF_HALO - (CONF_WIDTH - 1)
    for r0 in range(0, tt, CONF_CHUNK):
        for c in range(D // LANES):
            sl = slice(c * LANES, (c + 1) * LANES)
            acc = jnp.broadcast_to(bdw_ref[:, sl], (CONF_CHUNK, LANES))
            for r in range(SUBLANES):
                taps = [k for k in range(CONF_WIDTH) if (first + k) % SUBLANES == r]
                span = CONF_CHUNK if r == 0 else CONF_CHUNK + SUBLANES
                p = None
                for k in taps:
                    base = r0 + first + k - r
                    term = wdw_ref[k:k + 1, sl] * xx_ref[base:base + span, sl]
                    p = term if p is None else p + term
                if r == 0:
                    acc = acc + p
                else:
                    p_ref[r, :, :] = p
                    acc = acc + p_ref[r, r:r + CONF_CHUNK, :]
            z_ref[r0:r0 + CONF_CHUNK, sl] = acc
    y = _silu(_layer_norm(z_ref[...], lg_ref[...], lb_ref[...]))
    out = jnp.dot(y.astype(BF16), w2_ref[...], preferred_element_type=F32) + b2_ref[...]
    xo_ref[...] = x_ref[...] + gate_ref[...] * out


def _conf_prompt(glu, x, gate, wdw, bdw, lg, lb, w2, b2):
    bsz, seq, _ = glu.shape
    tt = 512
    full2 = lambda b, t: (0, 0)
    vec = pl.BlockSpec((1, D), full2)
    return pl.pallas_call(
        functools.partial(_conf_kernel, tt=tt),
        out_shape=jax.ShapeDtypeStruct((bsz, seq, D), F32),
        grid=(bsz, seq // tt),
        in_specs=[pl.BlockSpec((None, tt, D), lambda b, t: (b, t, 0)),
                  pl.BlockSpec((None, tt, D), lambda b, t: (b, t, 0)),
                  pl.BlockSpec((None, 1, D), lambda b, t: (b, 0, 0)),
                  pl.BlockSpec((CONF_WIDTH, D), full2),
                  vec, vec, vec,
                  pl.BlockSpec((D, D), full2),
                  vec],
        out_specs=pl.BlockSpec((None, tt, D), lambda b, t: (b, t, 0)),
        scratch_shapes=[pltpu.VMEM((tt + CONF_HALO, D), F32),
                        pltpu.VMEM((tt, D), F32),
                        pltpu.VMEM((SUBLANES, CONF_CHUNK + SUBLANES, LANES), F32)],
        compiler_params=_cparams(("arbitrary", "arbitrary")),
        name="conf_prompt",
    )(glu, x, gate, wdw, bdw, lg, lb, w2, b2)


def _conf_step_kernel(glu_ref, hist_ref, x_ref, gate_ref, wdw_ref, bdw_ref, lg_ref, lb_ref, w2_ref, b2_ref, xo_ref):
    acc = bdw_ref[...] + wdw_ref[CONF_WIDTH - 1:CONF_WIDTH, :] * glu_ref[...]
    for k in range(CONF_WIDTH - 1):
        acc = acc + wdw_ref[k:k + 1, :] * hist_ref[:, k * D:(k + 1) * D]
    y = _silu(_layer_norm(acc, lg_ref[...], lb_ref[...]))
    out = jnp.dot(y.astype(BF16), w2_ref[...], preferred_element_type=F32) + b2_ref[...]
    xo_ref[...] = x_ref[...] + gate_ref[...] * out


def _conf_step(glu, hist, x, gate, wdw, bdw, lg, lb, w2, b2):
    return pl.pallas_call(
        _conf_step_kernel,
        out_shape=jax.ShapeDtypeStruct(x.shape, F32),
        compiler_params=pltpu.CompilerParams(vmem_limit_bytes=VMEM_LIMIT),
        name="conf_step",
    )(glu, hist, x, gate, wdw, bdw, lg, lb, w2, b2)


def _moe_route_kernel(x_ref, g_ref, sh_ref, sc_ref, wr_ref, br_ref, tri_ref, cnt0_ref, *rest):
    ui_ref, cnt_ref, carry_ref = rest[-3], rest[-2], rest[-1]

    @pl.when(pl.program_id(0) == 0)
    def _():
        carry_ref[...] = cnt0_ref[...]

    u = _norm_mod(x_ref[...], g_ref[...], sh_ref[...], sc_ref[...])
    ui_ref[:, :D] = u
    u_hi = u.astype(BF16)
    u_lo = (u - u_hi.astype(F32)).astype(BF16)
    logits = (jnp.dot(u_hi, wr_ref[0], preferred_element_type=F32)
              + jnp.dot(u_lo, wr_ref[0], preferred_element_type=F32)
              + jnp.dot(u_hi, wr_ref[1], preferred_element_type=F32)) + br_ref[...]
    lane = lax.broadcasted_iota(jnp.int32, logits.shape, 1)
    neg = -jnp.inf
    big = jnp.int32(LANES)
    gl = jnp.where(lane < N_EGROUPS, logits, neg)
    gmax = jnp.max(gl, axis=-1, keepdims=True)
    gsel = jnp.min(jnp.where(gl == gmax, lane, big), axis=-1, keepdims=True)
    gp = 1.0 / jnp.sum(jnp.exp(gl - gmax), axis=-1, keepdims=True)
    base = N_EGROUPS + EXPERTS_PER_GROUP * gsel
    el = jnp.where((lane >= base) & (lane < base + EXPERTS_PER_GROUP), logits, neg)
    v1 = jnp.max(el, axis=-1, keepdims=True)
    i1 = jnp.min(jnp.where(el == v1, lane, big), axis=-1, keepdims=True)
    el2 = jnp.where(lane == i1, neg, el)
    v2 = jnp.max(el2, axis=-1, keepdims=True)
    i2 = jnp.min(jnp.where(el2 == v2, lane, big), axis=-1, keepdims=True)
    t = jnp.exp(v2 - v1)
    w1 = gp / (1.0 + t)
    w2 = gp * t / (1.0 + t)
    e1 = i1 - base
    e2 = i2 - base
    lo = jnp.minimum(e1, e2)
    hi = jnp.maximum(e1, e2)
    w_lo = jnp.where(e1 < e2, w1, w2)
    w_hi = jnp.where(e1 < e2, w2, w1)
    pair = (lo * (7 - lo)) // 2 + hi - lo - 1
    pair = jnp.where(pair == 3, 4, jnp.where(pair == 4, 3, pair))
    w_a = jnp.where(pair == 5, w_hi, w_lo)
    w_b = jnp.where(pair == 5, w_lo, w_hi)
    cls = gsel * N_PAIRS + pair
    onehot = jnp.where(lane == cls, 1.0, 0.0)
    before = jnp.dot(tri_ref[...], onehot.astype(BF16), preferred_element_type=F32) + carry_ref[...]
    rank = jnp.sum(onehot * before, axis=-1, keepdims=True)
    carry_ref[...] = carry_ref[...] + jnp.sum(onehot, axis=0, keepdims=True)
    cnt_ref[...] = carry_ref[...]
    ui_ref[:, D:] = jnp.where(lane == 0, cls.astype(F32),
                              jnp.where(lane == 1, w_a, jnp.where(lane == 2, w_b, jnp.where(lane == 3, rank, 0.0))))


ROUTE_ROWS = 24


def _moe_route_t_kernel(x_ref, g_ref, sh_ref, sc_ref, wr_ref, br_ref, triu_ref, ui_ref, cnt_ref, carry_ref):
    @pl.when(pl.program_id(0) == 0)
    def _():
        carry_ref[...] = jnp.zeros(carry_ref.shape, F32)

    u = _norm_mod(x_ref[...], g_ref[...], sh_ref[...], sc_ref[...])
    ui_ref[:, :D] = u
    u_hi = u.astype(BF16)
    u_lo = (u - u_hi.astype(F32)).astype(BF16)
    logits = (jnp.dot(u_hi, wr_ref[0], preferred_element_type=F32)
              + jnp.dot(u_lo, wr_ref[0], preferred_element_type=F32)
              + jnp.dot(u_hi, wr_ref[1], preferred_element_type=F32)) + br_ref[...]
    lt = logits.T[:ROUTE_ROWS]
    tokens = lt.shape[1]
    row = lax.broadcasted_iota(jnp.int32, lt.shape, 0)
    neg = -jnp.inf
    big = jnp.int32(LANES)
    gl = jnp.where(row < N_EGROUPS, lt, neg)
    gmax = jnp.max(gl, axis=0, keepdims=True)
    gsel = jnp.min(jnp.where(gl == gmax, row, big), axis=0, keepdims=True)
    gp = 1.0 / jnp.sum(jnp.exp(gl - gmax), axis=0, keepdims=True)
    base = N_EGROUPS + EXPERTS_PER_GROUP * gsel
    el = jnp.where(row >= base, jnp.where(row < base + EXPERTS_PER_GROUP, lt, neg), neg)
    v1 = jnp.max(el, axis=0, keepdims=True)
    i1 = jnp.min(jnp.where(el == v1, row, big), axis=0, keepdims=True)
    el2 = jnp.where(row == i1, neg, el)
    v2 = jnp.max(el2, axis=0, keepdims=True)
    i2 = jnp.min(jnp.where(el2 == v2, row, big), axis=0, keepdims=True)
    t = jnp.exp(v2 - v1)
    w1 = gp / (1.0 + t)
    w2 = gp * t / (1.0 + t)
    e1 = i1 - base
    e2 = i2 - base
    lo = jnp.minimum(e1, e2)
    hi = jnp.maximum(e1, e2)
    w_lo = jnp.where(e1 < e2, w1, w2)
    w_hi = jnp.where(e1 < e2, w2, w1)
    pair = (lo * (7 - lo)) // 2 + hi - lo - 1
    pair = jnp.where(pair == 3, 4, jnp.where(pair == 4, 3, pair))
    w_a = jnp.where(pair == 5, w_hi, w_lo)
    w_b = jnp.where(pair == 5, w_lo, w_hi)
    cls = gsel * N_PAIRS + pair
    onehot = jnp.where(row == cls, 1.0, 0.0)
    before = jnp.dot(onehot.astype(BF16), triu_ref[...], preferred_element_type=F32) + carry_ref[...]
    rank = jnp.sum(onehot * before, axis=0, keepdims=True)
    carry_ref[...] = carry_ref[...] + jnp.sum(onehot, axis=1, keepdims=True)
    cnt_ref[...] = carry_ref[...]
    sub = lax.broadcasted_iota(jnp.int32, (SUBLANES, tokens), 0)
    rec = jnp.where(sub == 0, cls.astype(F32),
                    jnp.where(sub == 1, w_a, jnp.where(sub == 2, w_b, jnp.where(sub == 3, rank, 0.0))))
    rec = jnp.concatenate([rec, jnp.zeros((LANES - SUBLANES, tokens), F32)], axis=0)
    ui_ref[:, D:] = rec.T


def _moe_route_prompt(x, g, shift, scale, wr, br, seq_len, rows_all):
    rows = x.shape[0]
    tm = _row_tile(rows)
    mod = _mod_spec(tm, False, seq_len // tm, 1)
    r = lax.broadcasted_iota(jnp.int32, (tm, tm), 0)
    c = lax.broadcasted_iota(jnp.int32, (tm, tm), 1)
    triu = (r < c).astype(BF16)
    return pl.pallas_call(
        _moe_route_t_kernel,
        out_shape=(jax.ShapeDtypeStruct((rows_all, D + LANES), F32), jax.ShapeDtypeStruct((ROUTE_ROWS, 1), F32)),
        grid=(rows // tm,),
        in_specs=[pl.BlockSpec((tm, D), lambda i: (i, 0)),
                  pl.BlockSpec((1, D), lambda i: (0, 0)),
                  mod, mod,
                  pl.BlockSpec((2, D, LANES), lambda i: (0, 0, 0)),
                  pl.BlockSpec((1, LANES), lambda i: (0, 0)),
                  pl.BlockSpec((tm, tm), lambda i: (0, 0))],
        out_specs=(pl.BlockSpec((tm, D + LANES), lambda i: (i, 0)),
                   pl.BlockSpec((ROUTE_ROWS, 1), lambda i: (0, 0))),
        scratch_shapes=[pltpu.VMEM((ROUTE_ROWS, 1), F32)],
        compiler_params=_cparams(("arbitrary",)),
        name="moe_route_prompt",
    )(x, g, shift, scale, wr, br, triu)


def _moe_route(x, g, shift, scale, wr, br, per_row, seq_len, rows_all, row_block0, prev):
    rows = x.shape[0]
    tm = _row_tile(rows)
    tps = max(seq_len // tm, 1)
    mod = _mod_spec(tm, per_row, tps, 1)
    r = lax.broadcasted_iota(jnp.int32, (tm, tm), 0)
    c = lax.broadcasted_iota(jnp.int32, (tm, tm), 1)
    tri = (c < r).astype(BF16)
    cnt0 = jnp.zeros((1, LANES), F32) if prev is None else prev[1]
    in_specs = [pl.BlockSpec((tm, D), lambda i: (i, 0)),
                pl.BlockSpec((1, D), lambda i: (0, 0)),
                mod, mod,
                pl.BlockSpec((2, D, LANES), lambda i: (0, 0, 0)),
                pl.BlockSpec((1, LANES), lambda i: (0, 0)),
                pl.BlockSpec((tm, tm), lambda i: (0, 0)),
                pl.BlockSpec((1, LANES), lambda i: (0, 0))]
    args = [x, g, shift, scale, wr, br, tri, cnt0]
    aliases = {}
    if prev is not None:
        in_specs += [pl.BlockSpec(memory_space=pl.ANY)]
        args += [prev[0]]
        aliases = {8: 0}
    return pl.pallas_call(
        _moe_route_kernel,
        out_shape=(jax.ShapeDtypeStruct((rows_all, D + LANES), F32), jax.ShapeDtypeStruct((1, LANES), F32)),
        grid=(rows // tm,),
        in_specs=in_specs,
        out_specs=(pl.BlockSpec((tm, D + LANES), lambda i: (i + row_block0, 0)),
                   pl.BlockSpec((1, LANES), lambda i: (0, 0))),
        scratch_shapes=[pltpu.VMEM((1, LANES), F32)],
        input_output_aliases=aliases,
        compiler_params=_cparams(("arbitrary",)),
        name="moe_route",
    )(*args)


def _moe_expert_kernel(ea_ref, eb_ref, valid_ref, newa_ref, newb_ref, x_ref, wga_ref, wua_ref, wda_ref,
                       wgb_ref, wub_ref, wdb_ref, o_ref, ga_ref, ua_ref, da_ref, gb_ref, ub_ref, db_ref):
    t = pl.program_id(0)

    @pl.when(newa_ref[t] == 1)
    def _():
        ga_ref[...] = wga_ref[...].astype(BF16)
        ua_ref[...] = wua_ref[...].astype(BF16)
        da_ref[...] = wda_ref[...].astype(BF16)

    @pl.when(newb_ref[t] == 1)
    def _():
        gb_ref[...] = wgb_ref[...].astype(BF16)
        ub_ref[...] = wub_ref[...].astype(BF16)
        db_ref[...] = wdb_ref[...].astype(BF16)

    @pl.when(valid_ref[t] == 1)
    def _():
        x = x_ref[:, :D].astype(BF16)

        def expert(wg_ref, wu_ref, wd_ref, w):
            hg = jnp.dot(x, wg_ref[...], preferred_element_type=F32)
            hu = jnp.dot(x, wu_ref[...], preferred_element_type=F32)
            act = _silu(hg) * hu * w
            return jnp.dot(act.astype(BF16), wd_ref[...], preferred_element_type=F32)

        o_ref[...] = (expert(ga_ref, ua_ref, da_ref, x_ref[:, D + 1:D + 2])
                      + expert(gb_ref, ub_ref, db_ref, x_ref[:, D + 2:D + 3]))

    @pl.when(valid_ref[t] == 0)
    def _():
        o_ref[...] = jnp.zeros(o_ref.shape, F32)


def _moe_experts(layer, tile_ea, tile_eb, tile_valid, tile_newa, tile_newb, ui_sorted, w_gate, w_up, w_down):
    n_tiles = tile_ea.shape[0]

    def wspec(shape, which):
        if which == 0:
            return pl.BlockSpec((None, None) + shape, lambda t, ea, eb, va, na, nb: (layer, ea[t], 0, 0))
        return pl.BlockSpec((None, None) + shape, lambda t, ea, eb, va, na, nb: (layer, eb[t], 0, 0))

    up = (D, D_EXPERT)
    down = (D_EXPERT, D)
    grid_spec = pltpu.PrefetchScalarGridSpec(
        num_scalar_prefetch=5,
        grid=(n_tiles,),
        in_specs=[pl.BlockSpec((MOE_TILE, D + LANES), lambda t, ea, eb, va, na, nb: (t, 0)),
                  wspec(up, 0), wspec(up, 0), wspec(down, 0),
                  wspec(up, 1), wspec(up, 1), wspec(down, 1)],
        out_specs=pl.BlockSpec((MOE_TILE, D), lambda t, ea, eb, va, na, nb: (t, 0)),
        scratch_shapes=[pltpu.VMEM(up, BF16), pltpu.VMEM(up, BF16), pltpu.VMEM(down, BF16),
                        pltpu.VMEM(up, BF16), pltpu.VMEM(up, BF16), pltpu.VMEM(down, BF16)],
    )
    return pl.pallas_call(
        _moe_expert_kernel,
        out_shape=jax.ShapeDtypeStruct((n_tiles * MOE_TILE, D), F32),
        grid_spec=grid_spec,
        compiler_params=_cparams(("arbitrary",)),
        name="moe_experts",
    )(tile_ea, tile_eb, tile_valid, tile_newa, tile_newb, ui_sorted,
      w_gate, w_up, w_down, w_gate, w_up, w_down)


_PAIR_A = (0, 0, 0, 1, 1, 3)
_PAIR_B = (1, 2, 3, 3, 2, 2)


def _lookup(table, idx):
    n = table.shape[0]
    return jnp.sum(jnp.where(idx[:, None] == jnp.arange(n, dtype=jnp.int32)[None, :], table[None, :], 0), axis=1)


def _moe_plan(ui, counts_vec, n_tiles):
    rows = ui.shape[0]
    cls = ui[:, D].astype(jnp.int32)
    rank = ui[:, D + 3].astype(jnp.int32)
    counts = counts_vec[0, :N_CLASSES].astype(jnp.int32)
    padded = ((counts + MOE_TILE - 1) // MOE_TILE) * MOE_TILE
    ends = jnp.cumsum(padded)
    offs = ends - padded
    uoffs = jnp.cumsum(counts) - counts
    dest = _lookup(offs, cls) + rank
    order = jnp.argsort(cls, stable=True).astype(jnp.int32)
    pos = jnp.arange(n_tiles * MOE_TILE, dtype=jnp.int32)
    pcls = jnp.minimum(jnp.sum((pos[:, None] >= ends[None, :]).astype(jnp.int32), axis=1), N_CLASSES - 1)
    within = pos - _lookup(offs, pcls)
    real = (within < _lookup(counts, pcls)) & (pos < ends[-1])
    src = jnp.take(order, jnp.clip(_lookup(uoffs, pcls) + within, 0, rows - 1), mode="clip")
    src = jnp.where(real, src, pos % rows)
    tile_start = jnp.arange(n_tiles, dtype=jnp.int32) * MOE_TILE
    tile_valid = (tile_start < ends[-1]).astype(jnp.int32)
    last_cls = jnp.max(jnp.where(counts > 0, jnp.arange(N_CLASSES, dtype=jnp.int32), 0))
    tcls = jnp.where(tile_valid == 1, pcls[::MOE_TILE], last_cls)
    grp = tcls // N_PAIRS
    pair = tcls % N_PAIRS
    tile_ea = grp * EXPERTS_PER_GROUP + _lookup(jnp.asarray(_PAIR_A, jnp.int32), pair)
    tile_eb = grp * EXPERTS_PER_GROUP + _lookup(jnp.asarray(_PAIR_B, jnp.int32), pair)
    one = jnp.ones((1,), jnp.int32)
    tile_newa = jnp.concatenate([one, (tile_ea[1:] != tile_ea[:-1]).astype(jnp.int32)])
    tile_newb = jnp.concatenate([one, (tile_eb[1:] != tile_eb[:-1]).astype(jnp.int32)])
    return dest, src, (tile_ea, tile_eb, tile_valid, tile_newa, tile_newb)


def _residual_kernel(x_ref, y_ref, gate_ref, fg_ref, xo_ref, *, final_norm):
    xn = x_ref[...] + gate_ref[...] * y_ref[...]
    if final_norm:
        ms = jnp.mean(xn * xn, axis=-1, keepdims=True)
        xn = xn * lax.rsqrt(ms + NORM_EPS) * fg_ref[...]
    xo_ref[...] = xn


def _residual(x, y_all, gate, fg, per_row, seq_len, row_block0, final_norm):
    rows = x.shape[0]
    tm = _row_tile(rows)
    tps = max(seq_len // tm, 1)
    return pl.pallas_call(
        functools.partial(_residual_kernel, final_norm=final_norm),
        out_shape=jax.ShapeDtypeStruct((rows, D), F32),
        grid=(rows // tm,),
        in_specs=[pl.BlockSpec((tm, D), lambda i: (i, 0)),
                  pl.BlockSpec((tm, D), lambda i: (i + row_block0, 0)),
                  _mod_spec(tm, per_row, tps, 1),
                  pl.BlockSpec((1, D), lambda i: (0, 0))],
        out_specs=pl.BlockSpec((tm, D), lambda i: (i, 0)),
        compiler_params=_cparams(("arbitrary",)),
        name="residual",
    )(x, y_all, gate, fg)


def kernel(x_prompt, x_sample, c_prompt, c_sample, state_a_conv, state_a_h, cache_b_k0, cache_b_v0, cache_b_k1, cache_b_v1, cache_b_k2, cache_b_v2, state_c_conv, norm_mix_g, norm_ffn_g, ada_w, ada_b, final_norm_g, a_w_in_y, a_w_in_x, a_conv_w, a_conv_b, a_gate_r_w, a_gate_r_b, a_gate_i_w, a_gate_i_b, a_lambda, a_w_out, b_w_qkv, b_w_o, conf_w_pw1, conf_b_pw1, conf_w_dw, conf_b_dw, conf_ln_g, conf_ln_b, conf_w_pw2, conf_b_pw2, moe_w_grouter, moe_b_grouter, moe_w_erouter, moe_b_erouter, moe_w_gate, moe_w_up, moe_w_down):
    bsz, seq, _ = x_prompt.shape
    dbsz = x_sample.shape[0]
    depth = ada_w.shape[0]
    rows_p = bsz * seq
    rows_all = rows_p + dbsz
    n_tiles = rows_all // MOE_TILE + N_CLASSES
    caches_k = (cache_b_k0, cache_b_k1, cache_b_k2)
    caches_v = (cache_b_v0, cache_b_v1, cache_b_v2)

    c_rows = -(-(bsz + dbsz) // SUBLANES) * SUBLANES
    c_all = jnp.concatenate([c_prompt, c_sample, jnp.zeros((c_rows - bsz - dbsz, D), F32)], axis=0)
    mods = _ada_mod(c_all, ada_w, ada_b)

    cos_all, sin_all = _rope_table(seq + SUBLANES)
    cos_p, sin_p = cos_all[:seq], sin_all[:seq]
    cos_s = jnp.broadcast_to(cos_all[PAST_LEN:PAST_LEN + 1], (dbsz, HEAD_DIM))
    sin_s = jnp.broadcast_to(sin_all[PAST_LEN:PAST_LEN + 1], (dbsz, HEAD_DIM))

    bf = lambda w: w.astype(BF16)
    a_w_in_y, a_w_in_x, a_w_out = bf(a_w_in_y), bf(a_w_in_x), bf(a_w_out)
    a_gate_r_w, a_gate_i_w = bf(a_gate_r_w), bf(a_gate_i_w)
    b_w_qkv, b_w_o = bf(b_w_qkv), bf(b_w_o)
    conf_w_pw1, conf_w_pw2 = bf(conf_w_pw1), bf(conf_w_pw2)

    xp = x_prompt.reshape(rows_p, D)
    xs = x_sample.reshape(dbsz, D)
    row1 = lambda v: v.reshape(1, -1)
    zero_bias = jnp.zeros((1, D), F32)

    a_conv_p, a_conv_s, a_h_p, a_h_s = [], [], [], []
    kp, vp, ksm, vsm = ([[] for _ in range(N_GROUPS)] for _ in range(4))
    cf_p, cf_s = [], []

    pend_p = pend_s = None
    for i in range(depth):
        kind, j = i % 3, i // 3
        mp = [mods[i, :bsz, k * D:(k + 1) * D].reshape(bsz, 1, D) for k in range(6)]
        ms = [mods[i, bsz:bsz + dbsz, k * D:(k + 1) * D] for k in range(6)]
        g_mix = row1(norm_mix_g[i])
        if kind == 0:
            wts = (a_conv_w[j], row1(a_conv_b[j]), a_gate_r_w[j], row1(a_gate_r_b[j]),
                   a_gate_i_w[j], row1(a_gate_i_b[j]), row1(a_lambda[j]), a_w_out[j])
            ybr, xin, xp = _proj_a(xp, g_mix, mp[0], mp[1], a_w_in_y[j], a_w_in_x[j], False, seq, pend_p)
            xp3, h_last = _rglru_prompt(xin.reshape(bsz, seq, D_RNN), ybr.reshape(bsz, seq, D_RNN),
                                        xp.reshape(bsz, seq, D), mp[2], *wts)
            xp = xp3.reshape(rows_p, D)
            a_conv_p.append(xin.reshape(bsz, seq, D_RNN)[:, seq - (LRU_CONV - 1):])
            a_h_p.append(h_last.reshape(bsz, D_RNN))
            ybr_s, xin_s, xs = _proj_a(xs, g_mix, ms[0], ms[1], a_w_in_y[j], a_w_in_x[j], True, 1, pend_s)
            hist = state_a_conv[j]
            xs, h_new = _rglru_step(xin_s, ybr_s, xs, ms[2], hist.reshape(dbsz, (LRU_CONV - 1) * D_RNN),
                                    state_a_h[j], *wts)
            a_conv_s.append(jnp.concatenate([hist[:, 1:], xin_s[:, None, :]], axis=1))
            a_h_s.append(h_new)
        elif kind == 1:
            qkv, xp = _proj_qkv(xp, g_mix, mp[0], mp[1], cos_p, sin_p, b_w_qkv[j], False, seq, pend_p)
            qkv3 = qkv.reshape(bsz, seq, QKV_WIDTH)
            for g, (win, dil) in enumerate(ATTN_GROUPS):
                keep = min(win, seq)
                kcol = (N_GROUPS + g) * D
                vcol = (2 * N_GROUPS + g) * D
                kp[g].append(qkv3[:, seq - keep:, kcol:kcol + D].reshape(bsz, keep, N_HEADS, HEAD_DIM))
                vp[g].append(qkv3[:, seq - keep:, vcol:vcol + D].reshape(bsz, keep, N_HEADS, HEAD_DIM))
            o_p = _attn_prompt(qkv3).reshape(rows_p, D)
            xp = _out_proj(o_p, b_w_o[j], zero_bias, xp, mp[2], False, seq)
            qkv_s, xs = _proj_qkv(xs, g_mix, ms[0], ms[1], cos_s, sin_s, b_w_qkv[j], True, 1, pend_s)
            o_s = _attn_step(qkv_s, [c[j] for c in caches_k], [c[j] for c in caches_v])
            xs = _out_proj(o_s, b_w_o[j], zero_bias, xs, ms[2], True, 1)
            for g in range(N_GROUPS):
                kcol = (N_GROUPS + g) * D
                vcol = (2 * N_GROUPS + g) * D
                ksm[g].append(qkv_s[:, kcol:kcol + D].reshape(dbsz, 1, N_HEADS, HEAD_DIM))
                vsm[g].append(qkv_s[:, vcol:vcol + D].reshape(dbsz, 1, N_HEADS, HEAD_DIM))
        else:
            wts = (conf_w_dw[j], row1(conf_b_dw[j]), row1(conf_ln_g[j]), row1(conf_ln_b[j]),
                   conf_w_pw2[j], row1(conf_b_pw2[j]))
            b1 = row1(conf_b_pw1[j])
            glu, xp = _proj_glu(xp, g_mix, mp[0], mp[1], conf_w_pw1[j], b1, False, seq, pend_p)
            glu3 = glu.reshape(bsz, seq, D)
            xp = _conf_prompt(glu3, xp.reshape(bsz, seq, D), mp[2], *wts).reshape(rows_p, D)
            cf_p.append(glu3[:, seq - (CONF_WIDTH - 1):])
            glu_s, xs = _proj_glu(xs, g_mix, ms[0], ms[1], conf_w_pw1[j], b1, True, 1, pend_s)
            hist = state_c_conv[j]
            xs = _conf_step(glu_s, hist.reshape(dbsz, (CONF_WIDTH - 1) * D), xs, ms[2], *wts)
            cf_s.append(jnp.concatenate([hist[:, 1:], glu_s[:, None, :]], axis=1))

        g_ffn = row1(norm_ffn_g[i])
        wr = jnp.concatenate([moe_w_grouter[i], moe_w_erouter[i],
                              jnp.zeros((D, LANES - N_EGROUPS - N_EXPERTS), F32)], axis=1)
        wr_hi = wr.astype(BF16)
        wr = jnp.stack([wr_hi, (wr - wr_hi.astype(F32)).astype(BF16)])
        br = jnp.concatenate([moe_b_grouter[i], moe_b_erouter[i],
                              jnp.zeros((LANES - N_EGROUPS - N_EXPERTS,), F32)]).reshape(1, LANES)
        ui_p, cnt_p = _moe_route_prompt(xp, g_ffn, mp[3], mp[4], wr, br, seq, rows_all)
        joint = (ui_p, jnp.pad(cnt_p[:, 0], (0, LANES - ROUTE_ROWS)).reshape(1, LANES))
        ui_all, counts = _moe_route(xs, g_ffn, ms[3], ms[4], wr, br, True, 1, rows_all, rows_p // dbsz, joint)
        dest, src, tiles = _moe_plan(ui_all, counts, n_tiles)
        ui_sorted = jnp.take(ui_all, src, axis=0, mode="clip")
        y_sorted = _moe_experts(i, *tiles, ui_sorted, moe_w_gate, moe_w_up, moe_w_down)
        y_all = jnp.take(y_sorted, dest, axis=0, mode="clip")
        if i < depth - 1:
            pend_p = (y_all, mp[5], 0)
            pend_s = (y_all, ms[5], rows_p // dbsz)
        else:
            fg = row1(final_norm_g)
            xp = _residual(xp, y_all, mp[5], fg, False, seq, 0, True)
            xs = _residual(xs, y_all, ms[5], fg, True, 1, rows_p // dbsz, True)

    y_prompt = xp.reshape(bsz, seq, D)
    y_sample = xs.reshape(dbsz, 1, D)
    return (y_prompt, y_sample,
            jnp.stack(a_conv_p), jnp.stack(a_conv_s), jnp.stack(a_h_p), jnp.stack(a_h_s),
            jnp.stack(kp[0]), jnp.stack(ksm[0]), jnp.stack(vp[0]), jnp.stack(vsm[0]),
            jnp.stack(kp[1]), jnp.stack(ksm[1]), jnp.stack(vp[1]), jnp.stack(vsm[1]),
            jnp.stack(kp[2]), jnp.stack(ksm[2]), jnp.stack(vp[2]), jnp.stack(vsm[2]),
            jnp.stack(cf_p), jnp.stack(cf_s))
```

```python
import functools
import math

import jax
import jax.numpy as jnp
from jax import lax
from jax.experimental import pallas as pl
from jax.experimental.pallas import tpu as pltpu

F32 = jnp.float32
BF16 = jnp.bfloat16

D = 1024
D_RNN = 1280
LRU_BLOCKS = 10
LRU_BLOCK = 128
LRU_CONV = 4
LRU_C = 8.0
ATTN_GROUPS = ((128, 1), (512, 4), (2048, 16))
N_GROUPS = 3
N_HEADS = 8
HEAD_DIM = 128
ROT_DIM = 32
ROPE_THETA = 500000.0
QKV_WIDTH = 9 * D
CONF_WIDTH = 31
N_EGROUPS = 4
EXPERTS_PER_GROUP = 4
N_EXPERTS = 16
D_EXPERT = 512
N_PAIRS = 6
N_CLASSES = N_EGROUPS * N_PAIRS
NORM_EPS = 1e-6
PAST_LEN = 8192

LANES = 128
SUBLANES = 8
ATT_BLOCK = 128
ATT_TOKENS = 2048
QKV_CHUNK = 256
ATT_BATCH = 16
MOE_TILE = 256
VMEM_LIMIT = 56 * 1024 * 1024


def _cparams(sem):
    return pltpu.CompilerParams(dimension_semantics=sem, vmem_limit_bytes=VMEM_LIMIT)


def _sigmoid(x):
    return 1.0 / (1.0 + jnp.exp(-x))


def _silu(x):
    return x * _sigmoid(x)


def _gelu_tanh(x):
    return 0.5 * x * (1.0 + jnp.tanh(math.sqrt(2.0 / math.pi) * (x + 0.044715 * (x * x * x))))


def _softplus(x):
    return jnp.maximum(x, 0.0) + jnp.log1p(jnp.exp(-jnp.abs(x)))


def _norm_mod(x, g, shift, scale):
    ms = jnp.mean(x * x, axis=-1, keepdims=True)
    y = x * lax.rsqrt(ms + NORM_EPS) * g
    return y * (1.0 + scale) + shift


def _ada_kernel(c_ref, w_ref, b_ref, o_ref):
    s = _silu(c_ref[...])
    w = w_ref[...]
    s_hi = s.astype(BF16)
    s_lo = (s - s_hi.astype(F32)).astype(BF16)
    w_hi = w.astype(BF16)
    w_lo = (w - w_hi.astype(F32)).astype(BF16)
    o_ref[...] = (jnp.dot(s_hi, w_hi, preferred_element_type=F32) + jnp.dot(s_lo, w_hi, preferred_element_type=F32)
                  + jnp.dot(s_hi, w_lo, preferred_element_type=F32)) + b_ref[...]


def _ada_mod(c_all, ada_w, ada_b):
    depth, _, n6 = ada_w.shape
    rows = c_all.shape[0]
    tn = 3072
    return pl.pallas_call(
        _ada_kernel,
        out_shape=jax.ShapeDtypeStruct((depth, rows, n6), F32),
        grid=(depth, n6 // tn),
        in_specs=[pl.BlockSpec((rows, D), lambda l, j: (0, 0)),
                  pl.BlockSpec((None, D, tn), lambda l, j: (l, 0, j)),
                  pl.BlockSpec((None, 1, tn), lambda l, j: (l, 0, j))],
        out_specs=pl.BlockSpec((None, rows, tn), lambda l, j: (l, 0, j)),
        compiler_params=_cparams(("arbitrary", "arbitrary")),
        name="ada_mod",
    )(c_all, ada_w, ada_b.reshape(depth, 1, n6))


def _mod_spec(tm, per_row, tiles_per_seq, ngrid):
    if per_row:
        if ngrid == 1:
            return pl.BlockSpec((tm, D), lambda i: (i, 0))
        return pl.BlockSpec((tm, D), lambda i, j: (i, 0))
    if ngrid == 1:
        return pl.BlockSpec((None, 1, D), lambda i: (i // tiles_per_seq, 0, 0))
    return pl.BlockSpec((None, 1, D), lambda i, j: (i // tiles_per_seq, 0, 0))


def _row_tile(rows):
    return 1024 if rows % 1024 == 0 else rows


def _load_x(x_ref, pend_refs, xn_ref):
    if pend_refs is None:
        return x_ref[...]
    y_ref, gate_ref = pend_refs
    x = x_ref[...] + gate_ref[...] * y_ref[...]
    xn_ref[...] = x
    return x


def _pend_specs(pending, tm, per_row, tps, ngrid):
    if pending is None:
        return [], [], [], []
    y_all, gate, row_block0 = pending
    if ngrid == 1:
        yspec = pl.BlockSpec((tm, D), lambda i: (i + row_block0, 0))
        ospec = pl.BlockSpec((tm, D), lambda i: (i, 0))
    else:
        yspec = pl.BlockSpec((tm, D), lambda i, j: (i + row_block0, 0))
        ospec = pl.BlockSpec((tm, D), lambda i, j: (i, 0))
    return [yspec, _mod_spec(tm, per_row, tps, ngrid)], [y_all, gate], [ospec], [D]


def _proj_a_kernel(*refs, fuse):
    x_ref, refs = refs[0], refs[1:]
    pend, refs = (refs[:2], refs[2:]) if fuse else (None, refs)
    g_ref, sh_ref, sc_ref, wy_ref, wx_ref, y_ref, xi_ref = refs[:7]
    x = _load_x(x_ref, pend, refs[7] if fuse else None)
    u = _norm_mod(x, g_ref[...], sh_ref[...], sc_ref[...]).astype(BF16)
    y_ref[...] = _gelu_tanh(jnp.dot(u, wy_ref[...], preferred_element_type=F32))
    xi_ref[...] = jnp.dot(u, wx_ref[...], preferred_element_type=F32)


def _proj_a(x, g, shift, scale, w_y, w_x, per_row, seq_len, pending):
    rows = x.shape[0]
    tm = 512 if rows % 512 == 0 else rows
    tps = max(seq_len // tm, 1)
    mod = _mod_spec(tm, per_row, tps, 1)
    wspec = pl.BlockSpec((D, D_RNN), lambda i: (0, 0))
    ospec = pl.BlockSpec((tm, D_RNN), lambda i: (i, 0))
    pspecs, pargs, pout, pwidth = _pend_specs(pending, tm, per_row, tps, 1)
    outs = pl.pallas_call(
        functools.partial(_proj_a_kernel, fuse=pending is not None),
        out_shape=tuple(jax.ShapeDtypeStruct((rows, n), F32) for n in [D_RNN, D_RNN] + pwidth),
        grid=(rows // tm,),
        in_specs=[pl.BlockSpec((tm, D), lambda i: (i, 0))] + pspecs + [
            pl.BlockSpec((1, D), lambda i: (0, 0)), mod, mod, wspec, wspec],
        out_specs=tuple([ospec, ospec] + pout),
        compiler_params=_cparams(("arbitrary",)),
        name="proj_a",
    )(x, *pargs, g, shift, scale, w_y, w_x)
    return outs[0], outs[1], (outs[2] if pending is not None else x)


def _scan_rows(a, b, h0):
    n, lanes = a.shape
    groups = n // SUBLANES
    a3 = a.reshape(groups, SUBLANES, lanes)
    b3 = b.reshape(groups, SUBLANES, lanes)
    sub = lax.broadcasted_iota(jnp.int32, a3.shape, 1)
    d = 1
    while d < SUBLANES:
        keep = sub >= d
        a_sh = jnp.where(keep, pltpu.roll(a3, d, 1), 1.0)
        b_sh = jnp.where(keep, pltpu.roll(b3, d, 1), 0.0)
        b3 = a3 * b_sh + b3
        a3 = a3 * a_sh
        d *= 2
    hs = []
    h_prev = h0
    for j in range(groups):
        hj = a3[j] * h_prev + b3[j]
        hs.append(hj)
        h_prev = hj[SUBLANES - 1:SUBLANES, :]
    return jnp.concatenate(hs, axis=0), h_prev


def _lru_gates(xc, grw, grb, giw, gib, lam):
    xcb = xc.astype(BF16)
    r = _sigmoid(jnp.dot(xcb, grw, preferred_element_type=F32) + grb)
    i = _sigmoid(jnp.dot(xcb, giw, preferred_element_type=F32) + gib)
    log_a = -LRU_C * r * _softplus(-lam)
    a = jnp.exp(log_a)
    b = jnp.sqrt(jnp.tanh(-log_a) * (a * a + 1.0)) * (i * xc)
    return a, b


def _rglru_kernel(xin_ref, ybr_ref, x_ref, gate_ref, cw_ref, cb_ref, grw_ref, grb_ref, giw_ref, gib_ref,
                  lam_ref, wo_ref, xo_ref, hl_ref, xx_ref, hc_ref, hy_ref, *, tt):
    t = pl.program_id(1)

    @pl.when(t == 0)
    def _():
        xx_ref[0:SUBLANES, :] = jnp.zeros((SUBLANES, D_RNN), F32)
        hc_ref[...] = jnp.zeros((1, D_RNN), F32)

    @pl.when(t > 0)
    def _():
        xx_ref[0:SUBLANES, :] = xx_ref[tt:tt + SUBLANES, :]

    xx_ref[SUBLANES:tt + SUBLANES, :] = xin_ref[...]
    for n in range(LRU_BLOCKS):
        sl = slice(n * LRU_BLOCK, (n + 1) * LRU_BLOCK)
        xc = cb_ref[:, sl]
        for k in range(LRU_CONV):
            off = SUBLANES - (LRU_CONV - 1) + k
            xc = xc + cw_ref[k:k + 1, sl] * xx_ref[off:off + tt, sl]
        a, bt = _lru_gates(xc, grw_ref[n], grb_ref[:, sl], giw_ref[n], gib_ref[:, sl], lam_ref[:, sl])
        h, h_last = _scan_rows(a, bt, hc_ref[:, sl])
        hc_ref[:, sl] = h_last
        hy_ref[:, sl] = (h * ybr_ref[:, sl]).astype(BF16)
    out = jnp.dot(hy_ref[...], wo_ref[...], preferred_element_type=F32)
    xo_ref[...] = x_ref[...] + gate_ref[...] * out
    hl_ref[...] = hc_ref[...]


def _rglru_prompt(xin, ybr, x, gate, cw, cb, grw, grb, giw, gib, lam, wo):
    bsz, seq, _ = xin.shape
    tt = 512
    full2 = lambda b, t: (0, 0)
    full3 = lambda b, t: (0, 0, 0)
    return pl.pallas_call(
        functools.partial(_rglru_kernel, tt=tt),
        out_shape=(jax.ShapeDtypeStruct((bsz, seq, D), F32), jax.ShapeDtypeStruct((bsz, 1, D_RNN), F32)),
        grid=(bsz, seq // tt),
        in_specs=[pl.BlockSpec((None, tt, D_RNN), lambda b, t: (b, t, 0)),
                  pl.BlockSpec((None, tt, D_RNN), lambda b, t: (b, t, 0)),
                  pl.BlockSpec((None, tt, D), lambda b, t: (b, t, 0)),
                  pl.BlockSpec((None, 1, D), lambda b, t: (b, 0, 0)),
                  pl.BlockSpec((LRU_CONV, D_RNN), full2),
                  pl.BlockSpec((1, D_RNN), full2),
                  pl.BlockSpec((LRU_BLOCKS, LRU_BLOCK, LRU_BLOCK), full3),
                  pl.BlockSpec((1, D_RNN), full2),
                  pl.BlockSpec((LRU_BLOCKS, LRU_BLOCK, LRU_BLOCK), full3),
                  pl.BlockSpec((1, D_RNN), full2),
                  pl.BlockSpec((1, D_RNN), full2),
                  pl.BlockSpec((D_RNN, D), full2)],
        out_specs=(pl.BlockSpec((None, tt, D), lambda b, t: (b, t, 0)),
                   pl.BlockSpec((None, 1, D_RNN), lambda b, t: (b, 0, 0))),
        scratch_shapes=[pltpu.VMEM((tt + SUBLANES, D_RNN), F32),
                        pltpu.VMEM((1, D_RNN), F32),
                        pltpu.VMEM((tt, D_RNN), BF16)],
        compiler_params=_cparams(("arbitrary", "arbitrary")),
        name="rglru_prompt",
    )(xin, ybr, x, gate, cw, cb, grw, grb, giw, gib, lam, wo)


def _rglru_step_kernel(xin_ref, ybr_ref, x_ref, gate_ref, hist_ref, hprev_ref, cw_ref, cb_ref, grw_ref, grb_ref,
                       giw_ref, gib_ref, lam_ref, wo_ref, xo_ref, hn_ref, hy_ref):
    for n in range(LRU_BLOCKS):
        sl = slice(n * LRU_BLOCK, (n + 1) * LRU_BLOCK)
        xc = cb_ref[:, sl] + cw_ref[LRU_CONV - 1:LRU_CONV, sl] * xin_ref[:, sl]
        for k in range(LRU_CONV - 1):
            xc = xc + cw_ref[k:k + 1, sl] * hist_ref[:, k * D_RNN + n * LRU_BLOCK:k * D_RNN + (n + 1) * LRU_BLOCK]
        a, bt = _lru_gates(xc, grw_ref[n], grb_ref[:, sl], giw_ref[n], gib_ref[:, sl], lam_ref[:, sl])
        h = a * hprev_ref[:, sl] + bt
        hn_ref[:, sl] = h
        hy_ref[:, sl] = (h * ybr_ref[:, sl]).astype(BF16)
    out = jnp.dot(hy_ref[...], wo_ref[...], preferred_element_type=F32)
    xo_ref[...] = x_ref[...] + gate_ref[...] * out


def _rglru_step(xin, ybr, x, gate, hist, hprev, cw, cb, grw, grb, giw, gib, lam, wo):
    rows = x.shape[0]
    return pl.pallas_call(
        _rglru_step_kernel,
        out_shape=(jax.ShapeDtypeStruct((rows, D), F32), jax.ShapeDtypeStruct((rows, D_RNN), F32)),
        scratch_shapes=[pltpu.VMEM((rows, D_RNN), BF16)],
        compiler_params=pltpu.CompilerParams(vmem_limit_bytes=VMEM_LIMIT),
        name="rglru_step",
    )(xin, ybr, x, gate, hist, hprev, cw, cb, grw, grb, giw, gib, lam, wo)


def _rope_table_kernel(cos_ref, sin_ref):
    half = ROT_DIM // 2
    shape = cos_ref.shape
    pos = lax.broadcasted_iota(jnp.int32, shape, 0).astype(F32)
    lane = lax.broadcasted_iota(jnp.int32, shape, 1)
    fidx = jnp.where(lane < half, lane, lane - half).astype(F32)
    inv_freq = jnp.exp(fidx * (-2.0 * math.log(ROPE_THETA) / ROT_DIM))
    ang = pos * inv_freq
    rot = lane < ROT_DIM
    cos_ref[...] = jnp.where(rot, jnp.cos(ang), 1.0)
    sin_ref[...] = jnp.where(rot, jnp.where(lane < half, -jnp.sin(ang), jnp.sin(ang)), 0.0)


def _rope_table(n_pos):
    return pl.pallas_call(
        _rope_table_kernel,
        out_shape=(jax.ShapeDtypeStruct((n_pos, HEAD_DIM), F32), jax.ShapeDtypeStruct((n_pos, HEAD_DIM), F32)),
        compiler_params=pltpu.CompilerParams(vmem_limit_bytes=VMEM_LIMIT),
        name="rope_table",
    )()


def _proj_qkv_kernel(*refs, tn, fuse):
    x_ref, refs = refs[0], refs[1:]
    pend, refs = (refs[:2], refs[2:]) if fuse else (None, refs)
    g_ref, sh_ref, sc_ref, cos_ref, sin_ref, w_ref, o_ref = refs[:7]
    u_ref = refs[-1]
    j = pl.program_id(1)

    @pl.when(j == 0)
    def _():
        x = _load_x(x_ref, pend, refs[7] if fuse else None)
        u_ref[...] = _norm_mod(x, g_ref[...], sh_ref[...], sc_ref[...]).astype(BF16)

    def chunk(c):
        return jnp.dot(u_ref[...], w_ref[:, c * QKV_CHUNK:(c + 1) * QKV_CHUNK], preferred_element_type=F32)

    @pl.when(j < 2 * N_GROUPS * D // tn)
    def _():
        half = ROT_DIM // 2
        qscale = jnp.where(j < N_GROUPS * D // tn, HEAD_DIM ** -0.5, 1.0).astype(F32)
        cos = cos_ref[...] * qscale
        sin = sin_ref[...] * qscale
        lane = lax.broadcasted_iota(jnp.int32, cos.shape, 1)
        for c in range(tn // QKV_CHUNK):
            acc = chunk(c)
            for h in range(QKV_CHUNK // HEAD_DIM):
                xh = acc[:, h * HEAD_DIM:(h + 1) * HEAD_DIM]
                partner = jnp.where(lane < half, pltpu.roll(xh, HEAD_DIM - half, 1), pltpu.roll(xh, half, 1))
                col = c * QKV_CHUNK + h * HEAD_DIM
                o_ref[:, col:col + HEAD_DIM] = xh * cos + partner * sin

    @pl.when(j >= 2 * N_GROUPS * D // tn)
    def _():
        for c in range(tn // QKV_CHUNK):
            o_ref[:, c * QKV_CHUNK:(c + 1) * QKV_CHUNK] = chunk(c)


def _proj_qkv(x, g, shift, scale, cos_t, sin_t, w, per_row, seq_len, pending):
    rows = x.shape[0]
    tm = _row_tile(rows)
    tn = N_GROUPS * D // 2
    tps = max(seq_len // tm, 1)
    mod = _mod_spec(tm, per_row, tps, 2)
    n_rope = cos_t.shape[0] // tm
    rope = pl.BlockSpec((tm, HEAD_DIM), lambda i, j: (i % n_rope, 0))
    pspecs, pargs, pout, pwidth = _pend_specs(pending, tm, per_row, tps, 2)
    outs = pl.pallas_call(
        functools.partial(_proj_qkv_kernel, tn=tn, fuse=pending is not None),
        out_shape=tuple(jax.ShapeDtypeStruct((rows, n), F32) for n in [QKV_WIDTH] + pwidth),
        grid=(rows // tm, QKV_WIDTH // tn),
        in_specs=[pl.BlockSpec((tm, D), lambda i, j: (i, 0))] + pspecs + [
            pl.BlockSpec((1, D), lambda i, j: (0, 0)),
            mod, mod, rope, rope,
            pl.BlockSpec((D, tn), lambda i, j: (0, j))],
        out_specs=tuple([pl.BlockSpec((tm, tn), lambda i, j: (i, j))] + pout),
        scratch_shapes=[pltpu.VMEM((tm, D), BF16)],
        compiler_params=_cparams(("arbitrary", "arbitrary")),
        name="proj_qkv",
    )(x, *pargs, g, shift, scale, cos_t, sin_t, w)
    return outs[0], (outs[1] if pending is not None else x)


def _rows(start, dil):
    if dil == 1:
        return pl.ds(start, ATT_BLOCK)
    return pl.ds(start, ATT_BLOCK, stride=dil)


def _attn_prompt_kernel(*refs):
    in_refs, o_ref, og_ref, lg_ref = refs[:15], refs[15], refs[16], refs[17]
    n = ATT_BLOCK
    nblk = ATT_BATCH
    ri = lax.broadcasted_iota(jnp.int32, (nblk, n, n), 1)
    cj = lax.broadcasted_iota(jnp.int32, (nblk, n, n), 2)
    neg = -jnp.inf
    has_prev = pl.program_id(1) > 0
    qk_dims = (((2,), (2,)), ((0,), (0,)))
    pv_dims = (((2,), (1,)), ((0,), (0,)))
    ones = jnp.ones((nblk, n, HEAD_DIM), BF16)
    for g, (win, dil) in enumerate(ATTN_GROUPS):
        q_ref, kc_ref, vc_ref, kp_ref, vp_ref = in_refs[5 * g:5 * g + 5]
        all_blocks = [(s, r) for s in range(ATT_TOKENS // win) for r in range(dil)]
        for b0 in range(0, len(all_blocks), nblk):
            blocks = all_blocks[b0:b0 + nblk]

            def load(ref, s, r):
                return ref[_rows(s * win + r, dil), :].astype(BF16)

            q = jnp.stack([load(q_ref, s, r) for s, r in blocks])
            kc = jnp.stack([load(kc_ref, s, r) for s, r in blocks])
            vc = jnp.stack([load(vc_ref, s, r) for s, r in blocks])
            kp = jnp.stack([load(kc_ref, s - 1, r) if s > 0 else kp_ref[_rows(r, dil), :].astype(BF16)
                            for s, r in blocks])
            vp = jnp.stack([load(vc_ref, s - 1, r) if s > 0 else vp_ref[_rows(r, dil), :].astype(BF16)
                            for s, r in blocks])
            sp = lax.dot_general(q, kp, qk_dims, preferred_element_type=F32)
            sc = lax.dot_general(q, kc, qk_dims, preferred_element_type=F32)
            if blocks[0][0] == 0:
                n_first = sum(1 for s, _ in blocks if s == 0)
                blk = lax.broadcasted_iota(jnp.int32, (nblk, n, n), 0)
                sp = jnp.where(blk >= jnp.where(has_prev, 0, n_first), sp, neg)
            sp = jnp.where(cj >= ri, sp, neg)
            sc = jnp.where(cj <= ri, sc, neg)
            m = jnp.max(jnp.maximum(sp, sc), axis=-1, keepdims=True)
            ep = jnp.exp(sp - m).astype(BF16)
            ec = jnp.exp(sc - m).astype(BF16)
            l = (lax.dot_general(ep, ones, pv_dims, preferred_element_type=F32)
                 + lax.dot_general(ec, ones, pv_dims, preferred_element_type=F32))
            o = (lax.dot_general(ep, vp, pv_dims, preferred_element_type=F32)
                 + lax.dot_general(ec, vc, pv_dims, preferred_element_type=F32)) / l
            lse = m + jnp.log(l)
            for idx, (s, r) in enumerate(blocks):
                rows = _rows(s * win + r, dil)
                og_ref[g, rows, :] = o[idx]
                lg_ref[g, rows, :] = lse[idx]
    l0, l1, l2 = lg_ref[0], lg_ref[1], lg_ref[2]
    m = jnp.maximum(jnp.maximum(l0, l1), l2)
    e0, e1, e2 = jnp.exp(l0 - m), jnp.exp(l1 - m), jnp.exp(l2 - m)
    o = (e0 * og_ref[0] + e1 * og_ref[1] + e2 * og_ref[2]) / (e0 + e1 + e2)
    o_ref[...] = o.astype(o_ref.dtype)


def _attn_prompt(qkv):
    bsz, seq, _ = qkv.shape
    tb = ATT_TOKENS
    specs = []
    for g, (win, dil) in enumerate(ATTN_GROUPS):
        per = tb // win

        def cur(which, g=g):
            return pl.BlockSpec((None, tb, HEAD_DIM),
                                lambda b, i, h: (b, i, (which * N_GROUPS + g) * N_HEADS + h))

        def prev(which, g=g, win=win, per=per):
            return pl.BlockSpec((None, win, HEAD_DIM),
                                lambda b, i, h: (b, jnp.maximum(i * per - 1, 0), (which * N_GROUPS + g) * N_HEADS + h))

        specs += [cur(0), cur(1), cur(2), prev(1), prev(2)]
    return pl.pallas_call(
        _attn_prompt_kernel,
        out_shape=jax.ShapeDtypeStruct((bsz, seq, D), BF16),
        grid=(bsz, seq // tb, N_HEADS),
        in_specs=specs,
        out_specs=pl.BlockSpec((None, tb, HEAD_DIM), lambda b, i, h: (b, i, h)),
        scratch_shapes=[pltpu.VMEM((N_GROUPS, tb, HEAD_DIM), F32),
                        pltpu.VMEM((N_GROUPS, tb, HEAD_DIM), F32)],
        compiler_params=_cparams(("arbitrary", "arbitrary", "arbitrary")),
        name="attn_prompt",
    )(*([qkv] * 15))


def _attn_step_kernel(*refs):
    q_refs, kn_refs, vn_refs = refs[0:3], refs[3:6], refs[6:9]
    kc_refs, vc_refs = refs[9:12], refs[12:15]
    o_ref = refs[15]
    outs, lses = [], []
    for g in range(N_GROUPS):
        q = q_refs[g][...]
        kn = kn_refs[g][...]
        vn = vn_refs[g][...]
        s_c = jnp.sum(kc_refs[g][...] * q[None], axis=-1, keepdims=True)
        s_n = jnp.sum(kn * q, axis=-1, keepdims=True)
        m = jnp.maximum(jnp.max(s_c, axis=0), s_n)
        e_c = jnp.exp(s_c - m[None])
        e_n = jnp.exp(s_n - m)
        l = jnp.sum(e_c, axis=0) + e_n
        outs.append((jnp.sum(e_c * vc_refs[g][...], axis=0) + e_n * vn) / l)
        lses.append(m + jnp.log(l))
    m = jnp.maximum(jnp.maximum(lses[0], lses[1]), lses[2])
    es = [jnp.exp(x - m) for x in lses]
    inv = 1.0 / (es[0] + es[1] + es[2])
    o_ref[...] = (es[0] * outs[0] + es[1] * outs[1] + es[2] * outs[2]) * inv


def _attn_step(qkv_s, caches_k, caches_v):
    bsz = qkv_s.shape[0]
    q4 = qkv_s.reshape(bsz, 9, N_HEADS, HEAD_DIM)

    def new_spec(which, g):
        return pl.BlockSpec((None, None, N_HEADS, HEAD_DIM), lambda b: (b, which * N_GROUPS + g, 0, 0))

    specs = [new_spec(w, g) for w in range(3) for g in range(N_GROUPS)]
    cache_spec = pl.BlockSpec((None, ATT_BLOCK, N_HEADS, HEAD_DIM), lambda b: (b, 0, 0, 0))
    ks, vs = [], []
    for g, (win, dil) in enumerate(ATTN_GROUPS):
        ks.append(caches_k[g].reshape(bsz, win // dil, dil * N_HEADS, HEAD_DIM))
        vs.append(caches_v[g].reshape(bsz, win // dil, dil * N_HEADS, HEAD_DIM))
    o = pl.pallas_call(
        _attn_step_kernel,
        out_shape=jax.ShapeDtypeStruct((bsz, N_HEADS, HEAD_DIM), F32),
        grid=(bsz,),
        in_specs=specs + [cache_spec] * 6,
        out_specs=pl.BlockSpec((None, N_HEADS, HEAD_DIM), lambda b: (b, 0, 0)),
        compiler_params=_cparams(("arbitrary",)),
        name="attn_step",
    )(*([q4] * 9), *ks, *vs)
    return o.reshape(bsz, D)


def _out_proj_kernel(a_ref, w_ref, b_ref, x_ref, gate_ref, xo_ref):
    out = jnp.dot(a_ref[...].astype(BF16), w_ref[...], preferred_element_type=F32) + b_ref[...]
    xo_ref[...] = x_ref[...] + gate_ref[...] * out


def _out_proj(a, w, bias, x, gate, per_row, seq_len):
    rows, k = a.shape
    tm = _row_tile(rows)
    tps = max(seq_len // tm, 1)
    return pl.pallas_call(
        _out_proj_kernel,
        out_shape=jax.ShapeDtypeStruct((rows, D), F32),
        grid=(rows // tm,),
        in_specs=[pl.BlockSpec((tm, k), lambda i: (i, 0)),
                  pl.BlockSpec((k, D), lambda i: (0, 0)),
                  pl.BlockSpec((1, D), lambda i: (0, 0)),
                  pl.BlockSpec((tm, D), lambda i: (i, 0)),
                  _mod_spec(tm, per_row, tps, 1)],
        out_specs=pl.BlockSpec((tm, D), lambda i: (i, 0)),
        compiler_params=_cparams(("arbitrary",)),
        name="out_proj",
    )(a, w, bias, x, gate)


def _proj_glu_kernel(*refs, fuse):
    x_ref, refs = refs[0], refs[1:]
    pend, refs = (refs[:2], refs[2:]) if fuse else (None, refs)
    g_ref, sh_ref, sc_ref, wa_ref, wb_ref, ba_ref, bb_ref, o_ref = refs[:8]
    x = _load_x(x_ref, pend, refs[8] if fuse else None)
    u = _norm_mod(x, g_ref[...], sh_ref[...], sc_ref[...]).astype(BF16)
    za = jnp.dot(u, wa_ref[...], preferred_element_type=F32) + ba_ref[...]
    zb = jnp.dot(u, wb_ref[...], preferred_element_type=F32) + bb_ref[...]
    o_ref[...] = za * _sigmoid(zb)


def _proj_glu(x, g, shift, scale, w, bias, per_row, seq_len, pending):
    rows = x.shape[0]
    tm = _row_tile(rows)
    tps = max(seq_len // tm, 1)
    mod = _mod_spec(tm, per_row, tps, 1)
    pspecs, pargs, pout, pwidth = _pend_specs(pending, tm, per_row, tps, 1)
    outs = pl.pallas_call(
        functools.partial(_proj_glu_kernel, fuse=pending is not None),
        out_shape=tuple(jax.ShapeDtypeStruct((rows, n), F32) for n in [D] + pwidth),
        grid=(rows // tm,),
        in_specs=[pl.BlockSpec((tm, D), lambda i: (i, 0))] + pspecs + [
            pl.BlockSpec((1, D), lambda i: (0, 0)),
            mod, mod,
            pl.BlockSpec((D, D), lambda i: (0, 0)),
            pl.BlockSpec((D, D), lambda i: (0, 1)),
            pl.BlockSpec((1, D), lambda i: (0, 0)),
            pl.BlockSpec((1, D), lambda i: (0, 1))],
        out_specs=tuple([pl.BlockSpec((tm, D), lambda i: (i, 0))] + pout),
        compiler_params=_cparams(("arbitrary",)),
        name="proj_glu",
    )(x, *pargs, g, shift, scale, w, w, bias, bias)
    return outs[0], (outs[1] if pending is not None else x)


def _layer_norm(y, g, b):
    mu = jnp.mean(y, axis=-1, keepdims=True)
    yc = y - mu
    var = jnp.mean(yc * yc, axis=-1, keepdims=True)
    return yc * lax.rsqrt(var + NORM_EPS) * g + b


CONF_HALO = 32
CONF_CHUNK = 128


def _conf_kernel(glu_ref, x_ref, gate_ref, wdw_ref, bdw_ref, lg_ref, lb_ref, w2_ref, b2_ref, xo_ref,
                 xx_ref, z_ref, p_ref, *, tt):
    t = pl.program_id(1)

    @pl.when(t == 0)
    def _():
        xx_ref[0:CONF_HALO, :] = jnp.zeros((CONF_HALO, D), F32)

    @pl.when(t > 0)
    def _():
        xx_ref[0:CONF_HALO, :] = xx_ref[tt:tt + CONF_HALO, :]

    xx_ref[CONF_HALO:tt + CONF_HALO, :] = glu_ref[...]
    first = CONF_HALO - (CONF_WIDTH - 1)
    for r0 in range(0, tt, CONF_CHUNK):
        for c in range(D // LANES):
            sl = slice(c * LANES, (c + 1) * LANES)
            acc = jnp.broadcast_to(bdw_ref[:, sl], (CONF_CHUNK, LANES))
            for r in range(SUBLANES):
                taps = [k for k in range(CONF_WIDTH) if (first + k) % SUBLANES == r]
                span = CONF_CHUNK if r == 0 else CONF_CHUNK + SUBLANES
                p = None
                for k in taps:
                    base = r0 + first + k - r
                    term = wdw_ref[k:k + 1, sl] * xx_ref[base:base + span, sl]
                    p = term if p is None else p + term
                if r == 0:
                    acc = acc + p
                else:
                    p_ref[r, :, :] = p
                    acc = acc + p_ref[r, r:r + CONF_CHUNK, :]
            z_ref[r0:r0 + CONF_CHUNK, sl] = acc
    y = _silu(_layer_norm(z_ref[...], lg_ref[...], lb_ref[...]))
    out = jnp.dot(y.astype(BF16), w2_ref[...], preferred_element_type=F32) + b2_ref[...]
    xo_ref[...] = x_ref[...] + gate_ref[...] * out


def _conf_prompt(glu, x, gate, wdw, bdw, lg, lb, w2, b2):
    bsz, seq, _ = glu.shape
    tt = 512
    full2 = lambda b, t: (0, 0)
    vec = pl.BlockSpec((1, D), full2)
    return pl.pallas_call(
        functools.partial(_conf_kernel, tt=tt),
        out_shape=jax.ShapeDtypeStruct((bsz, seq, D), F32),
        grid=(bsz, seq // tt),
        in_specs=[pl.BlockSpec((None, tt, D), lambda b, t: (b, t, 0)),
                  pl.BlockSpec((None, tt, D), lambda b, t: (b, t, 0)),
                  pl.BlockSpec((None, 1, D), lambda b, t: (b, 0, 0)),
                  pl.BlockSpec((CONF_WIDTH, D), full2),
                  vec, vec, vec,
                  pl.BlockSpec((D, D), full2),
                  vec],
        out_specs=pl.BlockSpec((None, tt, D), lambda b, t: (b, t, 0)),
        scratch_shapes=[pltpu.VMEM((tt + CONF_HALO, D), F32),
                        pltpu.VMEM((tt, D), F32),
                        pltpu.VMEM((SUBLANES, CONF_CHUNK + SUBLANES, LANES), F32)],
        compiler_params=_cparams(("arbitrary", "arbitrary")),
        name="conf_prompt",
    )(glu, x, gate, wdw, bdw, lg, lb, w2, b2)


def _conf_step_kernel(glu_ref, hist_ref, x_ref, gate_ref, wdw_ref, bdw_ref, lg_ref, lb_ref, w2_ref, b2_ref, xo_ref):
    acc = bdw_ref[...] + wdw_ref[CONF_WIDTH - 1:CONF_WIDTH, :] * glu_ref[...]
    for k in range(CONF_WIDTH - 1):
        acc = acc + wdw_ref[k:k + 1, :] * hist_ref[:, k * D:(k + 1) * D]
    y = _silu(_layer_norm(acc, lg_ref[...], lb_ref[...]))
    out = jnp.dot(y.astype(BF16), w2_ref[...], preferred_element_type=F32) + b2_ref[...]
    xo_ref[...] = x_ref[...] + gate_ref[...] * out


def _conf_step(glu, hist, x, gate, wdw, bdw, lg, lb, w2, b2):
    return pl.pallas_call(
        _conf_step_kernel,
        out_shape=jax.ShapeDtypeStruct(x.shape, F32),
        compiler_params=pltpu.CompilerParams(vmem_limit_bytes=VMEM_LIMIT),
        name="conf_step",
    )(glu, hist, x, gate, wdw, bdw, lg, lb, w2, b2)


def _moe_route_kernel(x_ref, g_ref, sh_ref, sc_ref, wr_ref, br_ref, tri_ref, cnt0_ref, *rest):
    ui_ref, cnt_ref, carry_ref = rest[-3], rest[-2], rest[-1]

    @pl.when(pl.program_id(0) == 0)
    def _():
        carry_ref[...] = cnt0_ref[...]

    u = _norm_mod(x_ref[...], g_ref[...], sh_ref[...], sc_ref[...])
    ui_ref[:, :D] = u
    u_hi = u.astype(BF16)
    u_lo = (u - u_hi.astype(F32)).astype(BF16)
    logits = (jnp.dot(u_hi, wr_ref[0], preferred_element_type=F32)
              + jnp.dot(u_lo, wr_ref[0], preferred_element_type=F32)
              + jnp.dot(u_hi, wr_ref[1], preferred_element_type=F32)) + br_ref[...]
    lane = lax.broadcasted_iota(jnp.int32, logits.shape, 1)
    neg = -jnp.inf
    big = jnp.int32(LANES)
    gl = jnp.where(lane < N_EGROUPS, logits, neg)
    gmax = jnp.max(gl, axis=-1, keepdims=True)
    gsel = jnp.min(jnp.where(gl == gmax, lane, big), axis=-1, keepdims=True)
    gp = 1.0 / jnp.sum(jnp.exp(gl - gmax), axis=-1, keepdims=True)
    base = N_EGROUPS + EXPERTS_PER_GROUP * gsel
    el = jnp.where((lane >= base) & (lane < base + EXPERTS_PER_GROUP), logits, neg)
    v1 = jnp.max(el, axis=-1, keepdims=True)
    i1 = jnp.min(jnp.where(el == v1, lane, big), axis=-1, keepdims=True)
    el2 = jnp.where(lane == i1, neg, el)
    v2 = jnp.max(el2, axis=-1, keepdims=True)
    i2 = jnp.min(jnp.where(el2 == v2, lane, big), axis=-1, keepdims=True)
    t = jnp.exp(v2 - v1)
    w1 = gp / (1.0 + t)
    w2 = gp * t / (1.0 + t)
    e1 = i1 - base
    e2 = i2 - base
    lo = jnp.minimum(e1, e2)
    hi = jnp.maximum(e1, e2)
    w_lo = jnp.where(e1 < e2, w1, w2)
    w_hi = jnp.where(e1 < e2, w2, w1)
    pair = (lo * (7 - lo)) // 2 + hi - lo - 1
    pair = jnp.where(pair == 3, 4, jnp.where(pair == 4, 3, pair))
    w_a = jnp.where(pair == 5, w_hi, w_lo)
    w_b = jnp.where(pair == 5, w_lo, w_hi)
    cls = gsel * N_PAIRS + pair
    onehot = jnp.where(lane == cls, 1.0, 0.0)
    before = jnp.dot(tri_ref[...], onehot.astype(BF16), preferred_element_type=F32) + carry_ref[...]
    rank = jnp.sum(onehot * before, axis=-1, keepdims=True)
    carry_ref[...] = carry_ref[...] + jnp.sum(onehot, axis=0, keepdims=True)
    cnt_ref[...] = carry_ref[...]
    ui_ref[:, D:] = jnp.where(lane == 0, cls.astype(F32),
                              jnp.where(lane == 1, w_a, jnp.where(lane == 2, w_b, jnp.where(lane == 3, rank, 0.0))))


ROUTE_ROWS = 24


def _moe_route_t_kernel(x_ref, g_ref, sh_ref, sc_ref, wr_ref, br_ref, triu_ref, ui_ref, cnt_ref, carry_ref):
    @pl.when(pl.program_id(0) == 0)
    def _():
        carry_ref[...] = jnp.zeros(carry_ref.shape, F32)

    u = _norm_mod(x_ref[...], g_ref[...], sh_ref[...], sc_ref[...])
    ui_ref[:, :D] = u
    u_hi = u.astype(BF16)
    u_lo = (u - u_hi.astype(F32)).astype(BF16)
    logits = (jnp.dot(u_hi, wr_ref[0], preferred_element_type=F32)
              + jnp.dot(u_lo, wr_ref[0], preferred_element_type=F32)
              + jnp.dot(u_hi, wr_ref[1], preferred_element_type=F32)) + br_ref[...]
    lt = logits.T[:ROUTE_ROWS]
    tokens = lt.shape[1]
    row = lax.broadcasted_iota(jnp.int32, lt.shape, 0)
    neg = -jnp.inf
    big = jnp.int32(LANES)
    gl = jnp.where(row < N_EGROUPS, lt, neg)
    gmax = jnp.max(gl, axis=0, keepdims=True)
    gsel = jnp.min(jnp.where(gl == gmax, row, big), axis=0, keepdims=True)
    gp = 1.0 / jnp.sum(jnp.exp(gl - gmax), axis=0, keepdims=True)
    base = N_EGROUPS + EXPERTS_PER_GROUP * gsel
    el = jnp.where(row >= base, jnp.where(row < base + EXPERTS_PER_GROUP, lt, neg), neg)
    v1 = jnp.max(el, axis=0, keepdims=True)
    i1 = jnp.min(jnp.where(el == v1, row, big), axis=0, keepdims=True)
    el2 = jnp.where(row == i1, neg, el)
    v2 = jnp.max(el2, axis=0, keepdims=True)
    i2 = jnp.min(jnp.where(el2 == v2, row, big), axis=0, keepdims=True)
    t = jnp.exp(v2 - v1)
    w1 = gp / (1.0 + t)
    w2 = gp * t / (1.0 + t)
    e1 = i1 - base
    e2 = i2 - base
    lo = jnp.minimum(e1, e2)
    hi = jnp.maximum(e1, e2)
    w_lo = jnp.where(e1 < e2, w1, w2)
    w_hi = jnp.where(e1 < e2, w2, w1)
    pair = (lo * (7 - lo)) // 2 + hi - lo - 1
    pair = jnp.where(pair == 3, 4, jnp.where(pair == 4, 3, pair))
    w_a = jnp.where(pair == 5, w_hi, w_lo)
    w_b = jnp.where(pair == 5, w_lo, w_hi)
    cls = gsel * N_PAIRS + pair
    onehot = jnp.where(row == cls, 1.0, 0.0)
    before = jnp.dot(onehot.astype(BF16), triu_ref[...], preferred_element_type=F32) + carry_ref[...]
    rank = jnp.sum(onehot * before, axis=0, keepdims=True)
    carry_ref[...] = carry_ref[...] + jnp.sum(onehot, axis=1, keepdims=True)
    cnt_ref[...] = carry_ref[...]
    sub = lax.broadcasted_iota(jnp.int32, (SUBLANES, tokens), 0)
    rec = jnp.where(sub == 0, cls.astype(F32),
                    jnp.where(sub == 1, w_a, jnp.where(sub == 2, w_b, jnp.where(sub == 3, rank, 0.0))))
    rec = jnp.concatenate([rec, jnp.zeros((LANES - SUBLANES, tokens), F32)], axis=0)
    ui_ref[:, D:] = rec.T


def _moe_route_prompt(x, g, shift, scale, wr, br, seq_len, rows_all):
    rows = x.shape[0]
    tm = _row_tile(rows)
    mod = _mod_spec(tm, False, seq_len // tm, 1)
    r = lax.broadcasted_iota(jnp.int32, (tm, tm), 0)
    c = lax.broadcasted_iota(jnp.int32, (tm, tm), 1)
    triu = (r < c).astype(BF16)
    return pl.pallas_call(
        _moe_route_t_kernel,
        out_shape=(jax.ShapeDtypeStruct((rows_all, D + LANES), F32), jax.ShapeDtypeStruct((ROUTE_ROWS, 1), F32)),
        grid=(rows // tm,),
        in_specs=[pl.BlockSpec((tm, D), lambda i: (i, 0)),
                  pl.BlockSpec((1, D), lambda i: (0, 0)),
                  mod, mod,
                  pl.BlockSpec((2, D, LANES), lambda i: (0, 0, 0)),
                  pl.BlockSpec((1, LANES), lambda i: (0, 0)),
                  pl.BlockSpec((tm, tm), lambda i: (0, 0))],
        out_specs=(pl.BlockSpec((tm, D + LANES), lambda i: (i, 0)),
                   pl.BlockSpec((ROUTE_ROWS, 1), lambda i: (0, 0))),
        scratch_shapes=[pltpu.VMEM((ROUTE_ROWS, 1), F32)],
        compiler_params=_cparams(("arbitrary",)),
        name="moe_route_prompt",
    )(x, g, shift, scale, wr, br, triu)


def _moe_route(x, g, shift, scale, wr, br, per_row, seq_len, rows_all, row_block0, prev):
    rows = x.shape[0]
    tm = _row_tile(rows)
    tps = max(seq_len // tm, 1)
    mod = _mod_spec(tm, per_row, tps, 1)
    r = lax.broadcasted_iota(jnp.int32, (tm, tm), 0)
    c = lax.broadcasted_iota(jnp.int32, (tm, tm), 1)
    tri = (c < r).astype(BF16)
    cnt0 = jnp.zeros((1, LANES), F32) if prev is None else prev[1]
    in_specs = [pl.BlockSpec((tm, D), lambda i: (i, 0)),
                pl.BlockSpec((1, D), lambda i: (0, 0)),
                mod, mod,
                pl.BlockSpec((2, D, LANES), lambda i: (0, 0, 0)),
                pl.BlockSpec((1, LANES), lambda i: (0, 0)),
                pl.BlockSpec((tm, tm), lambda i: (0, 0)),
                pl.BlockSpec((1, LANES), lambda i: (0, 0))]
    args = [x, g, shift, scale, wr, br, tri, cnt0]
    aliases = {}
    if prev is not None:
        in_specs += [pl.BlockSpec(memory_space=pl.ANY)]
        args += [prev[0]]
        aliases = {8: 0}
    return pl.pallas_call(
        _moe_route_kernel,
        out_shape=(jax.ShapeDtypeStruct((rows_all, D + LANES), F32), jax.ShapeDtypeStruct((1, LANES), F32)),
        grid=(rows // tm,),
        in_specs=in_specs,
        out_specs=(pl.BlockSpec((tm, D + LANES), lambda i: (i + row_block0, 0)),
                   pl.BlockSpec((1, LANES), lambda i: (0, 0))),
        scratch_shapes=[pltpu.VMEM((1, LANES), F32)],
        input_output_aliases=aliases,
        compiler_params=_cparams(("arbitrary",)),
        name="moe_route",
    )(*args)


def _moe_expert_kernel(ea_ref, eb_ref, valid_ref, newa_ref, newb_ref, x_ref, wga_ref, wua_ref, wda_ref,
                       wgb_ref, wub_ref, wdb_ref, o_ref, ga_ref, ua_ref, da_ref, gb_ref, ub_ref, db_ref):
    t = pl.program_id(0)

    @pl.when(newa_ref[t] == 1)
    def _():
        ga_ref[...] = wga_ref[...].astype(BF16)
        ua_ref[...] = wua_ref[...].astype(BF16)
        da_ref[...] = wda_ref[...].astype(BF16)

    @pl.when(newb_ref[t] == 1)
    def _():
        gb_ref[...] = wgb_ref[...].astype(BF16)
        ub_ref[...] = wub_ref[...].astype(BF16)
        db_ref[...] = wdb_ref[...].astype(BF16)

    @pl.when(valid_ref[t] == 1)
    def _():
        x = x_ref[:, :D].astype(BF16)

        def expert(wg_ref, wu_ref, wd_ref, w):
            hg = jnp.dot(x, wg_ref[...], preferred_element_type=F32)
            hu = jnp.dot(x, wu_ref[...], preferred_element_type=F32)
            act = _silu(hg) * hu * w
            return jnp.dot(act.astype(BF16), wd_ref[...], preferred_element_type=F32)

        o_ref[...] = (expert(ga_ref, ua_ref, da_ref, x_ref[:, D + 1:D + 2])
                      + expert(gb_ref, ub_ref, db_ref, x_ref[:, D + 2:D + 3]))

    @pl.when(valid_ref[t] == 0)
    def _():
        o_ref[...] = jnp.zeros(o_ref.shape, F32)


def _moe_experts(layer, tile_ea, tile_eb, tile_valid, tile_newa, tile_newb, ui_sorted, w_gate, w_up, w_down):
    n_tiles = tile_ea.shape[0]

    def wspec(shape, which):
        if which == 0:
            return pl.BlockSpec((None, None) + shape, lambda t, ea, eb, va, na, nb: (layer, ea[t], 0, 0))
        return pl.BlockSpec((None, None) + shape, lambda t, ea, eb, va, na, nb: (layer, eb[t], 0, 0))

    up = (D, D_EXPERT)
    down = (D_EXPERT, D)
    grid_spec = pltpu.PrefetchScalarGridSpec(
        num_scalar_prefetch=5,
        grid=(n_tiles,),
        in_specs=[pl.BlockSpec((MOE_TILE, D + LANES), lambda t, ea, eb, va, na, nb: (t, 0)),
                  wspec(up, 0), wspec(up, 0), wspec(down, 0),
                  wspec(up, 1), wspec(up, 1), wspec(down, 1)],
        out_specs=pl.BlockSpec((MOE_TILE, D), lambda t, ea, eb, va, na, nb: (t, 0)),
        scratch_shapes=[pltpu.VMEM(up, BF16), pltpu.VMEM(up, BF16), pltpu.VMEM(down, BF16),
                        pltpu.VMEM(up, BF16), pltpu.VMEM(up, BF16), pltpu.VMEM(down, BF16)],
    )
    return pl.pallas_call(
        _moe_expert_kernel,
        out_shape=jax.ShapeDtypeStruct((n_tiles * MOE_TILE, D), F32),
        grid_spec=grid_spec,
        compiler_params=_cparams(("arbitrary",)),
        name="moe_experts",
    )(tile_ea, tile_eb, tile_valid, tile_newa, tile_newb, ui_sorted,
      w_gate, w_up, w_down, w_gate, w_up, w_down)


_PAIR_A = (0, 0, 0, 1, 1, 3)
_PAIR_B = (1, 2, 3, 3, 2, 2)


def _lookup(table, idx):
    n = table.shape[0]
    return jnp.sum(jnp.where(idx[:, None] == jnp.arange(n, dtype=jnp.int32)[None, :], table[None, :], 0), axis=1)


def _moe_plan(ui, counts_vec, n_tiles):
    rows = ui.shape[0]
    cls = ui[:, D].astype(jnp.int32)
    rank = ui[:, D + 3].astype(jnp.int32)
    counts = counts_vec[0, :N_CLASSES].astype(jnp.int32)
    padded = ((counts + MOE_TILE - 1) // MOE_TILE) * MOE_TILE
    ends = jnp.cumsum(padded)
    offs = ends - padded
    uoffs = jnp.cumsum(counts) - counts
    dest = _lookup(offs, cls) + rank
    order = jnp.argsort(cls, stable=True).astype(jnp.int32)
    pos = jnp.arange(n_tiles * MOE_TILE, dtype=jnp.int32)
    pcls = jnp.minimum(jnp.sum((pos[:, None] >= ends[None, :]).astype(jnp.int32), axis=1), N_CLASSES - 1)
    within = pos - _lookup(offs, pcls)
    real = (within < _lookup(counts, pcls)) & (pos < ends[-1])
    src = jnp.take(order, jnp.clip(_lookup(uoffs, pcls) + within, 0, rows - 1), mode="clip")
    src = jnp.where(real, src, pos % rows)
    tile_start = jnp.arange(n_tiles, dtype=jnp.int32) * MOE_TILE
    tile_valid = (tile_start < ends[-1]).astype(jnp.int32)
    last_cls = jnp.max(jnp.where(counts > 0, jnp.arange(N_CLASSES, dtype=jnp.int32), 0))
    tcls = jnp.where(tile_valid == 1, pcls[::MOE_TILE], last_cls)
    grp = tcls // N_PAIRS
    pair = tcls % N_PAIRS
    tile_ea = grp * EXPERTS_PER_GROUP + _lookup(jnp.asarray(_PAIR_A, jnp.int32), pair)
    tile_eb = grp * EXPERTS_PER_GROUP + _lookup(jnp.asarray(_PAIR_B, jnp.int32), pair)
    one = jnp.ones((1,), jnp.int32)
    tile_newa = jnp.concatenate([one, (tile_ea[1:] != tile_ea[:-1]).astype(jnp.int32)])
    tile_newb = jnp.concatenate([one, (tile_eb[1:] != tile_eb[:-1]).astype(jnp.int32)])
    return dest, src, (tile_ea, tile_eb, tile_valid, tile_newa, tile_newb)


def _residual_kernel(x_ref, y_ref, gate_ref, fg_ref, xo_ref, *, final_norm):
    xn = x_ref[...] + gate_ref[...] * y_ref[...]
    if final_norm:
        ms = jnp.mean(xn * xn, axis=-1, keepdims=True)
        xn = xn * lax.rsqrt(ms + NORM_EPS) * fg_ref[...]
    xo_ref[...] = xn


def _residual(x, y_all, gate, fg, per_row, seq_len, row_block0, final_norm):
    rows = x.shape[0]
    tm = _row_tile(rows)
    tps = max(seq_len // tm, 1)
    return pl.pallas_call(
        functools.partial(_residual_kernel, final_norm=final_norm),
        out_shape=jax.ShapeDtypeStruct((rows, D), F32),
        grid=(rows // tm,),
        in_specs=[pl.BlockSpec((tm, D), lambda i: (i, 0)),
                  pl.BlockSpec((tm, D), lambda i: (i + row_block0, 0)),
                  _mod_spec(tm, per_row, tps, 1),
                  pl.BlockSpec((1, D), lambda i: (0, 0))],
        out_specs=pl.BlockSpec((tm, D), lambda i: (i, 0)),
        compiler_params=_cparams(("arbitrary",)),
        name="residual",
    )(x, y_all, gate, fg)


def kernel(x_prompt, x_sample, c_prompt, c_sample, state_a_conv, state_a_h, cache_b_k0, cache_b_v0, cache_b_k1, cache_b_v1, cache_b_k2, cache_b_v2, state_c_conv, norm_mix_g, norm_ffn_g, ada_w, ada_b, final_norm_g, a_w_in_y, a_w_in_x, a_conv_w, a_conv_b, a_gate_r_w, a_gate_r_b, a_gate_i_w, a_gate_i_b, a_lambda, a_w_out, b_w_qkv, b_w_o, conf_w_pw1, conf_b_pw1, conf_w_dw, conf_b_dw, conf_ln_g, conf_ln_b, conf_w_pw2, conf_b_pw2, moe_w_grouter, moe_b_grouter, moe_w_erouter, moe_b_erouter, moe_w_gate, moe_w_up, moe_w_down):
    bsz, seq, _ = x_prompt.shape
    dbsz = x_sample.shape[0]
    depth = ada_w.shape[0]
    rows_p = bsz * seq
    rows_all = rows_p + dbsz
    n_tiles = rows_all // MOE_TILE + N_CLASSES
    caches_k = (cache_b_k0, cache_b_k1, cache_b_k2)
    caches_v = (cache_b_v0, cache_b_v1, cache_b_v2)

    c_rows = -(-(bsz + dbsz) // SUBLANES) * SUBLANES
    c_all = jnp.concatenate([c_prompt, c_sample, jnp.zeros((c_rows - bsz - dbsz, D), F32)], axis=0)
    mods = _ada_mod(c_all, ada_w, ada_b)

    cos_all, sin_all = _rope_table(seq + SUBLANES)
    cos_p, sin_p = cos_all[:seq], sin_all[:seq]
    cos_s = jnp.broadcast_to(cos_all[PAST_LEN:PAST_LEN + 1], (dbsz, HEAD_DIM))
    sin_s = jnp.broadcast_to(sin_all[PAST_LEN:PAST_LEN + 1], (dbsz, HEAD_DIM))

    bf = lambda w: w.astype(BF16)
    a_w_in_y, a_w_in_x, a_w_out = bf(a_w_in_y), bf(a_w_in_x), bf(a_w_out)
    a_gate_r_w, a_gate_i_w = bf(a_gate_r_w), bf(a_gate_i_w)
    b_w_qkv, b_w_o = bf(b_w_qkv), bf(b_w_o)
    conf_w_pw1, conf_w_pw2 = bf(conf_w_pw1), bf(conf_w_pw2)

    xp = x_prompt.reshape(rows_p, D)
    xs = x_sample.reshape(dbsz, D)
    row1 = lambda v: v.reshape(1, -1)
    zero_bias = jnp.zeros((1, D), F32)

    a_conv_p, a_conv_s, a_h_p, a_h_s = [], [], [], []
    kp, vp, ksm, vsm = ([[] for _ in range(N_GROUPS)] for _ in range(4))
    cf_p, cf_s = [], []

    pend_p = pend_s = None
    for i in range(depth):
        kind, j = i % 3, i // 3
        mp = [mods[i, :bsz, k * D:(k + 1) * D].reshape(bsz, 1, D) for k in range(6)]
        ms = [mods[i, bsz:bsz + dbsz, k * D:(k + 1) * D] for k in range(6)]
        g_mix = row1(norm_mix_g[i])
        if kind == 0:
            wts = (a_conv_w[j], row1(a_conv_b[j]), a_gate_r_w[j], row1(a_gate_r_b[j]),
                   a_gate_i_w[j], row1(a_gate_i_b[j]), row1(a_lambda[j]), a_w_out[j])
            ybr, xin, xp = _proj_a(xp, g_mix, mp[0], mp[1], a_w_in_y[j], a_w_in_x[j], False, seq, pend_p)
            xp3, h_last = _rglru_prompt(xin.reshape(bsz, seq, D_RNN), ybr.reshape(bsz, seq, D_RNN),
                                        xp.reshape(bsz, seq, D), mp[2], *wts)
            xp = xp3.reshape(rows_p, D)
            a_conv_p.append(xin.reshape(bsz, seq, D_RNN)[:, seq - (LRU_CONV - 1):])
            a_h_p.append(h_last.reshape(bsz, D_RNN))
            ybr_s, xin_s, xs = _proj_a(xs, g_mix, ms[0], ms[1], a_w_in_y[j], a_w_in_x[j], True, 1, pend_s)
            hist = state_a_conv[j]
            xs, h_new = _rglru_step(xin_s, ybr_s, xs, ms[2], hist.reshape(dbsz, (LRU_CONV - 1) * D_RNN),
                                    state_a_h[j], *wts)
            a_conv_s.append(jnp.concatenate([hist[:, 1:], xin_s[:, None, :]], axis=1))
            a_h_s.append(h_new)
        elif kind == 1:
            qkv, xp = _proj_qkv(xp, g_mix, mp[0], mp[1], cos_p, sin_p, b_w_qkv[j], False, seq, pend_p)
            qkv3 = qkv.reshape(bsz, seq, QKV_WIDTH)
            for g, (win, dil) in enumerate(ATTN_GROUPS):
                keep = min(win, seq)
                kcol = (N_GROUPS + g) * D
                vcol = (2 * N_GROUPS + g) * D
                kp[g].append(qkv3[:, seq - keep:, kcol:kcol + D].reshape(bsz, keep, N_HEADS, HEAD_DIM))
                vp[g].append(qkv3[:, seq - keep:, vcol:vcol + D].reshape(bsz, keep, N_HEADS, HEAD_DIM))
            o_p = _attn_prompt(qkv3).reshape(rows_p, D)
            xp = _out_proj(o_p, b_w_o[j], zero_bias, xp, mp[2], False, seq)
            qkv_s, xs = _proj_qkv(xs, g_mix, ms[0], ms[1], cos_s, sin_s, b_w_qkv[j], True, 1, pend_s)
            o_s = _attn_step(qkv_s, [c[j] for c in caches_k], [c[j] for c in caches_v])
            xs = _out_proj(o_s, b_w_o[j], zero_bias, xs, ms[2], True, 1)
            for g in range(N_GROUPS):
                kcol = (N_GROUPS + g) * D
                vcol = (2 * N_GROUPS + g) * D
                ksm[g].append(qkv_s[:, kcol:kcol + D].reshape(dbsz, 1, N_HEADS, HEAD_DIM))
                vsm[g].append(qkv_s[:, vcol:vcol + D].reshape(dbsz, 1, N_HEADS, HEAD_DIM))
        else:
            wts = (conf_w_dw[j], row1(conf_b_dw[j]), row1(conf_ln_g[j]), row1(conf_ln_b[j]),
                   conf_w_pw2[j], row1(conf_b_pw2[j]))
            b1 = row1(conf_b_pw1[j])
            glu, xp = _proj_glu(xp, g_mix, mp[0], mp[1], conf_w_pw1[j], b1, False, seq, pend_p)
            glu3 = glu.reshape(bsz, seq, D)
            xp = _conf_prompt(glu3, xp.reshape(bsz, seq, D), mp[2], *wts).reshape(rows_p, D)
            cf_p.append(glu3[:, seq - (CONF_WIDTH - 1):])
            glu_s, xs = _proj_glu(xs, g_mix, ms[0], ms[1], conf_w_pw1[j], b1, True, 1, pend_s)
            hist = state_c_conv[j]
            xs = _conf_step(glu_s, hist.reshape(dbsz, (CONF_WIDTH - 1) * D), xs, ms[2], *wts)
            cf_s.append(jnp.concatenate([hist[:, 1:], glu_s[:, None, :]], axis=1))

        g_ffn = row1(norm_ffn_g[i])
        wr = jnp.concatenate([moe_w_grouter[i], moe_w_erouter[i],
                              jnp.zeros((D, LANES - N_EGROUPS - N_EXPERTS), F32)], axis=1)
        wr_hi = wr.astype(BF16)
        wr = jnp.stack([wr_hi, (wr - wr_hi.astype(F32)).astype(BF16)])
        br = jnp.concatenate([moe_b_grouter[i], moe_b_erouter[i],
                              jnp.zeros((LANES - N_EGROUPS - N_EXPERTS,), F32)]).reshape(1, LANES)
        ui_p, cnt_p = _moe_route_prompt(xp, g_ffn, mp[3], mp[4], wr, br, seq, rows_all)
        joint = (ui_p, jnp.pad(cnt_p[:, 0], (0, LANES - ROUTE_ROWS)).reshape(1, LANES))
        ui_all, counts = _moe_route(xs, g_ffn, ms[3], ms[4], wr, br, True, 1, rows_all, rows_p // dbsz, joint)
        dest, src, tiles = _moe_plan(ui_all, counts, n_tiles)
        ui_sorted = jnp.take(ui_all, src, axis=0, mode="clip")
        y_sorted = _moe_experts(i, *tiles, ui_sorted, moe_w_gate, moe_w_up, moe_w_down)
        y_all = jnp.take(y_sorted, dest, axis=0, mode="clip")
        if i < depth - 1:
            pend_p = (y_all, mp[5], 0)
            pend_s = (y_all, ms[5], rows_p // dbsz)
        else:
            fg = row1(final_norm_g)
            xp = _residual(xp, y_all, mp[5], fg, False, seq, 0, True)
            xs = _residual(xs, y_all, ms[5], fg, True, 1, rows_p // dbsz, True)

    y_prompt = xp.reshape(bsz, seq, D)
    y_sample = xs.reshape(dbsz, 1, D)
    return (y_prompt, y_sample,
            jnp.stack(a_conv_p), jnp.stack(a_conv_s), jnp.stack(a_h_p), jnp.stack(a_h_s),
            jnp.stack(kp[0]), jnp.stack(ksm[0]), jnp.stack(vp[0]), jnp.stack(vsm[0]),
            jnp.stack(kp[1]), jnp.stack(ksm[1]), jnp.stack(vp[1]), jnp.stack(vsm[1]),
            jnp.stack(kp[2]), jnp.stack(ksm[2]), jnp.stack(vp[2]), jnp.stack(vsm[2]),
            jnp.stack(cf_p), jnp.stack(cf_s))
```
